```python
import jax, jax.numpy as jnp
from jax import lax
import numpy as np

D_MODEL = 2048
BATCH = 4
SEQ = 2048
DEPTH = 1
DEC_BATCH = 32
DEC_SEQ = 4
PAST_LEN = 16384
PAGE_SIZE = 128

D_MIX = D_MODEL
HEAD_DIM = 128
N_HEADS_A = 8
D_A = N_HEADS_A * HEAD_DIM
CHUNK = 128
N_HEADS_B = 8
N_KV_B = 2
D_B = N_HEADS_B * HEAD_DIM
D_KV_B = N_KV_B * HEAD_DIM
ROT_DIM = HEAD_DIM // 4
ROPE_THETA = 500000.0
N_IDX_HEADS = 16
IDX_DIM = 64
IDX_ROT_DIM = IDX_DIM // 4
TOPK_MAX = 256
Q_BLOCK = 128
N_EXPERTS = 32
TOP_K_EXPERTS = 4
D_FF = 2048
SWIGLU_LIMIT = 7.0
SWIGLU_ALPHA = 1.702
RMS_EPS = 1e-5

OFF_U = 0
OFF_V = OFF_U + D_A
OFF_Q = OFF_V + D_A
OFF_K = OFF_Q + D_B
OFF_VB = OFF_K + D_KV_B
OFF_QI = OFF_VB + D_KV_B
OFF_KI = OFF_QI + N_IDX_HEADS * IDX_DIM
OFF_WI = OFF_KI + IDX_DIM
IN_COLS = OFF_WI + N_IDX_HEADS

kernel_name = "hymba_gmlp_dsa_moe_step"


def rmsnorm(x, g):
    xf = x.astype(jnp.float32)
    y = xf * lax.rsqrt(jnp.mean(xf * xf, axis=-1, keepdims=True) + RMS_EPS)
    return (y * g.astype(jnp.float32)).astype(x.dtype)


def rope(x, pos, rot_dim):
    half = rot_dim // 2
    inv = ROPE_THETA ** (-jnp.arange(half, dtype=jnp.float32) / half)
    ang = pos.astype(jnp.float32)[:, None] * inv[None, :]
    shp = (pos.shape[0],) + (1,) * (x.ndim - 3) + (half,)
    cos = jnp.cos(ang).reshape(shp).astype(x.dtype)
    sin = jnp.sin(ang).reshape(shp).astype(x.dtype)
    x1 = x[..., :half]
    x2 = x[..., half:rot_dim]
    return jnp.concatenate([x1 * cos - x2 * sin, x2 * cos + x1 * sin, x[..., rot_dim:]], axis=-1)


def project(x, pos, norm_mix, w_in, norm_v_a):
    B, T, _ = x.shape
    z = rmsnorm(x, norm_mix) @ w_in
    u = jax.nn.gelu(z[..., OFF_U:OFF_V], approximate=False).reshape(B, T, N_HEADS_A, HEAD_DIM)
    va = rmsnorm(jax.nn.gelu(z[..., OFF_V:OFF_Q], approximate=False), norm_v_a)
    va = va.reshape(B, T, N_HEADS_A, HEAD_DIM)
    q = rope(z[..., OFF_Q:OFF_K].reshape(B, T, N_HEADS_B, HEAD_DIM), pos, ROT_DIM)
    k = rope(z[..., OFF_K:OFF_VB].reshape(B, T, N_KV_B, HEAD_DIM), pos, ROT_DIM)
    vb = z[..., OFF_VB:OFF_QI].reshape(B, T, N_KV_B, HEAD_DIM)
    qi = rope(z[..., OFF_QI:OFF_KI].reshape(B, T, N_IDX_HEADS, IDX_DIM), pos, IDX_ROT_DIM)
    ki = rope(z[..., OFF_KI:OFF_WI], pos, IDX_ROT_DIM)
    wi = z[..., OFF_WI:] * (N_IDX_HEADS ** -0.5)
    return u, va, q, k, vb, qi, ki, wi


def chunk_spatial_gate(u, v, w_s, b_s):
    B, T, H, d = v.shape
    n_chunks = -(-T // CHUNK)
    pad = n_chunks * CHUNK - T
    vp = jnp.pad(v, ((0, 0), (0, pad), (0, 0), (0, 0))).reshape(B, n_chunks, CHUNK, H, d)
    causal = jnp.tril(jnp.ones((CHUNK, CHUNK), dtype=bool))
    wm = jnp.where(causal[None], w_s, 0.0)
    mixed = jnp.einsum('hts,bcshd->bcthd', wm, vp) + b_s.T[None, None, :, :, None]
    mixed = mixed.reshape(B, n_chunks * CHUNK, H, d)[:, :T]
    return u * mixed


def indexer_topk(qi, wi, ki, q_pos, k_top):
    dots = jnp.einsum('bthd,bsd->bths', qi, ki, preferred_element_type=jnp.float32) * (IDX_DIM ** -0.5)
    score = jnp.einsum('bth,bths->bts', wi.astype(jnp.float32), jax.nn.relu(dots))
    s_pos = jnp.arange(ki.shape[1], dtype=jnp.int32)
    admissible = s_pos[None, :] <= q_pos[:, None]
    score = jnp.where(admissible[None], score, -jnp.inf)
    _, idx = lax.top_k(score, k_top)
    valid = idx <= q_pos[None, :, None]
    return idx, valid


def sparse_attend(q, k_sel, v_sel, valid):
    B, T, H, d = q.shape
    G = H // N_KV_B
    qg = q.reshape(B, T, N_KV_B, G, d)
    s = jnp.einsum('btgrd,btkgd->btgrk', qg, k_sel, preferred_element_type=jnp.float32) * (d ** -0.5)
    s = jnp.where(valid[:, :, None, None, :], s, -jnp.inf)
    p = jax.nn.softmax(s, axis=-1).astype(v_sel.dtype)
    o = jnp.einsum('btgrk,btkgd->btgrd', p, v_sel)
    return o.reshape(B, T, H * d)


_gather_rows = jax.vmap(lambda rows, idx: rows[idx])


def prompt_sparse_attention(q, k, v, qi, ki, wi, k_top):
    B, S = q.shape[:2]
    nb = S // Q_BLOCK

    def blockify(a):
        return jnp.moveaxis(a.reshape((B, nb, Q_BLOCK) + a.shape[2:]), 1, 0)

    pos_blocks = jnp.arange(S, dtype=jnp.int32).reshape(nb, Q_BLOCK)

    def one_block(args):
        qb, qib, wb, pb = args
        idx, valid = indexer_topk(qib, wb, ki, pb, k_top)
        return sparse_attend(qb, _gather_rows(k, idx), _gather_rows(v, idx), valid)

    out = lax.map(one_block, (blockify(q), blockify(qi), blockify(wi), pos_blocks))
    return jnp.moveaxis(out, 0, 1).reshape(B, S, N_HEADS_B * HEAD_DIM)


def sample_sparse_attention(q, k_new, v_new, qi, ki_new, wi, cache_k, cache_v, cache_ki, page_table, k_top):
    DB, T = q.shape[:2]
    past = page_table.shape[1] * PAGE_SIZE
    ki_past = cache_ki[page_table].reshape(DB, past, IDX_DIM)
    ki_all = jnp.concatenate([ki_past, ki_new], axis=1)
    pos = past + jnp.arange(T, dtype=jnp.int32)
    idx, valid = indexer_topk(qi, wi, ki_all, pos, k_top)
    in_past = (idx < past)[..., None, None]
    pidx = jnp.minimum(idx, past - 1)
    phys = jnp.take_along_axis(page_table, (pidx // PAGE_SIZE).reshape(DB, -1), axis=1).reshape(pidx.shape)
    off = pidx % PAGE_SIZE
    nidx = jnp.clip(idx - past, 0, T - 1)
    k_sel = jnp.where(in_past, cache_k[phys, off], _gather_rows(k_new, nidx))
    v_sel = jnp.where(in_past, cache_v[phys, off], _gather_rows(v_new, nidx))
    return sparse_attend(q, k_sel, v_sel, valid)


def moe(x, w_router, b_router, w_gate_up, b_gate_up, w_down, b_down):
    logits = (x @ w_router + b_router).astype(jnp.float32)
    top_vals, top_idx = lax.top_k(logits, TOP_K_EXPERTS)
    gates = jax.nn.softmax(top_vals, axis=-1)
    combine = jnp.sum(jax.nn.one_hot(top_idx, N_EXPERTS, dtype=jnp.float32) * gates[..., None], axis=1)
    combine = combine.astype(x.dtype)
    out = jnp.zeros_like(x)
    for e in range(N_EXPERTS):
        gu = x @ w_gate_up[e] + b_gate_up[e]
        gate = jnp.minimum(gu[:, :D_FF], SWIGLU_LIMIT)
        up = jnp.clip(gu[:, D_FF:], -SWIGLU_LIMIT, SWIGLU_LIMIT)
        h = (up + 1.0) * (gate * jax.nn.sigmoid(SWIGLU_ALPHA * gate))
        out = out + combine[:, e:e + 1] * (h @ w_down[e] + b_down[e])
    return out


def finish_layer(x, a_out, b_out, w_out, norm_ffn, w_router, b_router, w_gate_up, b_gate_up, w_down, b_down):
    B, T, D = x.shape
    x = x + jnp.concatenate([a_out.reshape(B, T, D_A), b_out], axis=-1) @ w_out
    h = rmsnorm(x, norm_ffn).reshape(B * T, D)
    return x + moe(h, w_router, b_router, w_gate_up, b_gate_up, w_down, b_down).reshape(B, T, D)


def setup_inputs(seed: int = 0) -> dict:
    key = jax.random.key(seed)
    ks = jax.random.split(key, 20)
    f32 = jnp.float32
    n_pages = PAST_LEN // PAGE_SIZE
    n_pool = (5 * DEC_BATCH * n_pages) // 4

    def nrm(k, shape, scale):
        return jax.random.normal(k, shape, f32) * scale

    page_table = jax.random.permutation(ks[5], n_pool)[:DEC_BATCH * n_pages]
    page_table = page_table.reshape(DEC_BATCH, n_pages).astype(jnp.int32)
    return {
        "x_prompt": nrm(ks[0], (BATCH, SEQ, D_MODEL), 1.0),
        "x_sample": nrm(ks[1], (DEC_BATCH, DEC_SEQ, D_MODEL), 1.0),
        "cache_k": nrm(ks[2], (DEPTH, n_pool, PAGE_SIZE, N_KV_B, HEAD_DIM), 1.0),
        "cache_v": nrm(ks[3], (DEPTH, n_pool, PAGE_SIZE, N_KV_B, HEAD_DIM), 1.0),
        "cache_idx_k": nrm(ks[4], (DEPTH, n_pool, PAGE_SIZE, IDX_DIM), 1.0),
        "page_table": page_table,
        "norm_mix": 1.0 + nrm(ks[6], (DEPTH, D_MODEL), 0.1),
        "w_in": nrm(ks[7], (DEPTH, D_MODEL, IN_COLS), D_MODEL ** -0.5),
        "norm_v_a": 1.0 + nrm(ks[8], (DEPTH, D_A), 0.1),
        "w_spatial": nrm(ks[9], (DEPTH, N_HEADS_A, CHUNK, CHUNK), CHUNK ** -0.5),
        "b_spatial": 1.0 + nrm(ks[10], (DEPTH, N_HEADS_A, CHUNK), 0.1),
        "w_out": nrm(ks[11], (DEPTH, D_MIX, D_MODEL), D_MIX ** -0.5),
        "norm_ffn": 1.0 + nrm(ks[12], (DEPTH, D_MODEL), 0.1),
        "w_router": nrm(ks[13], (DEPTH, D_MODEL, N_EXPERTS), D_MODEL ** -0.5),
        "b_router": nrm(ks[14], (DEPTH, N_EXPERTS), 0.01),
        "w_gate_up": nrm(ks[15], (DEPTH, N_EXPERTS, D_MODEL, 2 * D_FF), D_MODEL ** -0.5),
        "b_gate_up": nrm(ks[16], (DEPTH, N_EXPERTS, 2 * D_FF), 0.02),
        "w_down": nrm(ks[17], (DEPTH, N_EXPERTS, D_FF, D_MODEL), D_FF ** -0.5),
        "b_down": nrm(ks[18], (DEPTH, N_EXPERTS, D_MODEL), 0.02),
        "norm_final": 1.0 + nrm(ks[19], (D_MODEL,), 0.1),
    }


def reference(x_prompt, x_sample, cache_k, cache_v, cache_idx_k, page_table, norm_mix, w_in, norm_v_a,
              w_spatial, b_spatial, w_out, norm_ffn, w_router, b_router, w_gate_up, b_gate_up, w_down,
              b_down, norm_final):
    S = x_prompt.shape[1]
    T = x_sample.shape[1]
    past = page_table.shape[1] * PAGE_SIZE
    k_top_prompt = min(TOPK_MAX, S // 4)
    k_top_sample = min(TOPK_MAX, (past + T) // 4)
    pos_p = jnp.arange(S, dtype=jnp.int32)
    pos_s = past + jnp.arange(T, dtype=jnp.int32)
    xp, xs = x_prompt, x_sample
    kp_l, vp_l, kip_l, ks_l, vs_l, kis_l, vas_l = [], [], [], [], [], [], []
    for l in range(DEPTH):
        ffn_args = (w_out[l], norm_ffn[l], w_router[l], b_router[l], w_gate_up[l], b_gate_up[l],
                    w_down[l], b_down[l])
        u, va, q, k, vb, qi, ki, wi = project(xp, pos_p, norm_mix[l], w_in[l], norm_v_a[l])
        a_out = chunk_spatial_gate(u, va, w_spatial[l], b_spatial[l])
        b_out = prompt_sparse_attention(q, k, vb, qi, ki, wi, k_top_prompt)
        xp = finish_layer(xp, a_out, b_out, *ffn_args)
        kp_l.append(k)
        vp_l.append(vb)
        kip_l.append(ki)
        u, va, q, k, vb, qi, ki, wi = project(xs, pos_s, norm_mix[l], w_in[l], norm_v_a[l])
        a_out = chunk_spatial_gate(u, va, w_spatial[l], b_spatial[l])
        b_out = sample_sparse_attention(q, k, vb, qi, ki, wi, cache_k[l], cache_v[l], cache_idx_k[l],
                                        page_table, k_top_sample)
        xs = finish_layer(xs, a_out, b_out, *ffn_args)
        ks_l.append(k)
        vs_l.append(vb)
        kis_l.append(ki)
        vas_l.append(va)
    y_prompt = rmsnorm(xp, norm_final)
    y_sample = rmsnorm(xs, norm_final)
    new_k_prompt = jnp.stack(kp_l)
    new_v_prompt = jnp.stack(vp_l)
    new_idx_k_prompt = jnp.stack(kip_l)
    new_k_sample = jnp.stack(ks_l)
    new_v_sample = jnp.stack(vs_l)
    new_idx_k_sample = jnp.stack(kis_l)
    new_chunk_v_sample = jnp.stack(vas_l)
    return (y_prompt, y_sample, new_k_prompt, new_v_prompt, new_idx_k_prompt, new_k_sample, new_v_sample,
            new_idx_k_sample, new_chunk_v_sample)
```

```python
import functools

import jax
import jax.numpy as jnp
from jax import lax
from jax.experimental import pallas as pl
from jax.experimental.pallas import tpu as pltpu

HEAD_DIM = 128
N_HEADS_A = 8
D_A = N_HEADS_A * HEAD_DIM
CHUNK = 128
N_HEADS_B = 8
N_KV_B = 2
KV_GROUP = N_HEADS_B // N_KV_B
D_B = N_HEADS_B * HEAD_DIM
D_KV_B = N_KV_B * HEAD_DIM
ROT_DIM = HEAD_DIM // 4
ROPE_THETA = 500000.0
N_IDX_HEADS = 16
IDX_DIM = 64
IDX_ROT_DIM = IDX_DIM // 4
TOPK_MAX = 256
N_EXPERTS = 32
TOP_K_EXPERTS = 4
SWIGLU_LIMIT = 7.0
SWIGLU_ALPHA = 1.702
RMS_EPS = 1e-5
PAGE_SIZE = 128

OFF_U = 0
OFF_V = OFF_U + D_A
OFF_Q = OFF_V + D_A
OFF_K = OFF_Q + D_B
OFF_VB = OFF_K + D_KV_B
OFF_QI = OFF_VB + D_KV_B
OFF_KI = OFF_QI + N_IDX_HEADS * IDX_DIM
OFF_WI = OFF_KI + IDX_DIM
IN_COLS = OFF_WI + N_IDX_HEADS

LANES = 128
SUBLANES = 8
VMEM_LIMIT_BYTES = 56 * 1024 * 1024

IN_COLS_PAD = pl.cdiv(IN_COLS, LANES) * LANES
INT_MIN = -(2 ** 31)
NEG_INF = float("-inf")

F32 = jnp.float32
BF16 = jnp.bfloat16


def _params(semantics):
    return pltpu.CompilerParams(dimension_semantics=semantics, vmem_limit_bytes=VMEM_LIMIT_BYTES)


def _dot(a, b):
    return jnp.dot(a, b, preferred_element_type=F32)


def _dot_nt(a, b):
    return lax.dot_general(a, b, (((1,), (1,)), ((), ())), preferred_element_type=F32)


IN_TM = 256


def _rope_lanes(z, cos, sin_lo, sin_hi, half):
    return z * cos + pltpu.roll(z, LANES - half, 1) * sin_lo + pltpu.roll(z, half, 1) * sin_hi


def _gelu(z):
    return 0.5 * z * (1.0 + lax.erf(z * (2.0 ** -0.5)))


def _in_proj_kernel(x_ref, g_ref, w_ref, nva_ref, wmix_ref, bmix_ref,
                    cq_ref, saq_ref, sbq_ref, ci_ref, sai_ref, sbi_ref,
                    aout_ref, q_ref, k_ref, kb_ref, vb_ref, vbb_ref, qi_ref, ki_ref, kib_ref, wi_ref, va_ref):
    x = x_ref[...]
    ms = jnp.mean(x * x, axis=-1, keepdims=True)
    xb = ((x * lax.rsqrt(ms + RMS_EPS)) * g_ref[...]).astype(BF16)

    u = _gelu(_dot(xb, w_ref[:, OFF_U:OFF_V]))
    gv = _gelu(_dot(xb, w_ref[:, OFF_V:OFF_Q]))
    va = gv * lax.rsqrt(jnp.mean(gv * gv, axis=-1, keepdims=True) + RMS_EPS) * nva_ref[...]
    va_ref[...] = va
    vab = va.astype(BF16)
    for sub in range(IN_TM // CHUNK):
        rows = slice(sub * CHUNK, (sub + 1) * CHUNK)
        for h in range(N_HEADS_A):
            cols = slice(h * HEAD_DIM, (h + 1) * HEAD_DIM)
            mixed = _dot(wmix_ref[h], vab[rows, cols]) + bmix_ref[:, h:h + 1]
            aout_ref[rows, cols] = (u[rows, cols] * mixed).astype(BF16)

    cq, saq, sbq = cq_ref[...], saq_ref[...], sbq_ref[...]
    zq = _dot(xb, w_ref[:, OFF_Q:OFF_K])
    for h in range(N_HEADS_B):
        cols = slice(h * HEAD_DIM, (h + 1) * HEAD_DIM)
        r = _rope_lanes(zq[:, cols], cq, saq, sbq, ROT_DIM // 2)
        q_ref[:, cols] = (r * (HEAD_DIM ** -0.5)).astype(BF16)
    zk = _dot(xb, w_ref[:, OFF_K:OFF_VB])
    for h in range(N_KV_B):
        cols = slice(h * HEAD_DIM, (h + 1) * HEAD_DIM)
        r = _rope_lanes(zk[:, cols], cq, saq, sbq, ROT_DIM // 2)
        k_ref[:, cols] = r
        kb_ref[:, cols] = r.astype(BF16)
    zvb = _dot(xb, w_ref[:, OFF_VB:OFF_QI])
    vb_ref[...] = zvb
    vbb_ref[...] = zvb.astype(BF16)

    ci, sai, sbi = ci_ref[...], sai_ref[...], sbi_ref[...]
    zqi = _dot(xb, w_ref[:, OFF_QI:OFF_KI])
    for p in range(N_IDX_HEADS * IDX_DIM // LANES):
        cols = slice(p * LANES, (p + 1) * LANES)
        qi_ref[:, cols] = _rope_lanes(zqi[:, cols], ci, sai, sbi, IDX_ROT_DIM // 2).astype(BF16)
    zl = _dot(xb, w_ref[:, OFF_KI:IN_COLS_PAD])
    r = _rope_lanes(zl, ci, sai, sbi, IDX_ROT_DIM // 2)
    ki_ref[...] = r[:, :IDX_DIM]
    lane = lax.broadcasted_iota(jnp.int32, r.shape, 1)
    kib_ref[...] = jnp.where(lane < IDX_DIM, r, pltpu.roll(r, IDX_DIM, 1)).astype(BF16)
    wi_ref[...] = zl[:, IDX_DIM:IDX_DIM + N_IDX_HEADS] * (N_IDX_HEADS ** -0.5 * IDX_DIM ** -0.5)


def _rope_tables(pos, rot_dim, head_dim):
    half = rot_dim // 2
    inv = ROPE_THETA ** (-jnp.arange(half, dtype=F32) / half)
    ang = pos.astype(F32)[:, None] * inv[None, :]
    cos, sin = jnp.cos(ang), jnp.sin(ang)
    lane = jnp.arange(LANES) % head_dim
    cos_l = jnp.take(cos, lane % half, axis=1)
    sin_l = jnp.take(sin, lane % half, axis=1)
    c = jnp.where(lane[None] < rot_dim, cos_l, 1.0)
    s_lo = jnp.where(lane[None] < half, -sin_l, 0.0)
    s_hi = jnp.where((lane[None] >= half) & (lane[None] < rot_dim), sin_l, 0.0)
    return c.astype(F32), s_lo.astype(F32), s_hi.astype(F32)


def _in_proj(x_all, n_prompt_tiles, tiles_per_seq, norm_mix, w_in_b, norm_v_a, wmix, bmix, tables):
    n_rows, d_model = x_all.shape
    n_tiles = n_rows // IN_TM
    row = lambda i: (i, 0)
    const = lambda i: (0, 0)
    tab = lambda i: (jnp.where(i < n_prompt_tiles, i % tiles_per_seq, tiles_per_seq), 0)
    grp = lambda i: (jnp.where(i < n_prompt_tiles, 0, 1), 0, 0, 0)
    grp2 = lambda i: (jnp.where(i < n_prompt_tiles, 0, 1), 0, 0)
    tab_spec = pl.BlockSpec((IN_TM, LANES), tab)
    outs = [
        ((n_rows, D_A), BF16),
        ((n_rows, D_B), BF16),
        ((n_rows, D_KV_B), F32),
        ((n_rows, D_KV_B), BF16),
        ((n_rows, D_KV_B), F32),
        ((n_rows, D_KV_B), BF16),
        ((n_rows, N_IDX_HEADS * IDX_DIM), BF16),
        ((n_rows, IDX_DIM), F32),
        ((n_rows, LANES), BF16),
        ((n_rows, N_IDX_HEADS), F32),
        ((n_rows, D_A), F32),
    ]
    return pl.pallas_call(
        _in_proj_kernel,
        grid=(n_tiles,),
        in_specs=[
            pl.BlockSpec((IN_TM, d_model), row),
            pl.BlockSpec((1, d_model), const),
            pl.BlockSpec((d_model, IN_COLS_PAD), const, pipeline_mode=pl.Buffered(1)),
            pl.BlockSpec((1, D_A), const),
            pl.BlockSpec((None, N_HEADS_A, CHUNK, CHUNK), grp),
            pl.BlockSpec((None, CHUNK, N_HEADS_A), grp2),
            tab_spec, tab_spec, tab_spec, tab_spec, tab_spec, tab_spec,
        ],
        out_specs=[pl.BlockSpec((IN_TM, s[1]), row) for s, _ in outs],
        out_shape=[jax.ShapeDtypeStruct(s, d) for s, d in outs],
        compiler_params=_params(("parallel",)),
        name="in_proj",
    )(x_all, norm_mix, w_in_b, norm_v_a, wmix, bmix, *tables)


def _order_key(score):
    bits = pltpu.bitcast(score, jnp.int32)
    return bits ^ ((bits >> 31) & jnp.int32(0x7FFFFFFF))


def _kth_largest_key(key, k_top):
    def body(it, t):
        cand = t + lax.shift_left(jnp.int32(1), jnp.int32(31) - it)
        cnt = jnp.sum((key >= cand).astype(F32), axis=1, keepdims=True)
        return jnp.where(cnt >= k_top, cand, t)

    t0 = jnp.full((key.shape[0], 1), INT_MIN, jnp.int32)
    return lax.fori_loop(0, 32, body, t0)


def _topk_select(key, k_top, tri):
    thr = _kth_largest_key(key, k_top)
    above = key > thr
    need = k_top - jnp.sum(above.astype(F32), axis=1, keepdims=True)
    tie = key == thr
    pieces = []
    run = jnp.zeros_like(need)
    for j in range(key.shape[1] // LANES):
        cols = slice(j * LANES, (j + 1) * LANES)
        tie_j = tie[:, cols]
        prefix = _dot(tie_j.astype(BF16), tri) + run
        run = prefix[:, LANES - 1:LANES]
        pieces.append(jnp.where(above[:, cols] | (tie_j & (prefix <= need)), 1.0, 0.0))
    return jnp.concatenate(pieces, axis=1)


def _tri_matrix():
    r = lax.broadcasted_iota(jnp.int32, (LANES, LANES), 0)
    c = lax.broadcasted_iota(jnp.int32, (LANES, LANES), 1)
    return (r <= c).astype(BF16)


PA_TQ = 128


def _prompt_attn_kernel(k_top, qi_ref, wi_ref, kib_ref, q_ref, kb_ref, vbb_ref, o_ref):
    seq = kb_ref.shape[0]
    i = pl.program_id(1)
    qi = qi_ref[...]
    wi = wi_ref[...]
    ki2 = kib_ref[...]
    lane = lax.broadcasted_iota(jnp.int32, (PA_TQ, LANES), 1)
    score = jnp.zeros((PA_TQ, seq), F32)
    for h in range(N_IDX_HEADS):
        pair = qi[:, (h // 2) * LANES:(h // 2 + 1) * LANES]
        keep = (lane < IDX_DIM) if h % 2 == 0 else (lane >= IDX_DIM)
        d = _dot_nt(jnp.where(keep, pair, jnp.zeros_like(pair)), ki2)
        score = score + wi[:, h:h + 1] * jnp.maximum(d, 0.0)
    q_pos = i * PA_TQ + lax.broadcasted_iota(jnp.int32, (PA_TQ, seq), 0)
    s_pos = lax.broadcasted_iota(jnp.int32, (PA_TQ, seq), 1)
    causal = s_pos <= q_pos
    key = jnp.where(causal, _order_key(score), INT_MIN)
    sel = _topk_select(key, k_top, _tri_matrix())
    bias = jnp.where(causal & (sel > 0.0), 0.0, NEG_INF)

    q = q_ref[...]
    for g in range(N_KV_B):
        q4 = jnp.concatenate(
            [q[:, (g * KV_GROUP + r) * HEAD_DIM:(g * KV_GROUP + r + 1) * HEAD_DIM] for r in range(KV_GROUP)], axis=0)
        kg = kb_ref[:, g * HEAD_DIM:(g + 1) * HEAD_DIM]
        vg = vbb_ref[:, g * HEAD_DIM:(g + 1) * HEAD_DIM]
        s = _dot_nt(q4, kg).reshape(KV_GROUP, PA_TQ, seq) + bias[None]
        m = jnp.max(s, axis=-1, keepdims=True)
        p = jnp.exp(s - m)
        l = jnp.sum(p, axis=-1, keepdims=True)
        o = _dot(p.reshape(KV_GROUP * PA_TQ, seq).astype(BF16), vg).reshape(KV_GROUP, PA_TQ, HEAD_DIM) / l
        for r in range(KV_GROUP):
            o_ref[:, (g * KV_GROUP + r) * HEAD_DIM:(g * KV_GROUP + r + 1) * HEAD_DIM] = o[r].astype(BF16)


def _prompt_attn(n_batch, seq, k_top, qi, wi, kib, q, kb, vbb):
    nq = seq // PA_TQ
    qrow = lambda b, i: (b * nq + i, 0)
    krow = lambda b, i: (b, 0)
    return pl.pallas_call(
        functools.partial(_prompt_attn_kernel, k_top),
        grid=(n_batch, nq),
        in_specs=[
            pl.BlockSpec((PA_TQ, N_IDX_HEADS * IDX_DIM), qrow),
            pl.BlockSpec((PA_TQ, N_IDX_HEADS), qrow),
            pl.BlockSpec((seq, LANES), krow),
            pl.BlockSpec((PA_TQ, D_B), qrow),
            pl.BlockSpec((seq, D_KV_B), krow),
            pl.BlockSpec((seq, D_KV_B), krow),
        ],
        out_specs=pl.BlockSpec((PA_TQ, D_B), qrow),
        out_shape=jax.ShapeDtypeStruct((n_batch * seq, D_B), BF16),
        compiler_params=_params(("parallel", "parallel")),
        name="prompt_attn",
    )(qi, wi, kib, q, kb, vbb)


SP_PAGES = 16
SP_KEYS = SP_PAGES * PAGE_SIZE
SP_TPAD = SUBLANES


def _page_specs(width, n_steps):
    def make(p):
        def index(b, j, pt):
            return (pt[b, jnp.minimum(j, n_steps - 1) * SP_PAGES + p], 0, 0)
        return pl.BlockSpec((None, PAGE_SIZE, width), index)
    return [make(p) for p in range(SP_PAGES)]


def _sample_scores_kernel(n_steps, pt_ref, qs_ref, ws_ref, kin_ref, *refs):
    page_refs, o_ref = refs[:SP_PAGES], refs[SP_PAGES]
    j = pl.program_id(1)
    qs = qs_ref[...]
    ws = ws_ref[...]

    def head_sum(d):
        w = ws * jnp.maximum(d, 0.0)
        return jnp.sum(w.reshape(N_IDX_HEADS, SP_TPAD, d.shape[1]), axis=0)

    @pl.when(j < n_steps)
    def _():
        keys = jnp.concatenate([r[...] for r in page_refs], axis=0).astype(BF16)
        o_ref[...] = head_sum(_dot_nt(qs, keys))

    @pl.when(j == n_steps)
    def _():
        d = _dot_nt(qs, kin_ref[...])
        o_ref[...] = jnp.concatenate(
            [head_sum(d), jnp.zeros((SP_TPAD, SP_KEYS - d.shape[1]), F32)], axis=1)


def _sample_scores(page_table, qs, ws, ki_new, cache_idx_k):
    n_db, n_pages = page_table.shape
    n_steps = n_pages // SP_PAGES
    rows = N_IDX_HEADS * SP_TPAD
    per_b = lambda b, j, pt: (b, 0, 0)
    grid_spec = pltpu.PrefetchScalarGridSpec(
        num_scalar_prefetch=1,
        grid=(n_db, n_steps + 1),
        in_specs=[
            pl.BlockSpec((None, rows, IDX_DIM), per_b),
            pl.BlockSpec((None, rows, 1), per_b),
            pl.BlockSpec((None, LANES, IDX_DIM), per_b),
        ] + _page_specs(IDX_DIM, n_steps),
        out_specs=pl.BlockSpec((None, SP_TPAD, SP_KEYS), lambda b, j, pt: (b, 0, j)),
    )
    return pl.pallas_call(
        functools.partial(_sample_scores_kernel, n_steps),
        grid_spec=grid_spec,
        out_shape=jax.ShapeDtypeStruct((n_db, SP_TPAD, (n_steps + 1) * SP_KEYS), F32),
        compiler_params=_params(("parallel", "arbitrary")),
        name="sample_scores",
    )(page_table, qs, ws, ki_new, *([cache_idx_k] * SP_PAGES))


SM_ROWS = 64


def _sample_mask_kernel(k_top, past, n_new, s_ref, o_ref):
    score = s_ref[...]
    t = lax.broadcasted_iota(jnp.int32, score.shape, 0) % SP_TPAD
    s_pos = lax.broadcasted_iota(jnp.int32, score.shape, 1)
    admissible = s_pos <= past + t
    key = jnp.where(admissible, _order_key(score), INT_MIN)
    sel = _topk_select(key, k_top, _tri_matrix())
    o_ref[...] = jnp.where((admissible & (sel > 0.0)) | (t >= n_new), 0.0, NEG_INF)


def _sample_mask(scores2d, k_top, past, n_new):
    n_rows, width = scores2d.shape
    return pl.pallas_call(
        functools.partial(_sample_mask_kernel, k_top, past, n_new),
        grid=(n_rows // SM_ROWS,),
        in_specs=[pl.BlockSpec((SM_ROWS, width), lambda i: (i, 0))],
        out_specs=pl.BlockSpec((SM_ROWS, width), lambda i: (i, 0)),
        out_shape=jax.ShapeDtypeStruct((n_rows, width), F32),
        compiler_params=_params(("parallel",)),
        name="sample_mask",
    )(scores2d)


SA_ROWS = KV_GROUP * SP_TPAD


def _sample_attn_kernel(n_steps, pt_ref, q_ref, bias_ref, kn_ref, vn_ref, *refs):
    k_refs, v_refs = refs[:SP_PAGES], refs[SP_PAGES:2 * SP_PAGES]
    o_ref, m_ref, l_ref, acc_ref = refs[2 * SP_PAGES:]
    j = pl.program_id(1)

    @pl.when(j == 0)
    def _():
        m_ref[...] = jnp.full(m_ref.shape, NEG_INF, F32)
        l_ref[...] = jnp.zeros(l_ref.shape, F32)
        acc_ref[...] = jnp.zeros(acc_ref.shape, F32)

    def update(g, kg, vg, bias):
        s = _dot_nt(q_ref[g], kg) + jnp.concatenate([bias] * KV_GROUP, axis=0)
        m_old = m_ref[g]
        m_new = jnp.maximum(m_old, jnp.max(s, axis=-1, keepdims=True))
        m_safe = jnp.where(m_new == NEG_INF, 0.0, m_new)
        alpha = jnp.exp(m_old - m_safe)
        p = jnp.exp(s - m_safe)
        l_ref[g] = alpha * l_ref[g] + jnp.sum(p, axis=-1, keepdims=True)
        acc_ref[g] = alpha * acc_ref[g] + _dot(p.astype(BF16), vg)
        m_ref[g] = m_new

    @pl.when(j < n_steps)
    def _():
        bias = bias_ref[...]
        for g in range(N_KV_B):
            cols = slice(g * HEAD_DIM, (g + 1) * HEAD_DIM)
            kg = jnp.concatenate([r[:, cols] for r in k_refs], axis=0).astype(BF16)
            vg = jnp.concatenate([r[:, cols] for r in v_refs], axis=0).astype(BF16)
            update(g, kg, vg, bias)

    @pl.when(j == n_steps)
    def _():
        bias = bias_ref[:, :LANES]
        for g in range(N_KV_B):
            cols = slice(g * HEAD_DIM, (g + 1) * HEAD_DIM)
            update(g, kn_ref[:, cols], vn_ref[:, cols], bias)
            o_ref[g] = acc_ref[g] / l_ref[g]


def _sample_attn(page_table, qs, bias, k_new, v_new, cache_k, cache_v):
    n_db, n_pages = page_table.shape
    n_steps = n_pages // SP_PAGES
    per_b3 = lambda b, j, pt: (b, 0, 0)
    per_b4 = lambda b, j, pt: (b, 0, 0, 0)
    grid_spec = pltpu.PrefetchScalarGridSpec(
        num_scalar_prefetch=1,
        grid=(n_db, n_steps + 1),
        in_specs=[
            pl.BlockSpec((None, N_KV_B, SA_ROWS, HEAD_DIM), per_b4),
            pl.BlockSpec((None, SP_TPAD, SP_KEYS), lambda b, j, pt: (b, 0, j)),
            pl.BlockSpec((None, LANES, D_KV_B), per_b3),
            pl.BlockSpec((None, LANES, D_KV_B), per_b3),
        ] + 2 * _page_specs(D_KV_B, n_steps),
        out_specs=pl.BlockSpec((None, N_KV_B, SA_ROWS, HEAD_DIM), per_b4),
        scratch_shapes=[
            pltpu.VMEM((N_KV_B, SA_ROWS, 1), F32),
            pltpu.VMEM((N_KV_B, SA_ROWS, 1), F32),
            pltpu.VMEM((N_KV_B, SA_ROWS, HEAD_DIM), F32),
        ],
    )
    return pl.pallas_call(
        functools.partial(_sample_attn_kernel, n_steps),
        grid_spec=grid_spec,
        out_shape=jax.ShapeDtypeStruct((n_db, N_KV_B, SA_ROWS, HEAD_DIM), F32),
        compiler_params=_params(("parallel", "arbitrary")),
        name="sample_attn",
    )(page_table, qs, bias, k_new, v_new, *([cache_k] * SP_PAGES), *([cache_v] * SP_PAGES))


OR_TM = 256


def _split_bf16(x):
    hi = x.astype(BF16)
    return hi, (x - hi.astype(F32)).astype(BF16)


def _out_router_kernel(a_ref, b_ref, x_ref, wo_ref, nf_ref, wrh_ref, wrl_ref, br_ref,
                       xmid_ref, h_ref, idx_ref, gate_ref):
    x_mid = x_ref[...] + _dot(a_ref[...], wo_ref[:D_A, :]) + _dot(b_ref[...], wo_ref[D_A:, :])
    xmid_ref[...] = x_mid
    h = (x_mid * lax.rsqrt(jnp.mean(x_mid * x_mid, axis=-1, keepdims=True) + RMS_EPS)) * nf_ref[...]
    h_ref[...] = h
    h_hi, h_lo = _split_bf16(h)
    wr_hi = wrh_ref[...]
    logits = _dot(h_hi, wr_hi) + _dot(h_lo, wr_hi) + _dot(h_hi, wrl_ref[...]) + br_ref[...]
    lane = lax.broadcasted_iota(jnp.int32, logits.shape, 1)
    vals, idxs = [], []
    for _ in range(TOP_K_EXPERTS):
        m = jnp.max(logits, axis=-1, keepdims=True)
        i = jnp.min(jnp.where(logits == m, lane, LANES), axis=-1, keepdims=True)
        vals.append(m)
        idxs.append(i)
        logits = jnp.where(lane == i, NEG_INF, logits)
    exps = [jnp.exp(v - vals[0]) for v in vals]
    total = exps[0]
    for e in exps[1:]:
        total = total + e
    idx_out = jnp.zeros(logits.shape, jnp.int32)
    gate_out = jnp.zeros(logits.shape, F32)
    for k in range(TOP_K_EXPERTS):
        idx_out = jnp.where(lane == k, idxs[k], idx_out)
        gate_out = jnp.where(lane == k, exps[k] / total, gate_out)
    idx_ref[...] = idx_out
    gate_ref[...] = gate_out


def _out_router(a_out, b_out, x_all, w_out_b, norm_ffn, wr_hi, wr_lo, br_pad):
    n_rows, d_model = x_all.shape
    row = lambda i: (i, 0)
    const = lambda i: (0, 0)
    return pl.pallas_call(
        _out_router_kernel,
        grid=(n_rows // OR_TM,),
        in_specs=[
            pl.BlockSpec((OR_TM, D_A), row),
            pl.BlockSpec((OR_TM, D_B), row),
            pl.BlockSpec((OR_TM, d_model), row),
            pl.BlockSpec((D_A + D_B, d_model), const),
            pl.BlockSpec((1, d_model), const),
            pl.BlockSpec((d_model, LANES), const),
            pl.BlockSpec((d_model, LANES), const),
            pl.BlockSpec((1, LANES), const),
        ],
        out_specs=[
            pl.BlockSpec((OR_TM, d_model), row),
            pl.BlockSpec((OR_TM, d_model), row),
            pl.BlockSpec((OR_TM, LANES), row),
            pl.BlockSpec((OR_TM, LANES), row),
        ],
        out_shape=[
            jax.ShapeDtypeStruct((n_rows, d_model), F32),
            jax.ShapeDtypeStruct((n_rows, d_model), F32),
            jax.ShapeDtypeStruct((n_rows, LANES), jnp.int32),
            jax.ShapeDtypeStruct((n_rows, LANES), F32),
        ],
        compiler_params=_params(("parallel",)),
        name="out_router",
    )(a_out, b_out, x_all, w_out_b, norm_ffn, wr_hi, wr_lo, br_pad)


MOE_TM = 256
MOE_UNIT_TILES = 5
MOE_R = MOE_TM * MOE_UNIT_TILES
MOE_TF = 256
DISPATCH_CHUNK = 1280
COMBINE_TM = 128


def _row_copy(src_ref, src_row, dst_ref, dst_row, sem):
    return pltpu.make_async_copy(src_ref.at[pl.ds(src_row, 1), :], dst_ref.at[pl.ds(dst_row, 1), :], sem)


def _dispatch_kernel(pos_ref, h_ref, xs_ref, sem):
    base = pl.program_id(0) * DISPATCH_CHUNK

    def start(r, c):
        _row_copy(h_ref, (base + r) // TOP_K_EXPERTS, xs_ref, pos_ref[0, r], sem).start()
        return c

    def wait(r, c):
        _row_copy(h_ref, (base + r) // TOP_K_EXPERTS, xs_ref, pos_ref[0, r], sem).wait()
        return c

    lax.fori_loop(0, DISPATCH_CHUNK, start, 0)
    lax.fori_loop(0, DISPATCH_CHUNK, wait, 0)


def _dispatch(pos, h, n_slots):
    n_assign = pos.shape[0]
    n_chunks = n_assign // DISPATCH_CHUNK
    return pl.pallas_call(
        _dispatch_kernel,
        grid=(n_chunks,),
        in_specs=[
            pl.BlockSpec((None, 1, DISPATCH_CHUNK), lambda i: (i, 0, 0), memory_space=pltpu.SMEM),
            pl.BlockSpec(memory_space=pl.ANY),
        ],
        out_specs=pl.BlockSpec(memory_space=pl.ANY),
        out_shape=jax.ShapeDtypeStruct((n_slots, h.shape[1]), h.dtype),
        scratch_shapes=[pltpu.SemaphoreType.DMA],
        compiler_params=_params(("arbitrary",)),
        name="dispatch",
    )(pos.reshape(n_chunks, 1, DISPATCH_CHUNK), h)


def _moe_kernel(ue_ref, ur_ref, ub_ref, x_ref, wg_ref, wu_ref, bg_ref, bu_ref, wd_ref, bd_ref,
                y_ref, wgb_ref, wub_ref, wdb_ref):
    u, f = pl.program_id(0), pl.program_id(1)
    rows = ur_ref[u]

    @pl.when(rows > 0)
    def _():
        wgb_ref[...] = wg_ref[...].astype(BF16)
        wub_ref[...] = wu_ref[...].astype(BF16)
        wdb_ref[...] = wd_ref[...].astype(BF16)
        n_tiles = pl.cdiv(rows, MOE_TM)

        def tile(m, c):
            r0 = pl.multiple_of(m * MOE_TM, MOE_TM)
            row = r0 + lax.broadcasted_iota(jnp.int32, (MOE_TM, 1), 0)
            xm = jnp.where(row < rows, x_ref[pl.ds(r0, MOE_TM), :], 0.0).astype(BF16)
            gate = jnp.minimum(_dot(xm, wgb_ref[...]) + bg_ref[...], SWIGLU_LIMIT)
            up = jnp.clip(_dot(xm, wub_ref[...]) + bu_ref[...], -SWIGLU_LIMIT, SWIGLU_LIMIT)
            act = (up + 1.0) * (gate * jax.nn.sigmoid(SWIGLU_ALPHA * gate))
            contrib = _dot(act.astype(BF16), wdb_ref[...])

            @pl.when(f == 0)
            def _():
                y_ref[pl.ds(r0, MOE_TM), :] = contrib + bd_ref[...]

            @pl.when(f > 0)
            def _():
                y_ref[pl.ds(r0, MOE_TM), :] += contrib
            return c

        lax.fori_loop(0, n_tiles, tile, 0)

        @pl.when(f == 0)
        def _():
            def clear(m, c):
                y_ref[pl.ds(pl.multiple_of(m * MOE_TM, MOE_TM), MOE_TM), :] = jnp.zeros(
                    (MOE_TM, y_ref.shape[1]), F32)
                return c
            lax.fori_loop(n_tiles, MOE_UNIT_TILES, clear, 0)


def _moe_experts(unit_expert, unit_rows, unit_block, xs, w_gate_up, b_gate_up, w_down, b_down):
    n_units = unit_expert.shape[0]
    d_model = xs.shape[1]
    d_ff = w_down.shape[1]
    n_f = d_ff // MOE_TF

    def f_eff(u, f, ur):
        return jnp.where(ur[u] > 0, f, n_f - 1)

    grid_spec = pltpu.PrefetchScalarGridSpec(
        num_scalar_prefetch=3,
        grid=(n_units, n_f),
        in_specs=[
            pl.BlockSpec((MOE_R, d_model), lambda u, f, ue, ur, ub: (ub[u], 0), pipeline_mode=pl.Buffered(1)),
            pl.BlockSpec((None, d_model, MOE_TF), lambda u, f, ue, ur, ub: (ue[u], 0, f_eff(u, f, ur))),
            pl.BlockSpec((None, d_model, MOE_TF), lambda u, f, ue, ur, ub: (ue[u], 0, n_f + f_eff(u, f, ur))),
            pl.BlockSpec((None, 1, MOE_TF), lambda u, f, ue, ur, ub: (ue[u], 0, f_eff(u, f, ur))),
            pl.BlockSpec((None, 1, MOE_TF), lambda u, f, ue, ur, ub: (ue[u], 0, n_f + f_eff(u, f, ur))),
            pl.BlockSpec((None, MOE_TF, d_model), lambda u, f, ue, ur, ub: (ue[u], f_eff(u, f, ur), 0)),
            pl.BlockSpec((None, 1, d_model), lambda u, f, ue, ur, ub: (ue[u], 0, 0)),
        ],
        out_specs=pl.BlockSpec((MOE_R, d_model), lambda u, f, ue, ur, ub: (ub[u], 0)),
        scratch_shapes=[
            pltpu.VMEM((d_model, MOE_TF), BF16),
            pltpu.VMEM((d_model, MOE_TF), BF16),
            pltpu.VMEM((MOE_TF, d_model), BF16),
        ],
    )
    return pl.pallas_call(
        _moe_kernel,
        grid_spec=grid_spec,
        out_shape=jax.ShapeDtypeStruct(xs.shape, F32),
        compiler_params=_params(("arbitrary", "arbitrary")),
        name="moe_experts",
    )(unit_expert, unit_rows, unit_block, xs, w_gate_up, w_gate_up, b_gate_up, b_gate_up, w_down, b_down)


def _combine_kernel(pos_ref, gate_ref, xmid_ref, nfin_ref, ys_ref, o_ref, buf_ref, sem):
    n = COMBINE_TM * TOP_K_EXPERTS

    def dst_row(r):
        return (r % TOP_K_EXPERTS) * COMBINE_TM + r // TOP_K_EXPERTS

    def start(r, c):
        _row_copy(ys_ref, pos_ref[0, r], buf_ref, dst_row(r), sem).start()
        return c

    def wait(r, c):
        _row_copy(ys_ref, pos_ref[0, r], buf_ref, dst_row(r), sem).wait()
        return c

    lax.fori_loop(0, n, start, 0)
    lax.fori_loop(0, n, wait, 0)
    gates = gate_ref[...]
    out = xmid_ref[...]
    for k in range(TOP_K_EXPERTS):
        out = out + gates[:, k:k + 1] * buf_ref[k * COMBINE_TM:(k + 1) * COMBINE_TM, :]
    o_ref[...] = (out * lax.rsqrt(jnp.mean(out * out, axis=-1, keepdims=True) + RMS_EPS)) * nfin_ref[...]


def _combine(pos, gates, x_mid, norm_final, ys, n_tokens):
    d_model = x_mid.shape[1]
    n_tiles = n_tokens // COMBINE_TM
    n = COMBINE_TM * TOP_K_EXPERTS
    row = lambda i: (i, 0)
    return pl.pallas_call(
        _combine_kernel,
        grid=(n_tiles,),
        in_specs=[
            pl.BlockSpec((None, 1, n), lambda i: (i, 0, 0), memory_space=pltpu.SMEM),
            pl.BlockSpec((COMBINE_TM, LANES), row),
            pl.BlockSpec((COMBINE_TM, d_model), row),
            pl.BlockSpec((1, d_model), lambda i: (0, 0)),
            pl.BlockSpec(memory_space=pl.ANY),
        ],
        out_specs=pl.BlockSpec((COMBINE_TM, d_model), row),
        out_shape=jax.ShapeDtypeStruct((n_tokens, d_model), F32),
        scratch_shapes=[pltpu.VMEM((n, d_model), F32), pltpu.SemaphoreType.DMA],
        compiler_params=_params(("arbitrary",)),
        name="combine",
    )(pos.reshape(n_tiles, 1, n), gates, x_mid, norm_final, ys)


def _routing(top_idx, n_units):
    flat_e = top_idx.reshape(-1)
    onehot = (flat_e[:, None] == jnp.arange(N_EXPERTS, dtype=jnp.int32)[None, :]).astype(jnp.int32)
    csum = jnp.cumsum(onehot, axis=0)
    rank = jnp.take_along_axis(csum, flat_e[:, None], axis=1)[:, 0] - 1
    counts = csum[-1]
    units_e = (counts + MOE_R - 1) // MOE_R
    units_end = jnp.cumsum(units_e)
    units_start = units_end - units_e
    pos = (units_start[flat_e] + rank // MOE_R) * MOE_R + rank % MOE_R
    u = jnp.arange(n_units, dtype=jnp.int32)
    total = units_end[-1]
    ue = jnp.minimum(jnp.searchsorted(units_end, u, side="right"), N_EXPERTS - 1).astype(jnp.int32)
    rows = jnp.clip(counts[ue] - (u - units_start[ue]) * MOE_R, 0, MOE_R)
    used = u < total
    last_e = ue[jnp.maximum(total - 1, 0)]
    unit_expert = jnp.where(used, ue, last_e).astype(jnp.int32)
    unit_rows = jnp.where(used, rows, 0).astype(jnp.int32)
    unit_block = jnp.minimum(u, total - 1).astype(jnp.int32)
    return pos.astype(jnp.int32), unit_expert, unit_rows, unit_block


def kernel(x_prompt, x_sample, cache_k, cache_v, cache_idx_k, page_table, norm_mix, w_in, norm_v_a,
           w_spatial, b_spatial, w_out, norm_ffn, w_router, b_router, w_gate_up, b_gate_up, w_down,
           b_down, norm_final):
    n_b, seq, d_model = x_prompt.shape
    n_db, n_new, _ = x_sample.shape
    depth = w_in.shape[0]
    past = page_table.shape[1] * PAGE_SIZE
    n_prompt, n_sample = n_b * seq, n_db * n_new
    n_tokens = n_prompt + n_sample
    assert seq % IN_TM == 0 and seq % PA_TQ == 0 and CHUNK % n_new == 0 and n_new <= SP_TPAD
    assert n_sample % CHUNK == 0 and page_table.shape[1] % SP_PAGES == 0
    assert n_tokens % COMBINE_TM == 0 and (n_tokens * TOP_K_EXPERTS) % DISPATCH_CHUNK == 0
    n_rows = pl.cdiv(n_tokens, IN_TM) * IN_TM
    k_top_prompt = min(TOPK_MAX, seq // 4)
    k_top_sample = min(TOPK_MAX, (past + n_new) // 4)
    n_assign = n_tokens * TOP_K_EXPERTS
    n_units = N_EXPERTS + n_assign // MOE_R

    pos_rows = jnp.concatenate([jnp.arange(seq, dtype=jnp.int32),
                                past + jnp.arange(IN_TM, dtype=jnp.int32) % n_new])
    tables = _rope_tables(pos_rows, ROT_DIM, HEAD_DIM) + _rope_tables(pos_rows, IDX_ROT_DIM, IDX_DIM)

    x_all = jnp.concatenate([x_prompt.reshape(n_prompt, d_model), x_sample.reshape(n_sample, d_model),
                             jnp.zeros((n_rows - n_tokens, d_model), F32)])
    outs = []
    for l in range(depth):
        causal = jnp.tril(jnp.ones((CHUNK, CHUNK), bool))
        w_prompt = jnp.where(causal[None], w_spatial[l], 0.0)
        eye = jnp.eye(CHUNK // n_new, dtype=F32)
        w_decode = jax.vmap(lambda w: jnp.kron(eye, w))(w_prompt[:, :n_new, :n_new])
        wmix = jnp.stack([w_prompt, w_decode]).astype(BF16)
        b_decode = jnp.tile(b_spatial[l][:, :n_new], (1, CHUNK // n_new))
        bmix = jnp.stack([b_spatial[l].T, b_decode.T])
        w_in_b = jnp.pad(w_in[l], ((0, 0), (0, IN_COLS_PAD - IN_COLS))).astype(BF16)

        (a_out, q, k, kb, vb, vbb, qi, ki, kib, wi, va) = _in_proj(
            x_all, n_prompt // IN_TM, seq // IN_TM, norm_mix[l][None], w_in_b, norm_v_a[l][None],
            wmix, bmix, tables)

        b_prompt = _prompt_attn(n_b, seq, k_top_prompt, qi, wi, kib, q, kb, vbb)

        sl = slice(n_prompt, n_tokens)
        pad_t = ((0, 0), (0, 0), (0, SP_TPAD - n_new), (0, 0))
        qi_s = qi[sl].reshape(n_db, n_new, N_IDX_HEADS, IDX_DIM).transpose(0, 2, 1, 3)
        qs_idx = jnp.pad(qi_s, pad_t).reshape(n_db, N_IDX_HEADS * SP_TPAD, IDX_DIM)
        wi_s = wi[sl].reshape(n_db, n_new, N_IDX_HEADS).transpose(0, 2, 1)
        ws_idx = jnp.pad(wi_s, ((0, 0), (0, 0), (0, SP_TPAD - n_new))).reshape(n_db, N_IDX_HEADS * SP_TPAD, 1)
        pad_rows = ((0, 0), (0, LANES - n_new), (0, 0))
        ki_new = jnp.pad(kib[sl, :IDX_DIM].reshape(n_db, n_new, IDX_DIM), pad_rows)
        scores = _sample_scores(page_table, qs_idx, ws_idx, ki_new, cache_idx_k[l])
        width = scores.shape[-1]
        bias = _sample_mask(scores.reshape(n_db * SP_TPAD, width), k_top_sample, past, n_new)
        q_s = q[sl].reshape(n_db, n_new, N_KV_B, KV_GROUP, HEAD_DIM).transpose(0, 2, 3, 1, 4)
        qs_att = jnp.pad(q_s, ((0, 0), (0, 0), (0, 0), (0, SP_TPAD - n_new), (0, 0))).reshape(
            n_db, N_KV_B, SA_ROWS, HEAD_DIM)
        k_new = jnp.pad(kb[sl].reshape(n_db, n_new, D_KV_B), pad_rows)
        v_new = jnp.pad(vbb[sl].reshape(n_db, n_new, D_KV_B), pad_rows)
        n_pool = cache_k.shape[1]
        o_s = _sample_attn(page_table, qs_att, bias.reshape(n_db, SP_TPAD, width), k_new, v_new,
                           cache_k[l].reshape(n_pool, PAGE_SIZE, D_KV_B),
                           cache_v[l].reshape(n_pool, PAGE_SIZE, D_KV_B))
        b_sample = o_s.reshape(n_db, N_KV_B, KV_GROUP, SP_TPAD, HEAD_DIM)[:, :, :, :n_new]
        b_sample = b_sample.transpose(0, 3, 1, 2, 4).reshape(n_sample, D_B).astype(BF16)
        b_out = jnp.concatenate([b_prompt, b_sample, jnp.zeros((n_rows - n_tokens, D_B), BF16)])

        wr_pad = jnp.pad(w_router[l], ((0, 0), (0, LANES - N_EXPERTS)))
        wr_hi, wr_lo = _split_bf16(wr_pad)
        br_pad = jnp.pad(b_router[l], (0, LANES - N_EXPERTS), constant_values=NEG_INF)[None]
        x_mid, h, top_idx, gates = _out_router(a_out, b_out, x_all, w_out[l].astype(BF16), norm_ffn[l][None],
                                               wr_hi, wr_lo, br_pad)

        pos, unit_expert, unit_rows, unit_block = _routing(top_idx[:n_tokens, :TOP_K_EXPERTS], n_units)
        xs = _dispatch(pos, h, n_units * MOE_R)
        ys = _moe_experts(unit_expert, unit_rows, unit_block, xs, w_gate_up[l], b_gate_up[l][:, None, :],
                          w_down[l], b_down[l][:, None, :])
        last = l == depth - 1
        y = _combine(pos, gates, x_mid, norm_final[None] if last else jnp.ones((1, d_model), F32), ys, n_tokens)
        outs.append((k, vb, ki, va))
        if not last:
            raise NotImplementedError("multi-layer stacks need the un-normalised residual stream")

    def stack(j, lo, hi, shape):
        return jnp.stack([o[j][lo:hi].reshape(shape) for o in outs])

    y_prompt = y[:n_prompt].reshape(n_b, seq, d_model)
    y_sample = y[n_prompt:n_tokens].reshape(n_db, n_new, d_model)
    return (
        y_prompt, y_sample,
        stack(0, 0, n_prompt, (n_b, seq, N_KV_B, HEAD_DIM)),
        stack(1, 0, n_prompt, (n_b, seq, N_KV_B, HEAD_DIM)),
        stack(2, 0, n_prompt, (n_b, seq, IDX_DIM)),
        stack(0, n_prompt, n_tokens, (n_db, n_new, N_KV_B, HEAD_DIM)),
        stack(1, n_prompt, n_tokens, (n_db, n_new, N_KV_B, HEAD_DIM)),
        stack(2, n_prompt, n_tokens, (n_db, n_new, IDX_DIM)),
        stack(3, n_prompt, n_tokens, (n_db, n_new, N_HEADS_A, HEAD_DIM)),
    )
```

```python
import functools

import jax
import jax.numpy as jnp
from jax import lax
from jax.experimental import pallas as pl
from jax.experimental.pallas import tpu as pltpu

HEAD_DIM = 128
N_HEADS_A = 8
D_A = N_HEADS_A * HEAD_DIM
CHUNK = 128
N_HEADS_B = 8
N_KV_B = 2
KV_GROUP = N_HEADS_B // N_KV_B
D_B = N_HEADS_B * HEAD_DIM
D_KV_B = N_KV_B * HEAD_DIM
ROT_DIM = HEAD_DIM // 4
ROPE_THETA = 500000.0
N_IDX_HEADS = 16
IDX_DIM = 64
IDX_ROT_DIM = IDX_DIM // 4
TOPK_MAX = 256
N_EXPERTS = 32
TOP_K_EXPERTS = 4
SWIGLU_LIMIT = 7.0
SWIGLU_ALPHA = 1.702
RMS_EPS = 1e-5
PAGE_SIZE = 128

OFF_U = 0
OFF_V = OFF_U + D_A
OFF_Q = OFF_V + D_A
OFF_K = OFF_Q + D_B
OFF_VB = OFF_K + D_KV_B
OFF_QI = OFF_VB + D_KV_B
OFF_KI = OFF_QI + N_IDX_HEADS * IDX_DIM
OFF_WI = OFF_KI + IDX_DIM
IN_COLS = OFF_WI + N_IDX_HEADS

LANES = 128
SUBLANES = 8
VMEM_LIMIT_BYTES = 56 * 1024 * 1024

IN_COLS_PAD = pl.cdiv(IN_COLS, LANES) * LANES
INT_MIN = -(2 ** 31)
NEG_INF = float("-inf")

F32 = jnp.float32
BF16 = jnp.bfloat16


def _params(semantics):
    return pltpu.CompilerParams(dimension_semantics=semantics, vmem_limit_bytes=VMEM_LIMIT_BYTES)


def _dot(a, b):
    return jnp.dot(a, b, preferred_element_type=F32)


def _dot_nt(a, b):
    return lax.dot_general(a, b, (((1,), (1,)), ((), ())), preferred_element_type=F32)


IN_TM = 256


def _rope_lanes(z, cos, sin_lo, sin_hi, half):
    return z * cos + pltpu.roll(z, LANES - half, 1) * sin_lo + pltpu.roll(z, half, 1) * sin_hi


def _gelu(z):
    return 0.5 * z * (1.0 + lax.erf(z * (2.0 ** -0.5)))


def _in_proj_kernel(x_ref, g_ref, w_ref, nva_ref, wmix_ref, bmix_ref,
                    cq_ref, saq_ref, sbq_ref, ci_ref, sai_ref, sbi_ref,
                    aout_ref, q_ref, k_ref, kb_ref, vb_ref, vbb_ref, qi_ref, ki_ref, kib_ref, wi_ref, va_ref):
    x = x_ref[...]
    ms = jnp.mean(x * x, axis=-1, keepdims=True)
    xb = ((x * lax.rsqrt(ms + RMS_EPS)) * g_ref[...]).astype(BF16)

    u = _gelu(_dot(xb, w_ref[:, OFF_U:OFF_V]))
    gv = _gelu(_dot(xb, w_ref[:, OFF_V:OFF_Q]))
    va = gv * lax.rsqrt(jnp.mean(gv * gv, axis=-1, keepdims=True) + RMS_EPS) * nva_ref[...]
    va_ref[...] = va
    vab = va.astype(BF16)
    for sub in range(IN_TM // CHUNK):
        rows = slice(sub * CHUNK, (sub + 1) * CHUNK)
        for h in range(N_HEADS_A):
            cols = slice(h * HEAD_DIM, (h + 1) * HEAD_DIM)
            mixed = _dot(wmix_ref[h], vab[rows, cols]) + bmix_ref[:, h:h + 1]
            aout_ref[rows, cols] = (u[rows, cols] * mixed).astype(BF16)

    cq, saq, sbq = cq_ref[...], saq_ref[...], sbq_ref[...]
    zq = _dot(xb, w_ref[:, OFF_Q:OFF_K])
    for h in range(N_HEADS_B):
        cols = slice(h * HEAD_DIM, (h + 1) * HEAD_DIM)
        r = _rope_lanes(zq[:, cols], cq, saq, sbq, ROT_DIM // 2)
        q_ref[:, cols] = (r * (HEAD_DIM ** -0.5)).astype(BF16)
    zk = _dot(xb, w_ref[:, OFF_K:OFF_VB])
    for h in range(N_KV_B):
        cols = slice(h * HEAD_DIM, (h + 1) * HEAD_DIM)
        r = _rope_lanes(zk[:, cols], cq, saq, sbq, ROT_DIM // 2)
        k_ref[:, cols] = r
        kb_ref[:, cols] = r.astype(BF16)
    zvb = _dot(xb, w_ref[:, OFF_VB:OFF_QI])
    vb_ref[...] = zvb
    vbb_ref[...] = zvb.astype(BF16)

    ci, sai, sbi = ci_ref[...], sai_ref[...], sbi_ref[...]
    zqi = _dot(xb, w_ref[:, OFF_QI:OFF_KI])
    for p in range(N_IDX_HEADS * IDX_DIM // LANES):
        cols = slice(p * LANES, (p + 1) * LANES)
        qi_ref[:, cols] = _rope_lanes(zqi[:, cols], ci, sai, sbi, IDX_ROT_DIM // 2).astype(BF16)
    zl = _dot(xb, w_ref[:, OFF_KI:IN_COLS_PAD])
    r = _rope_lanes(zl, ci, sai, sbi, IDX_ROT_DIM // 2)
    ki_ref[...] = r[:, :IDX_DIM]
    lane = lax.broadcasted_iota(jnp.int32, r.shape, 1)
    kib_ref[...] = jnp.where(lane < IDX_DIM, r, pltpu.roll(r, IDX_DIM, 1)).astype(BF16)
    wi_ref[...] = zl[:, IDX_DIM:IDX_DIM + N_IDX_HEADS] * (N_IDX_HEADS ** -0.5 * IDX_DIM ** -0.5)


def _rope_tables(pos, rot_dim, head_dim):
    half = rot_dim // 2
    inv = ROPE_THETA ** (-jnp.arange(half, dtype=F32) / half)
    ang = pos.astype(F32)[:, None] * inv[None, :]
    cos, sin = jnp.cos(ang), jnp.sin(ang)
    lane = jnp.arange(LANES) % head_dim
    cos_l = jnp.take(cos, lane % half, axis=1)
    sin_l = jnp.take(sin, lane % half, axis=1)
    c = jnp.where(lane[None] < rot_dim, cos_l, 1.0)
    s_lo = jnp.where(lane[None] < half, -sin_l, 0.0)
    s_hi = jnp.where((lane[None] >= half) & (lane[None] < rot_dim), sin_l, 0.0)
    return c.astype(F32), s_lo.astype(F32), s_hi.astype(F32)


def _in_proj(x_all, n_prompt_tiles, tiles_per_seq, norm_mix, w_in_b, norm_v_a, wmix, bmix, tables):
    n_rows, d_model = x_all.shape
    n_tiles = n_rows // IN_TM
    row = lambda i: (i, 0)
    const = lambda i: (0, 0)
    tab = lambda i: (jnp.where(i < n_prompt_tiles, i % tiles_per_seq, tiles_per_seq), 0)
    grp = lambda i: (jnp.where(i < n_prompt_tiles, 0, 1), 0, 0, 0)
    grp2 = lambda i: (jnp.where(i < n_prompt_tiles, 0, 1), 0, 0)
    tab_spec = pl.BlockSpec((IN_TM, LANES), tab)
    outs = [
        ((n_rows, D_A), BF16),
        ((n_rows, D_B), BF16),
        ((n_rows, D_KV_B), F32),
        ((n_rows, D_KV_B), BF16),
        ((n_rows, D_KV_B), F32),
        ((n_rows, D_KV_B), BF16),
        ((n_rows, N_IDX_HEADS * IDX_DIM), BF16),
        ((n_rows, IDX_DIM), F32),
        ((n_rows, LANES), BF16),
        ((n_rows, N_IDX_HEADS), F32),
        ((n_rows, D_A), F32),
    ]
    return pl.pallas_call(
        _in_proj_kernel,
        grid=(n_tiles,),
        in_specs=[
            pl.BlockSpec((IN_TM, d_model), row),
            pl.BlockSpec((1, d_model), const),
            pl.BlockSpec((d_model, IN_COLS_PAD), const, pipeline_mode=pl.Buffered(1)),
            pl.BlockSpec((1, D_A), const),
            pl.BlockSpec((None, N_HEADS_A, CHUNK, CHUNK), grp),
            pl.BlockSpec((None, CHUNK, N_HEADS_A), grp2),
            tab_spec, tab_spec, tab_spec, tab_spec, tab_spec, tab_spec,
        ],
        out_specs=[pl.BlockSpec((IN_TM, s[1]), row) for s, _ in outs],
        out_shape=[jax.ShapeDtypeStruct(s, d) for s, d in outs],
        compiler_params=_params(("parallel",)),
        name="in_proj",
    )(x_all, norm_mix, w_in_b, norm_v_a, wmix, bmix, *tables)


def _order_key(score):
    bits = pltpu.bitcast(score, jnp.int32)
    return bits ^ ((bits >> 31) & jnp.int32(0x7FFFFFFF))


def _kth_largest_key(key, k_top):
    def body(it, t):
        cand = t + lax.shift_left(jnp.int32(1), jnp.int32(31) - it)
        cnt = jnp.sum((key >= cand).astype(F32), axis=1, keepdims=True)
        return jnp.where(cnt >= k_top, cand, t)

    t0 = jnp.full((key.shape[0], 1), INT_MIN, jnp.int32)
    return lax.fori_loop(0, 32, body, t0)


def _topk_select(key, k_top, tri):
    thr = _kth_largest_key(key, k_top)
    above = key > thr
    need = k_top - jnp.sum(above.astype(F32), axis=1, keepdims=True)
    tie = key == thr
    pieces = []
    run = jnp.zeros_like(need)
    for j in range(key.shape[1] // LANES):
        cols = slice(j * LANES, (j + 1) * LANES)
        tie_j = tie[:, cols]
        prefix = _dot(tie_j.astype(BF16), tri) + run
        run = prefix[:, LANES - 1:LANES]
        pieces.append(jnp.where(above[:, cols] | (tie_j & (prefix <= need)), 1.0, 0.0))
    return jnp.concatenate(pieces, axis=1)


def _tri_matrix():
    r = lax.broadcasted_iota(jnp.int32, (LANES, LANES), 0)
    c = lax.broadcasted_iota(jnp.int32, (LANES, LANES), 1)
    return (r <= c).astype(BF16)


PA_TQ = 128


PA_EXTENTS = 4


def _prompt_attn_kernel(k_top, qi_ref, wi_ref, kib_ref, q_ref, kb_ref, vbb_ref, o_ref):
    i = pl.program_id(1)
    step = kb_ref.shape[0] // PA_EXTENTS
    blocks_per_step = step // PA_TQ
    for c in range(PA_EXTENTS):
        @pl.when(i // blocks_per_step == c)
        def _():
            _prompt_attn_block(k_top, (c + 1) * step, i, qi_ref, wi_ref, kib_ref, q_ref, kb_ref, vbb_ref, o_ref)


def _prompt_attn_block(k_top, seq, i, qi_ref, wi_ref, kib_ref, q_ref, kb_ref, vbb_ref, o_ref):
    qi = qi_ref[...]
    wi = wi_ref[...]
    ki2 = kib_ref[:seq, :]
    lane = lax.broadcasted_iota(jnp.int32, (PA_TQ, LANES), 1)
    score = jnp.zeros((PA_TQ, seq), F32)
    for h in range(N_IDX_HEADS):
        pair = qi[:, (h // 2) * LANES:(h // 2 + 1) * LANES]
        keep = (lane < IDX_DIM) if h % 2 == 0 else (lane >= IDX_DIM)
        d = _dot_nt(jnp.where(keep, pair, jnp.zeros_like(pair)), ki2)
        score = score + wi[:, h:h + 1] * jnp.maximum(d, 0.0)
    q_pos = i * PA_TQ + lax.broadcasted_iota(jnp.int32, (PA_TQ, seq), 0)
    s_pos = lax.broadcasted_iota(jnp.int32, (PA_TQ, seq), 1)
    causal = s_pos <= q_pos
    key = jnp.where(causal, _order_key(score), INT_MIN)
    sel = _topk_select(key, k_top, _tri_matrix())
    bias = jnp.where(causal & (sel > 0.0), 0.0, NEG_INF)

    q = q_ref[...]
    for g in range(N_KV_B):
        q4 = jnp.concatenate(
            [q[:, (g * KV_GROUP + r) * HEAD_DIM:(g * KV_GROUP + r + 1) * HEAD_DIM] for r in range(KV_GROUP)], axis=0)
        kg = kb_ref[:seq, g * HEAD_DIM:(g + 1) * HEAD_DIM]
        vg = vbb_ref[:seq, g * HEAD_DIM:(g + 1) * HEAD_DIM]
        s = _dot_nt(q4, kg).reshape(KV_GROUP, PA_TQ, seq) + bias[None]
        m = jnp.max(s, axis=-1, keepdims=True)
        p = jnp.exp(s - m)
        l = jnp.sum(p, axis=-1, keepdims=True)
        o = _dot(p.reshape(KV_GROUP * PA_TQ, seq).astype(BF16), vg).reshape(KV_GROUP, PA_TQ, HEAD_DIM) / l
        for r in range(KV_GROUP):
            o_ref[:, (g * KV_GROUP + r) * HEAD_DIM:(g * KV_GROUP + r + 1) * HEAD_DIM] = o[r].astype(BF16)


def _prompt_attn(n_batch, seq, k_top, qi, wi, kib, q, kb, vbb):
    nq = seq // PA_TQ
    qrow = lambda b, i: (b * nq + i, 0)
    krow = lambda b, i: (b, 0)
    return pl.pallas_call(
        functools.partial(_prompt_attn_kernel, k_top),
        grid=(n_batch, nq),
        in_specs=[
            pl.BlockSpec((PA_TQ, N_IDX_HEADS * IDX_DIM), qrow),
            pl.BlockSpec((PA_TQ, N_IDX_HEADS), qrow),
            pl.BlockSpec((seq, LANES), krow),
            pl.BlockSpec((PA_TQ, D_B), qrow),
            pl.BlockSpec((seq, D_KV_B), krow),
            pl.BlockSpec((seq, D_KV_B), krow),
        ],
        out_specs=pl.BlockSpec((PA_TQ, D_B), qrow),
        out_shape=jax.ShapeDtypeStruct((n_batch * seq, D_B), BF16),
        compiler_params=_params(("parallel", "parallel")),
        name="prompt_attn",
    )(qi, wi, kib, q, kb, vbb)


SP_PAGES = 16
SP_KEYS = SP_PAGES * PAGE_SIZE
SP_TPAD = SUBLANES


def _page_specs(tail, n_steps, layer):
    def make(p):
        def index(b, j, pt):
            return (layer, pt[b, jnp.minimum(j, n_steps - 1) * SP_PAGES + p]) + (0,) * (1 + len(tail))
        return pl.BlockSpec((None, None, PAGE_SIZE) + tail, index)
    return [make(p) for p in range(SP_PAGES)]


def _sample_scores_kernel(n_steps, pt_ref, qs_ref, ws_ref, kin_ref, *refs):
    page_refs, o_ref = refs[:SP_PAGES], refs[SP_PAGES]
    j = pl.program_id(1)
    qs = qs_ref[...]
    ws = ws_ref[...]

    def head_sum(d):
        w = ws * jnp.maximum(d, 0.0)
        return jnp.sum(w.reshape(N_IDX_HEADS, SP_TPAD, d.shape[1]), axis=0)

    @pl.when(j < n_steps)
    def _():
        keys = jnp.concatenate([r[...] for r in page_refs], axis=0).astype(BF16)
        o_ref[...] = head_sum(_dot_nt(qs, keys))

    @pl.when(j == n_steps)
    def _():
        d = _dot_nt(qs, kin_ref[...])
        o_ref[...] = jnp.concatenate(
            [head_sum(d), jnp.zeros((SP_TPAD, SP_KEYS - d.shape[1]), F32)], axis=1)


def _sample_scores(page_table, qs, ws, ki_new, cache_idx_k, layer):
    n_db, n_pages = page_table.shape
    n_steps = n_pages // SP_PAGES
    rows = N_IDX_HEADS * SP_TPAD
    per_b = lambda b, j, pt: (b, 0, 0)
    grid_spec = pltpu.PrefetchScalarGridSpec(
        num_scalar_prefetch=1,
        grid=(n_db, n_steps + 1),
        in_specs=[
            pl.BlockSpec((None, rows, IDX_DIM), per_b),
            pl.BlockSpec((None, rows, 1), per_b),
            pl.BlockSpec((None, LANES, IDX_DIM), per_b),
        ] + _page_specs((IDX_DIM,), n_steps, layer),
        out_specs=pl.BlockSpec((None, SP_TPAD, SP_KEYS), lambda b, j, pt: (b, 0, j)),
    )
    return pl.pallas_call(
        functools.partial(_sample_scores_kernel, n_steps),
        grid_spec=grid_spec,
        out_shape=jax.ShapeDtypeStruct((n_db, SP_TPAD, (n_steps + 1) * SP_KEYS), F32),
        compiler_params=_params(("parallel", "arbitrary")),
        name="sample_scores",
    )(page_table, qs, ws, ki_new, *([cache_idx_k] * SP_PAGES))


SM_ROWS = 64


def _sample_mask_kernel(k_top, past, n_new, s_ref, o_ref):
    score = s_ref[...]
    t = lax.broadcasted_iota(jnp.int32, score.shape, 0) % SP_TPAD
    s_pos = lax.broadcasted_iota(jnp.int32, score.shape, 1)
    admissible = s_pos <= past + t
    key = jnp.where(admissible, _order_key(score), INT_MIN)
    sel = _topk_select(key, k_top, _tri_matrix())
    o_ref[...] = jnp.where((admissible & (sel > 0.0)) | (t >= n_new), 0.0, NEG_INF)


def _sample_mask(scores2d, k_top, past, n_new):
    n_rows, width = scores2d.shape
    return pl.pallas_call(
        functools.partial(_sample_mask_kernel, k_top, past, n_new),
        grid=(n_rows // SM_ROWS,),
        in_specs=[pl.BlockSpec((SM_ROWS, width), lambda i: (i, 0))],
        out_specs=pl.BlockSpec((SM_ROWS, width), lambda i: (i, 0)),
        out_shape=jax.ShapeDtypeStruct((n_rows, width), F32),
        compiler_params=_params(("parallel",)),
        name="sample_mask",
    )(scores2d)


SA_ROWS = KV_GROUP * SP_TPAD


def _sample_attn_kernel(n_steps, pt_ref, q_ref, bias_ref, kn_ref, vn_ref, *refs):
    k_refs, v_refs = refs[:SP_PAGES], refs[SP_PAGES:2 * SP_PAGES]
    o_ref, m_ref, l_ref, acc_ref = refs[2 * SP_PAGES:]
    j = pl.program_id(1)

    @pl.when(j == 0)
    def _():
        m_ref[...] = jnp.full(m_ref.shape, NEG_INF, F32)
        l_ref[...] = jnp.zeros(l_ref.shape, F32)
        acc_ref[...] = jnp.zeros(acc_ref.shape, F32)

    def update(g, kg, vg, bias):
        s = _dot_nt(q_ref[g], kg) + jnp.concatenate([bias] * KV_GROUP, axis=0)
        m_old = m_ref[g]
        m_new = jnp.maximum(m_old, jnp.max(s, axis=-1, keepdims=True))
        m_safe = jnp.where(m_new == NEG_INF, 0.0, m_new)
        alpha = jnp.exp(m_old - m_safe)
        p = jnp.exp(s - m_safe)
        l_ref[g] = alpha * l_ref[g] + jnp.sum(p, axis=-1, keepdims=True)
        acc_ref[g] = alpha * acc_ref[g] + _dot(p.astype(BF16), vg)
        m_ref[g] = m_new

    @pl.when(j < n_steps)
    def _():
        bias = bias_ref[...]
        for g in range(N_KV_B):
            cols = slice(g * HEAD_DIM, (g + 1) * HEAD_DIM)
            kg = jnp.concatenate([r[:, g, :] for r in k_refs], axis=0).astype(BF16)
            vg = jnp.concatenate([r[:, g, :] for r in v_refs], axis=0).astype(BF16)
            update(g, kg, vg, bias)

    @pl.when(j == n_steps)
    def _():
        bias = bias_ref[:, :LANES]
        for g in range(N_KV_B):
            cols = slice(g * HEAD_DIM, (g + 1) * HEAD_DIM)
            update(g, kn_ref[:, cols], vn_ref[:, cols], bias)
            o_ref[g] = acc_ref[g] / l_ref[g]


def _sample_attn(page_table, qs, bias, k_new, v_new, cache_k, cache_v, layer):
    n_db, n_pages = page_table.shape
    n_steps = n_pages // SP_PAGES
    per_b3 = lambda b, j, pt: (b, 0, 0)
    per_b4 = lambda b, j, pt: (b, 0, 0, 0)
    grid_spec = pltpu.PrefetchScalarGridSpec(
        num_scalar_prefetch=1,
        grid=(n_db, n_steps + 1),
        in_specs=[
            pl.BlockSpec((None, N_KV_B, SA_ROWS, HEAD_DIM), per_b4),
            pl.BlockSpec((None, SP_TPAD, SP_KEYS), lambda b, j, pt: (b, 0, j)),
            pl.BlockSpec((None, LANES, D_KV_B), per_b3),
            pl.BlockSpec((None, LANES, D_KV_B), per_b3),
        ] + 2 * _page_specs((N_KV_B, HEAD_DIM), n_steps, layer),
        out_specs=pl.BlockSpec((None, N_KV_B, SA_ROWS, HEAD_DIM), per_b4),
        scratch_shapes=[
            pltpu.VMEM((N_KV_B, SA_ROWS, 1), F32),
            pltpu.VMEM((N_KV_B, SA_ROWS, 1), F32),
            pltpu.VMEM((N_KV_B, SA_ROWS, HEAD_DIM), F32),
        ],
    )
    return pl.pallas_call(
        functools.partial(_sample_attn_kernel, n_steps),
        grid_spec=grid_spec,
        out_shape=jax.ShapeDtypeStruct((n_db, N_KV_B, SA_ROWS, HEAD_DIM), F32),
        compiler_params=_params(("parallel", "arbitrary")),
        name="sample_attn",
    )(page_table, qs, bias, k_new, v_new, *([cache_k] * SP_PAGES), *([cache_v] * SP_PAGES))


OR_TM = 256


def _split_bf16(x):
    hi = x.astype(BF16)
    return hi, (x - hi.astype(F32)).astype(BF16)


def _out_router_kernel(a_ref, b_ref, x_ref, wo_ref, nf_ref, wrh_ref, wrl_ref, br_ref,
                       xmid_ref, h_ref, idx_ref, gate_ref):
    x_mid = x_ref[...] + _dot(a_ref[...], wo_ref[:D_A, :]) + _dot(b_ref[...], wo_ref[D_A:, :])
    xmid_ref[...] = x_mid
    h = (x_mid * lax.rsqrt(jnp.mean(x_mid * x_mid, axis=-1, keepdims=True) + RMS_EPS)) * nf_ref[...]
    h_ref[...] = h
    h_hi, h_lo = _split_bf16(h)
    wr_hi = wrh_ref[...]
    logits = _dot(h_hi, wr_hi) + _dot(h_lo, wr_hi) + _dot(h_hi, wrl_ref[...]) + br_ref[...]
    lane = lax.broadcasted_iota(jnp.int32, logits.shape, 1)
    vals, idxs = [], []
    for _ in range(TOP_K_EXPERTS):
        m = jnp.max(logits, axis=-1, keepdims=True)
        i = jnp.min(jnp.where(logits == m, lane, LANES), axis=-1, keepdims=True)
        vals.append(m)
        idxs.append(i)
        logits = jnp.where(lane == i, NEG_INF, logits)
    exps = [jnp.exp(v - vals[0]) for v in vals]
    total = exps[0]
    for e in exps[1:]:
        total = total + e
    idx_out = jnp.zeros(logits.shape, jnp.int32)
    gate_out = jnp.zeros(logits.shape, F32)
    for k in range(TOP_K_EXPERTS):
        idx_out = jnp.where(lane == k, idxs[k], idx_out)
        gate_out = jnp.where(lane == k, exps[k] / total, gate_out)
    idx_ref[...] = idx_out
    gate_ref[...] = gate_out


def _out_router(a_out, b_out, x_all, w_out_b, norm_ffn, wr_hi, wr_lo, br_pad):
    n_rows, d_model = x_all.shape
    row = lambda i: (i, 0)
    const = lambda i: (0, 0)
    return pl.pallas_call(
        _out_router_kernel,
        grid=(n_rows // OR_TM,),
        in_specs=[
            pl.BlockSpec((OR_TM, D_A), row),
            pl.BlockSpec((OR_TM, D_B), row),
            pl.BlockSpec((OR_TM, d_model), row),
            pl.BlockSpec((D_A + D_B, d_model), const),
            pl.BlockSpec((1, d_model), const),
            pl.BlockSpec((d_model, LANES), const),
            pl.BlockSpec((d_model, LANES), const),
            pl.BlockSpec((1, LANES), const),
        ],
        out_specs=[
            pl.BlockSpec((OR_TM, d_model), row),
            pl.BlockSpec((OR_TM, d_model), row),
            pl.BlockSpec((OR_TM, LANES), row),
            pl.BlockSpec((OR_TM, LANES), row),
        ],
        out_shape=[
            jax.ShapeDtypeStruct((n_rows, d_model), F32),
            jax.ShapeDtypeStruct((n_rows, d_model), F32),
            jax.ShapeDtypeStruct((n_rows, LANES), jnp.int32),
            jax.ShapeDtypeStruct((n_rows, LANES), F32),
        ],
        compiler_params=_params(("parallel",)),
        name="out_router",
    )(a_out, b_out, x_all, w_out_b, norm_ffn, wr_hi, wr_lo, br_pad)


MOE_TM = 256
MOE_UNIT_TILES = 5
MOE_R = MOE_TM * MOE_UNIT_TILES
MOE_TF = 256
COMBINE_TM = 128


def _row_copy(src_ref, src_row, dst_ref, dst_row, sem):
    return pltpu.make_async_copy(src_ref.at[pl.ds(src_row, 1), :], dst_ref.at[pl.ds(dst_row, 1), :], sem)


def _tile_rows(rows):
    return pl.cdiv(rows, MOE_TM) * MOE_TM


def _moe_kernel(ue_ref, ur_ref, ub_ref, tok_ref, tokn_ref, h_ref, wg_ref, wu_ref, bg_ref, bu_ref, wd_ref, bd_ref,
                ys_ref, land_ref, xb_ref, y_ref, wgb_ref, wub_ref, wdb_ref, gsem, ysem):
    u, f = pl.program_id(0), pl.program_id(1)
    n_units, n_f = pl.num_programs(0), pl.num_programs(1)
    rows = ur_ref[u]
    n_tiles = pl.cdiv(rows, MOE_TM)

    def gather(tokens_ref, n_rows, wait):
        def body(r, c):
            cp = _row_copy(h_ref, tokens_ref[0, r], land_ref, r, gsem)
            cp.wait() if wait else cp.start()
            return c
        lax.fori_loop(0, n_rows, body, 0)

    def writeback(unit, n, wait):
        def body(m, c):
            r0 = pl.multiple_of(m * MOE_TM, MOE_TM)
            cp = pltpu.make_async_copy(y_ref.at[pl.ds(r0, MOE_TM), :],
                                       ys_ref.at[pl.ds(ub_ref[unit] * MOE_R + r0, MOE_TM), :], ysem)
            cp.wait() if wait else cp.start()
            return c
        lax.fori_loop(0, n, body, 0)

    @pl.when((f == 0) & (rows > 0))
    def _():
        @pl.when(u == 0)
        def _():
            gather(tok_ref, _tile_rows(rows), wait=False)

        gather(tok_ref, _tile_rows(rows), wait=True)

        def convert(m, c):
            r0 = pl.multiple_of(m * MOE_TM, MOE_TM)
            xb_ref[pl.ds(r0, MOE_TM), :] = land_ref[pl.ds(r0, MOE_TM), :].astype(BF16)
            return c
        lax.fori_loop(0, n_tiles, convert, 0)

        nxt = jnp.minimum(u + 1, n_units - 1)
        rows_next = jnp.where(u + 1 < n_units, ur_ref[nxt], 0)
        gather(tokn_ref, _tile_rows(rows_next), wait=False)

    @pl.when((f == 0) & (u > 0))
    def _():
        writeback(u - 1, pl.cdiv(ur_ref[jnp.maximum(u - 1, 0)], MOE_TM), wait=True)

    @pl.when(rows > 0)
    def _():
        wgb_ref[...] = wg_ref[...].astype(BF16)
        wub_ref[...] = wu_ref[...].astype(BF16)
        wdb_ref[...] = wd_ref[...].astype(BF16)

        def tile(m, c):
            r0 = pl.multiple_of(m * MOE_TM, MOE_TM)
            xm = xb_ref[pl.ds(r0, MOE_TM), :]
            gate = jnp.minimum(_dot(xm, wgb_ref[...]) + bg_ref[...], SWIGLU_LIMIT)
            up = jnp.clip(_dot(xm, wub_ref[...]) + bu_ref[...], -SWIGLU_LIMIT, SWIGLU_LIMIT)
            act = (up + 1.0) * (gate * jax.nn.sigmoid(SWIGLU_ALPHA * gate))
            contrib = _dot(act.astype(BF16), wdb_ref[...])

            @pl.when(f == 0)
            def _():
                y_ref[pl.ds(r0, MOE_TM), :] = contrib + bd_ref[...]

            @pl.when(f > 0)
            def _():
                y_ref[pl.ds(r0, MOE_TM), :] += contrib
            return c

        lax.fori_loop(0, n_tiles, tile, 0)

        @pl.when(f == n_f - 1)
        def _():
            writeback(u, n_tiles, wait=False)

    @pl.when((f == n_f - 1) & (u == n_units - 1))
    def _():
        writeback(u, n_tiles, wait=True)


def _moe_experts(unit_expert, unit_rows, unit_block, slot_token, h, w_gate_up, b_gate_up, w_down, b_down):
    n_units = unit_expert.shape[0]
    d_model = h.shape[1]
    d_ff = w_down.shape[1]
    n_f = d_ff // MOE_TF

    def f_eff(u, f, ur):
        return jnp.where(ur[u] > 0, f, n_f - 1)

    smem_tokens = lambda index: pl.BlockSpec((None, 1, MOE_R), index, memory_space=pltpu.SMEM)
    grid_spec = pltpu.PrefetchScalarGridSpec(
        num_scalar_prefetch=3,
        grid=(n_units, n_f),
        in_specs=[
            smem_tokens(lambda u, f, ue, ur, ub: (ub[u], 0, 0)),
            smem_tokens(lambda u, f, ue, ur, ub: (ub[jnp.minimum(u + 1, n_units - 1)], 0, 0)),
            pl.BlockSpec(memory_space=pl.ANY),
            pl.BlockSpec((None, d_model, MOE_TF), lambda u, f, ue, ur, ub: (ue[u], 0, f_eff(u, f, ur))),
            pl.BlockSpec((None, d_model, MOE_TF), lambda u, f, ue, ur, ub: (ue[u], 0, n_f + f_eff(u, f, ur))),
            pl.BlockSpec((None, 1, MOE_TF), lambda u, f, ue, ur, ub: (ue[u], 0, f_eff(u, f, ur))),
            pl.BlockSpec((None, 1, MOE_TF), lambda u, f, ue, ur, ub: (ue[u], 0, n_f + f_eff(u, f, ur))),
            pl.BlockSpec((None, MOE_TF, d_model), lambda u, f, ue, ur, ub: (ue[u], f_eff(u, f, ur), 0)),
            pl.BlockSpec((None, 1, d_model), lambda u, f, ue, ur, ub: (ue[u], 0, 0)),
        ],
        out_specs=pl.BlockSpec(memory_space=pl.ANY),
        scratch_shapes=[
            pltpu.VMEM((MOE_R, d_model), F32),
            pltpu.VMEM((MOE_R, d_model), BF16),
            pltpu.VMEM((MOE_R, d_model), F32),
            pltpu.VMEM((d_model, MOE_TF), BF16),
            pltpu.VMEM((d_model, MOE_TF), BF16),
            pltpu.VMEM((MOE_TF, d_model), BF16),
            pltpu.SemaphoreType.DMA,
            pltpu.SemaphoreType.DMA,
        ],
    )
    tokens = slot_token.reshape(n_units, 1, MOE_R)
    return pl.pallas_call(
        _moe_kernel,
        grid_spec=grid_spec,
        out_shape=jax.ShapeDtypeStruct((n_units * MOE_R, d_model), F32),
        compiler_params=_params(("arbitrary", "arbitrary")),
        name="moe_experts",
    )(unit_expert, unit_rows, unit_block, tokens, tokens, h, w_gate_up, w_gate_up, b_gate_up, b_gate_up,
      w_down, b_down)


def _combine_kernel(pos_ref, gate_ref, xmid_ref, nfin_ref, ys_ref, o_ref, buf_ref, sem):
    n = COMBINE_TM * TOP_K_EXPERTS

    def dst_row(r):
        return (r % TOP_K_EXPERTS) * COMBINE_TM + r // TOP_K_EXPERTS

    def start(r, c):
        _row_copy(ys_ref, pos_ref[0, r], buf_ref, dst_row(r), sem).start()
        return c

    def wait(r, c):
        _row_copy(ys_ref, pos_ref[0, r], buf_ref, dst_row(r), sem).wait()
        return c

    lax.fori_loop(0, n, start, 0)
    lax.fori_loop(0, n, wait, 0)
    gates = gate_ref[...]
    out = xmid_ref[...]
    for k in range(TOP_K_EXPERTS):
        out = out + gates[:, k:k + 1] * buf_ref[k * COMBINE_TM:(k + 1) * COMBINE_TM, :]
    o_ref[...] = (out * lax.rsqrt(jnp.mean(out * out, axis=-1, keepdims=True) + RMS_EPS)) * nfin_ref[...]


def _combine(pos, gates, x_mid, norm_final, ys, n_tokens):
    d_model = x_mid.shape[1]
    n_tiles = n_tokens // COMBINE_TM
    n = COMBINE_TM * TOP_K_EXPERTS
    row = lambda i: (i, 0)
    return pl.pallas_call(
        _combine_kernel,
        grid=(n_tiles,),
        in_specs=[
            pl.BlockSpec((None, 1, n), lambda i: (i, 0, 0), memory_space=pltpu.SMEM),
            pl.BlockSpec((COMBINE_TM, LANES), row),
            pl.BlockSpec((COMBINE_TM, d_model), row),
            pl.BlockSpec((1, d_model), lambda i: (0, 0)),
            pl.BlockSpec(memory_space=pl.ANY),
        ],
        out_specs=pl.BlockSpec((COMBINE_TM, d_model), row),
        out_shape=jax.ShapeDtypeStruct((n_tokens, d_model), F32),
        scratch_shapes=[pltpu.VMEM((n, d_model), F32), pltpu.SemaphoreType.DMA],
        compiler_params=_params(("arbitrary",)),
        name="combine",
    )(pos.reshape(n_tiles, 1, n), gates, x_mid, norm_final, ys)


def _routing(top_idx, n_units, zero_row):
    flat_e = top_idx.reshape(-1)
    onehot = (flat_e[:, None] == jnp.arange(N_EXPERTS, dtype=jnp.int32)[None, :]).astype(jnp.int32)
    csum = jnp.cumsum(onehot, axis=0)
    rank = jnp.take_along_axis(csum, flat_e[:, None], axis=1)[:, 0] - 1
    counts = csum[-1]
    units_e = (counts + MOE_R - 1) // MOE_R
    units_end = jnp.cumsum(units_e)
    units_start = units_end - units_e
    pos = (units_start[flat_e] + rank // MOE_R) * MOE_R + rank % MOE_R
    u = jnp.arange(n_units, dtype=jnp.int32)
    total = units_end[-1]
    ue = jnp.minimum(jnp.sum((units_end[None, :] <= u[:, None]).astype(jnp.int32), axis=1), N_EXPERTS - 1)
    rows = jnp.clip(counts[ue] - (u - units_start[ue]) * MOE_R, 0, MOE_R)
    used = u < total
    last_e = ue[jnp.maximum(total - 1, 0)]
    unit_expert = jnp.where(used, ue, last_e).astype(jnp.int32)
    unit_rows = jnp.where(used, rows, 0).astype(jnp.int32)
    unit_block = jnp.minimum(u, total - 1).astype(jnp.int32)
    pos = pos.astype(jnp.int32)
    token = jnp.arange(flat_e.shape[0], dtype=jnp.int32) // TOP_K_EXPERTS
    slot_token = jnp.full((n_units * MOE_R,), zero_row, jnp.int32).at[pos].set(token, unique_indices=True)
    return pos, slot_token, unit_expert, unit_rows, unit_block


def kernel(x_prompt, x_sample, cache_k, cache_v, cache_idx_k, page_table, norm_mix, w_in, norm_v_a,
           w_spatial, b_spatial, w_out, norm_ffn, w_router, b_router, w_gate_up, b_gate_up, w_down,
           b_down, norm_final):
    n_b, seq, d_model = x_prompt.shape
    n_db, n_new, _ = x_sample.shape
    depth = w_in.shape[0]
    past = page_table.shape[1] * PAGE_SIZE
    n_prompt, n_sample = n_b * seq, n_db * n_new
    n_tokens = n_prompt + n_sample
    assert seq % IN_TM == 0 and seq % PA_TQ == 0 and CHUNK % n_new == 0 and n_new <= SP_TPAD
    assert n_sample % CHUNK == 0 and page_table.shape[1] % SP_PAGES == 0
    assert n_tokens % COMBINE_TM == 0
    n_rows = (n_tokens // IN_TM + 1) * IN_TM
    k_top_prompt = min(TOPK_MAX, seq // 4)
    k_top_sample = min(TOPK_MAX, (past + n_new) // 4)
    n_assign = n_tokens * TOP_K_EXPERTS
    n_units = N_EXPERTS + n_assign // MOE_R

    pos_rows = jnp.concatenate([jnp.arange(seq, dtype=jnp.int32),
                                past + jnp.arange(IN_TM, dtype=jnp.int32) % n_new])
    tables = _rope_tables(pos_rows, ROT_DIM, HEAD_DIM) + _rope_tables(pos_rows, IDX_ROT_DIM, IDX_DIM)

    x_all = jnp.concatenate([x_prompt.reshape(n_prompt, d_model), x_sample.reshape(n_sample, d_model),
                             jnp.zeros((n_rows - n_tokens, d_model), F32)])
    outs = []
    for l in range(depth):
        causal = jnp.tril(jnp.ones((CHUNK, CHUNK), bool))
        w_prompt = jnp.where(causal[None], w_spatial[l], 0.0)
        eye = jnp.eye(CHUNK // n_new, dtype=F32)
        w_decode = jax.vmap(lambda w: jnp.kron(eye, w))(w_prompt[:, :n_new, :n_new])
        wmix = jnp.stack([w_prompt, w_decode]).astype(BF16)
        b_decode = jnp.tile(b_spatial[l][:, :n_new], (1, CHUNK // n_new))
        bmix = jnp.stack([b_spatial[l].T, b_decode.T])
        w_in_b = jnp.pad(w_in[l], ((0, 0), (0, IN_COLS_PAD - IN_COLS))).astype(BF16)

        (a_out, q, k, kb, vb, vbb, qi, ki, kib, wi, va) = _in_proj(
            x_all, n_prompt // IN_TM, seq // IN_TM, norm_mix[l][None], w_in_b, norm_v_a[l][None],
            wmix, bmix, tables)

        b_prompt = _prompt_attn(n_b, seq, k_top_prompt, qi, wi, kib, q, kb, vbb)

        sl = slice(n_prompt, n_tokens)
        pad_t = ((0, 0), (0, 0), (0, SP_TPAD - n_new), (0, 0))
        qi_s = qi[sl].reshape(n_db, n_new, N_IDX_HEADS, IDX_DIM).transpose(0, 2, 1, 3)
        qs_idx = jnp.pad(qi_s, pad_t).reshape(n_db, N_IDX_HEADS * SP_TPAD, IDX_DIM)
        wi_s = wi[sl].reshape(n_db, n_new, N_IDX_HEADS).transpose(0, 2, 1)
        ws_idx = jnp.pad(wi_s, ((0, 0), (0, 0), (0, SP_TPAD - n_new))).reshape(n_db, N_IDX_HEADS * SP_TPAD, 1)
        pad_rows = ((0, 0), (0, LANES - n_new), (0, 0))
        ki_new = jnp.pad(kib[sl, :IDX_DIM].reshape(n_db, n_new, IDX_DIM), pad_rows)
        scores = _sample_scores(page_table, qs_idx, ws_idx, ki_new, cache_idx_k, l)
        width = scores.shape[-1]
        bias = _sample_mask(scores.reshape(n_db * SP_TPAD, width), k_top_sample, past, n_new)
        q_s = q[sl].reshape(n_db, n_new, N_KV_B, KV_GROUP, HEAD_DIM).transpose(0, 2, 3, 1, 4)
        qs_att = jnp.pad(q_s, ((0, 0), (0, 0), (0, 0), (0, SP_TPAD - n_new), (0, 0))).reshape(
            n_db, N_KV_B, SA_ROWS, HEAD_DIM)
        k_new = jnp.pad(kb[sl].reshape(n_db, n_new, D_KV_B), pad_rows)
        v_new = jnp.pad(vbb[sl].reshape(n_db, n_new, D_KV_B), pad_rows)
        o_s = _sample_attn(page_table, qs_att, bias.reshape(n_db, SP_TPAD, width), k_new, v_new,
                           cache_k, cache_v, l)
        b_sample = o_s.reshape(n_db, N_KV_B, KV_GROUP, SP_TPAD, HEAD_DIM)[:, :, :, :n_new]
        b_sample = b_sample.transpose(0, 3, 1, 2, 4).reshape(n_sample, D_B).astype(BF16)
        b_out = jnp.concatenate([b_prompt, b_sample, jnp.zeros((n_rows - n_tokens, D_B), BF16)])

        wr_pad = jnp.pad(w_router[l], ((0, 0), (0, LANES - N_EXPERTS)))
        wr_hi, wr_lo = _split_bf16(wr_pad)
        br_pad = jnp.pad(b_router[l], (0, LANES - N_EXPERTS), constant_values=NEG_INF)[None]
        x_mid, h, top_idx, gates = _out_router(a_out, b_out, x_all, w_out[l].astype(BF16), norm_ffn[l][None],
                                               wr_hi, wr_lo, br_pad)

        pos, slot_token, unit_expert, unit_rows, unit_block = _routing(
            top_idx[:n_tokens, :TOP_K_EXPERTS], n_units, zero_row=n_tokens)
        ys = _moe_experts(unit_expert, unit_rows, unit_block, slot_token, h, w_gate_up[l],
                          b_gate_up[l][:, None, :], w_down[l], b_down[l][:, None, :])
        last = l == depth - 1
        y = _combine(pos, gates, x_mid, norm_final[None] if last else jnp.ones((1, d_model), F32), ys, n_tokens)
        outs.append((k, vb, ki, va))
        if not last:
            raise NotImplementedError("multi-layer stacks need the un-normalised residual stream")

    def stack(j, lo, hi, shape):
        return jnp.stack([o[j][lo:hi].reshape(shape) for o in outs])

    y_prompt = y[:n_prompt].reshape(n_b, seq, d_model)
    y_sample = y[n_prompt:n_tokens].reshape(n_db, n_new, d_model)
    return (
        y_prompt, y_sample,
        stack(0, 0, n_prompt, (n_b, seq, N_KV_B, HEAD_DIM)),
        stack(1, 0, n_prompt, (n_b, seq, N_KV_B, HEAD_DIM)),
        stack(2, 0, n_prompt, (n_b, seq, IDX_DIM)),
        stack(0, n_prompt, n_tokens, (n_db, n_new, N_KV_B, HEAD_DIM)),
        stack(1, n_prompt, n_tokens, (n_db, n_new, N_KV_B, HEAD_DIM)),
        stack(2, n_prompt, n_tokens, (n_db, n_new, IDX_DIM)),
        stack(3, n_prompt, n_tokens, (n_db, n_new, N_HEADS_A, HEAD_DIM)),
    )
```

```python
import functools

import jax
import jax.numpy as jnp
from jax import lax
from jax.experimental import pallas as pl
from jax.experimental.pallas import tpu as pltpu

HEAD_DIM = 128
N_HEADS_A = 8
D_A = N_HEADS_A * HEAD_DIM
CHUNK = 128
N_HEADS_B = 8
N_KV_B = 2
KV_GROUP = N_HEADS_B // N_KV_B
D_B = N_HEADS_B * HEAD_DIM
D_KV_B = N_KV_B * HEAD_DIM
ROT_DIM = HEAD_DIM // 4
ROPE_THETA = 500000.0
N_IDX_HEADS = 16
IDX_DIM = 64
IDX_ROT_DIM = IDX_DIM // 4
TOPK_MAX = 256
N_EXPERTS = 32
TOP_K_EXPERTS = 4
SWIGLU_LIMIT = 7.0
SWIGLU_ALPHA = 1.702
RMS_EPS = 1e-5
PAGE_SIZE = 128

OFF_U = 0
OFF_V = OFF_U + D_A
OFF_Q = OFF_V + D_A
OFF_K = OFF_Q + D_B
OFF_VB = OFF_K + D_KV_B
OFF_QI = OFF_VB + D_KV_B
OFF_KI = OFF_QI + N_IDX_HEADS * IDX_DIM
OFF_WI = OFF_KI + IDX_DIM
IN_COLS = OFF_WI + N_IDX_HEADS

LANES = 128
SUBLANES = 8
VMEM_LIMIT_BYTES = 56 * 1024 * 1024

IN_COLS_PAD = pl.cdiv(IN_COLS, LANES) * LANES
INT_MIN = -(2 ** 31)
NEG_INF = float("-inf")

F32 = jnp.float32
BF16 = jnp.bfloat16


def _params(semantics):
    return pltpu.CompilerParams(dimension_semantics=semantics, vmem_limit_bytes=VMEM_LIMIT_BYTES)


def _dot(a, b):
    return jnp.dot(a, b, preferred_element_type=F32)


def _dot_nt(a, b):
    return lax.dot_general(a, b, (((1,), (1,)), ((), ())), preferred_element_type=F32)


IN_TM = 256


def _rope_lanes(z, cos, sin_lo, sin_hi, half):
    return z * cos + pltpu.roll(z, LANES - half, 1) * sin_lo + pltpu.roll(z, half, 1) * sin_hi


def _gelu(z):
    return 0.5 * z * (1.0 + lax.erf(z * (2.0 ** -0.5)))


def _group_rows(n_prompt_tiles, xp_ref, xs_ref):
    return jnp.where(pl.program_id(0) < n_prompt_tiles, xp_ref[...], xs_ref[...])


def _group_specs(n_prompt_tiles, tm, d_model):
    prompt = pl.BlockSpec((tm, d_model), lambda i: (jnp.minimum(i, n_prompt_tiles - 1), 0))
    decode = pl.BlockSpec((tm, d_model), lambda i: (jnp.maximum(i - n_prompt_tiles, 0), 0))
    return [prompt, decode]


def _in_proj_kernel(n_prompt_tiles, xp_ref, xs_ref, g_ref, w_ref, nva_ref, wmix_ref, bmix_ref,
                    cq_ref, saq_ref, sbq_ref, ci_ref, sai_ref, sbi_ref,
                    aout_ref, q_ref, k_ref, kb_ref, vb_ref, vbb_ref, qi_ref, ki_ref, kib_ref, wi_ref, va_ref):
    x = _group_rows(n_prompt_tiles, xp_ref, xs_ref)
    ms = jnp.mean(x * x, axis=-1, keepdims=True)
    xb = ((x * lax.rsqrt(ms + RMS_EPS)) * g_ref[...]).astype(BF16)

    u = _gelu(_dot(xb, w_ref[:, OFF_U:OFF_V]))
    gv = _gelu(_dot(xb, w_ref[:, OFF_V:OFF_Q]))
    va = gv * lax.rsqrt(jnp.mean(gv * gv, axis=-1, keepdims=True) + RMS_EPS) * nva_ref[...]
    va_ref[...] = va
    vab = va.astype(BF16)
    for sub in range(IN_TM // CHUNK):
        rows = slice(sub * CHUNK, (sub + 1) * CHUNK)
        for h in range(N_HEADS_A):
            cols = slice(h * HEAD_DIM, (h + 1) * HEAD_DIM)
            mixed = _dot(wmix_ref[h], vab[rows, cols]) + bmix_ref[:, h:h + 1]
            aout_ref[rows, cols] = (u[rows, cols] * mixed).astype(BF16)

    cq, saq, sbq = cq_ref[...], saq_ref[...], sbq_ref[...]
    zq = _dot(xb, w_ref[:, OFF_Q:OFF_K])
    for h in range(N_HEADS_B):
        cols = slice(h * HEAD_DIM, (h + 1) * HEAD_DIM)
        r = _rope_lanes(zq[:, cols], cq, saq, sbq, ROT_DIM // 2)
        q_ref[:, cols] = (r * (HEAD_DIM ** -0.5)).astype(BF16)
    zk = _dot(xb, w_ref[:, OFF_K:OFF_VB])
    for h in range(N_KV_B):
        cols = slice(h * HEAD_DIM, (h + 1) * HEAD_DIM)
        r = _rope_lanes(zk[:, cols], cq, saq, sbq, ROT_DIM // 2)
        k_ref[:, cols] = r
        kb_ref[:, cols] = r.astype(BF16)
    zvb = _dot(xb, w_ref[:, OFF_VB:OFF_QI])
    vb_ref[...] = zvb
    vbb_ref[...] = zvb.astype(BF16)

    ci, sai, sbi = ci_ref[...], sai_ref[...], sbi_ref[...]
    zqi = _dot(xb, w_ref[:, OFF_QI:OFF_KI])
    for p in range(N_IDX_HEADS * IDX_DIM // LANES):
        cols = slice(p * LANES, (p + 1) * LANES)
        qi_ref[:, cols] = _rope_lanes(zqi[:, cols], ci, sai, sbi, IDX_ROT_DIM // 2).astype(BF16)
    zl = _dot(xb, w_ref[:, OFF_KI:IN_COLS_PAD])
    r = _rope_lanes(zl, ci, sai, sbi, IDX_ROT_DIM // 2)
    ki_ref[...] = r[:, :IDX_DIM]
    lane = lax.broadcasted_iota(jnp.int32, r.shape, 1)
    kib_ref[...] = jnp.where(lane < IDX_DIM, r, pltpu.roll(r, IDX_DIM, 1)).astype(BF16)
    wi_ref[...] = zl[:, IDX_DIM:IDX_DIM + N_IDX_HEADS] * (N_IDX_HEADS ** -0.5 * IDX_DIM ** -0.5)


def _rope_tables(pos, rot_dim, head_dim):
    half = rot_dim // 2
    inv = ROPE_THETA ** (-jnp.arange(half, dtype=F32) / half)
    ang = pos.astype(F32)[:, None] * inv[None, :]
    cos, sin = jnp.cos(ang), jnp.sin(ang)
    lane = jnp.arange(LANES) % head_dim
    cos_l = jnp.take(cos, lane % half, axis=1)
    sin_l = jnp.take(sin, lane % half, axis=1)
    c = jnp.where(lane[None] < rot_dim, cos_l, 1.0)
    s_lo = jnp.where(lane[None] < half, -sin_l, 0.0)
    s_hi = jnp.where((lane[None] >= half) & (lane[None] < rot_dim), sin_l, 0.0)
    return c.astype(F32), s_lo.astype(F32), s_hi.astype(F32)


def _in_proj(x_p, x_s, tiles_per_seq, norm_mix, w_in_b, norm_v_a, wmix, bmix, tables):
    d_model = x_p.shape[1]
    n_prompt_tiles = x_p.shape[0] // IN_TM
    n_rows = x_p.shape[0] + x_s.shape[0]
    n_tiles = n_rows // IN_TM
    row = lambda i: (i, 0)
    const = lambda i: (0, 0)
    tab = lambda i: (jnp.where(i < n_prompt_tiles, i % tiles_per_seq, tiles_per_seq), 0)
    grp = lambda i: (jnp.where(i < n_prompt_tiles, 0, 1), 0, 0, 0)
    grp2 = lambda i: (jnp.where(i < n_prompt_tiles, 0, 1), 0, 0)
    tab_spec = pl.BlockSpec((IN_TM, LANES), tab)
    outs = [
        ((n_rows, D_A), BF16),
        ((n_rows, D_B), BF16),
        ((n_rows, D_KV_B), F32),
        ((n_rows, D_KV_B), BF16),
        ((n_rows, D_KV_B), F32),
        ((n_rows, D_KV_B), BF16),
        ((n_rows, N_IDX_HEADS * IDX_DIM), BF16),
        ((n_rows, IDX_DIM), F32),
        ((n_rows, LANES), BF16),
        ((n_rows, N_IDX_HEADS), F32),
        ((n_rows, D_A), F32),
    ]
    return pl.pallas_call(
        functools.partial(_in_proj_kernel, n_prompt_tiles),
        grid=(n_tiles,),
        in_specs=_group_specs(n_prompt_tiles, IN_TM, d_model) + [
            pl.BlockSpec((1, d_model), const),
            pl.BlockSpec((d_model, IN_COLS_PAD), const, pipeline_mode=pl.Buffered(1)),
            pl.BlockSpec((1, D_A), const),
            pl.BlockSpec((None, N_HEADS_A, CHUNK, CHUNK), grp),
            pl.BlockSpec((None, CHUNK, N_HEADS_A), grp2),
            tab_spec, tab_spec, tab_spec, tab_spec, tab_spec, tab_spec,
        ],
        out_specs=[pl.BlockSpec((IN_TM, s[1]), row) for s, _ in outs],
        out_shape=[jax.ShapeDtypeStruct(s, d) for s, d in outs],
        compiler_params=_params(("parallel",)),
        name="in_proj",
    )(x_p, x_s, norm_mix, w_in_b, norm_v_a, wmix, bmix, *tables)


ORDER_KEY_NEG_INF = -(2 ** 31) + 0x7FFFFF


def _order_key_to_float(key):
    return pltpu.bitcast(key ^ ((key >> 31) & jnp.int32(0x7FFFFFFF)), F32)


def _kth_largest(score, k_top):
    def body(it, t):
        cand = t + lax.shift_left(jnp.int32(1), jnp.int32(31) - it)
        cnt = jnp.sum((score >= _order_key_to_float(cand)).astype(F32), axis=1, keepdims=True)
        return jnp.where(cnt >= k_top, cand, t)

    t0 = jnp.full((score.shape[0], 1), INT_MIN, jnp.int32)
    t = lax.fori_loop(0, 32, body, t0)
    return _order_key_to_float(jnp.maximum(t, ORDER_KEY_NEG_INF))


def _topk_select(score, k_top, tri):
    thr = _kth_largest(score, k_top)
    above = score > thr
    need = k_top - jnp.sum(above.astype(F32), axis=1, keepdims=True)
    tie = score == thr
    pieces = []
    run = jnp.zeros_like(need)
    for j in range(score.shape[1] // LANES):
        cols = slice(j * LANES, (j + 1) * LANES)
        tie_j = tie[:, cols]
        prefix = _dot(tie_j.astype(BF16), tri) + run
        run = prefix[:, LANES - 1:LANES]
        pieces.append(jnp.where(above[:, cols] | (tie_j & (prefix <= need)), 1.0, 0.0))
    return jnp.concatenate(pieces, axis=1)


def _tri_matrix():
    r = lax.broadcasted_iota(jnp.int32, (LANES, LANES), 0)
    c = lax.broadcasted_iota(jnp.int32, (LANES, LANES), 1)
    return (r <= c).astype(BF16)


PA_TQ = 128


PA_EXTENTS = 4


def _prompt_attn_kernel(k_top, qi_ref, wi_ref, kib_ref, q_ref, kb_ref, vbb_ref, o_ref):
    i = pl.program_id(1)
    step = kb_ref.shape[0] // PA_EXTENTS
    blocks_per_step = step // PA_TQ
    for c in range(PA_EXTENTS):
        @pl.when(i // blocks_per_step == c)
        def _():
            _prompt_attn_block(k_top, (c + 1) * step, i, qi_ref, wi_ref, kib_ref, q_ref, kb_ref, vbb_ref, o_ref)


def _prompt_attn_block(k_top, seq, i, qi_ref, wi_ref, kib_ref, q_ref, kb_ref, vbb_ref, o_ref):
    qi = qi_ref[...]
    wi = wi_ref[...]
    ki2 = kib_ref[:seq, :]
    lane = lax.broadcasted_iota(jnp.int32, (PA_TQ, LANES), 1)
    score = jnp.zeros((PA_TQ, seq), F32)
    for h in range(N_IDX_HEADS):
        pair = qi[:, (h // 2) * LANES:(h // 2 + 1) * LANES]
        keep = (lane < IDX_DIM) if h % 2 == 0 else (lane >= IDX_DIM)
        d = _dot_nt(jnp.where(keep, pair, jnp.zeros_like(pair)), ki2)
        score = score + wi[:, h:h + 1] * jnp.maximum(d, 0.0)
    q_pos = i * PA_TQ + lax.broadcasted_iota(jnp.int32, (PA_TQ, seq), 0)
    s_pos = lax.broadcasted_iota(jnp.int32, (PA_TQ, seq), 1)
    causal = s_pos <= q_pos
    sel = _topk_select(jnp.where(causal, score, NEG_INF), k_top, _tri_matrix())
    bias = jnp.where(causal & (sel > 0.0), 0.0, NEG_INF)

    q = q_ref[...]
    for g in range(N_KV_B):
        q4 = jnp.concatenate(
            [q[:, (g * KV_GROUP + r) * HEAD_DIM:(g * KV_GROUP + r + 1) * HEAD_DIM] for r in range(KV_GROUP)], axis=0)
        kg = kb_ref[:seq, g * HEAD_DIM:(g + 1) * HEAD_DIM]
        vg = vbb_ref[:seq, g * HEAD_DIM:(g + 1) * HEAD_DIM]
        s = _dot_nt(q4, kg).reshape(KV_GROUP, PA_TQ, seq) + bias[None]
        m = jnp.max(s, axis=-1, keepdims=True)
        p = jnp.exp(s - m)
        l = jnp.sum(p, axis=-1, keepdims=True)
        o = _dot(p.reshape(KV_GROUP * PA_TQ, seq).astype(BF16), vg).reshape(KV_GROUP, PA_TQ, HEAD_DIM) / l
        for r in range(KV_GROUP):
            o_ref[:, (g * KV_GROUP + r) * HEAD_DIM:(g * KV_GROUP + r + 1) * HEAD_DIM] = o[r].astype(BF16)


def _prompt_attn(n_batch, seq, k_top, qi, wi, kib, q, kb, vbb):
    nq = seq // PA_TQ
    qrow = lambda b, i: (b * nq + i, 0)
    krow = lambda b, i: (b, 0)
    return pl.pallas_call(
        functools.partial(_prompt_attn_kernel, k_top),
        grid=(n_batch, nq),
        in_specs=[
            pl.BlockSpec((PA_TQ, N_IDX_HEADS * IDX_DIM), qrow),
            pl.BlockSpec((PA_TQ, N_IDX_HEADS), qrow),
            pl.BlockSpec((seq, LANES), krow),
            pl.BlockSpec((PA_TQ, D_B), qrow),
            pl.BlockSpec((seq, D_KV_B), krow),
            pl.BlockSpec((seq, D_KV_B), krow),
        ],
        out_specs=pl.BlockSpec((PA_TQ, D_B), qrow),
        out_shape=jax.ShapeDtypeStruct((n_batch * seq, D_B), BF16),
        compiler_params=_params(("parallel", "parallel")),
        name="prompt_attn",
    )(qi, wi, kib, q, kb, vbb)


SP_PAGES = 16
SP_KEYS = SP_PAGES * PAGE_SIZE
SP_TPAD = SUBLANES


def _page_specs(tail, n_steps, layer):
    def make(p):
        def index(b, j, pt):
            return (layer, pt[b, jnp.minimum(j, n_steps - 1) * SP_PAGES + p]) + (0,) * (1 + len(tail))
        return pl.BlockSpec((None, None, PAGE_SIZE) + tail, index)
    return [make(p) for p in range(SP_PAGES)]


def _sample_scores_kernel(n_steps, pt_ref, qs_ref, ws_ref, kin_ref, *refs):
    page_refs, o_ref = refs[:SP_PAGES], refs[SP_PAGES]
    j = pl.program_id(1)
    qs = qs_ref[...]
    ws = ws_ref[...]

    def head_sum(d):
        w = ws * jnp.maximum(d, 0.0)
        return jnp.sum(w.reshape(N_IDX_HEADS, SP_TPAD, d.shape[1]), axis=0)

    @pl.when(j < n_steps)
    def _():
        keys = jnp.concatenate([r[...] for r in page_refs], axis=0).astype(BF16)
        o_ref[...] = head_sum(_dot_nt(qs, keys))

    @pl.when(j == n_steps)
    def _():
        d = _dot_nt(qs, kin_ref[...])
        o_ref[...] = jnp.concatenate(
            [head_sum(d), jnp.zeros((SP_TPAD, SP_KEYS - d.shape[1]), F32)], axis=1)


def _sample_scores(page_table, qs, ws, ki_new, cache_idx_k, layer):
    n_db, n_pages = page_table.shape
    n_steps = n_pages // SP_PAGES
    rows = N_IDX_HEADS * SP_TPAD
    per_b = lambda b, j, pt: (b, 0, 0)
    grid_spec = pltpu.PrefetchScalarGridSpec(
        num_scalar_prefetch=1,
        grid=(n_db, n_steps + 1),
        in_specs=[
            pl.BlockSpec((None, rows, IDX_DIM), per_b),
            pl.BlockSpec((None, rows, 1), per_b),
            pl.BlockSpec((None, LANES, IDX_DIM), per_b),
        ] + _page_specs((IDX_DIM,), n_steps, layer),
        out_specs=pl.BlockSpec((None, SP_TPAD, SP_KEYS), lambda b, j, pt: (b, 0, j)),
    )
    return pl.pallas_call(
        functools.partial(_sample_scores_kernel, n_steps),
        grid_spec=grid_spec,
        out_shape=jax.ShapeDtypeStruct((n_db, SP_TPAD, (n_steps + 1) * SP_KEYS), F32),
        compiler_params=_params(("parallel", "arbitrary")),
        name="sample_scores",
    )(page_table, qs, ws, ki_new, *([cache_idx_k] * SP_PAGES))


SM_ROWS = 64


def _sample_mask_kernel(k_top, past, n_new, s_ref, o_ref):
    score = s_ref[...]
    t = lax.broadcasted_iota(jnp.int32, score.shape, 0) % SP_TPAD
    s_pos = lax.broadcasted_iota(jnp.int32, score.shape, 1)
    admissible = s_pos <= past + t
    sel = _topk_select(jnp.where(admissible, score, NEG_INF), k_top, _tri_matrix())
    o_ref[...] = jnp.where((admissible & (sel > 0.0)) | (t >= n_new), 0.0, NEG_INF)


def _sample_mask(scores2d, k_top, past, n_new):
    n_rows, width = scores2d.shape
    return pl.pallas_call(
        functools.partial(_sample_mask_kernel, k_top, past, n_new),
        grid=(n_rows // SM_ROWS,),
        in_specs=[pl.BlockSpec((SM_ROWS, width), lambda i: (i, 0))],
        out_specs=pl.BlockSpec((SM_ROWS, width), lambda i: (i, 0)),
        out_shape=jax.ShapeDtypeStruct((n_rows, width), F32),
        compiler_params=_params(("parallel",)),
        name="sample_mask",
    )(scores2d)


SA_ROWS = KV_GROUP * SP_TPAD


def _sample_attn_kernel(n_steps, pt_ref, q_ref, bias_ref, kn_ref, vn_ref, *refs):
    k_refs, v_refs = refs[:SP_PAGES], refs[SP_PAGES:2 * SP_PAGES]
    o_ref, m_ref, l_ref, acc_ref = refs[2 * SP_PAGES:]
    j = pl.program_id(1)

    @pl.when(j == 0)
    def _():
        m_ref[...] = jnp.full(m_ref.shape, NEG_INF, F32)
        l_ref[...] = jnp.zeros(l_ref.shape, F32)
        acc_ref[...] = jnp.zeros(acc_ref.shape, F32)

    def update(g, kg, vg, bias):
        s = _dot_nt(q_ref[g], kg) + jnp.concatenate([bias] * KV_GROUP, axis=0)
        m_old = m_ref[g]
        m_new = jnp.maximum(m_old, jnp.max(s, axis=-1, keepdims=True))
        m_safe = jnp.where(m_new == NEG_INF, 0.0, m_new)
        alpha = jnp.exp(m_old - m_safe)
        p = jnp.exp(s - m_safe)
        l_ref[g] = alpha * l_ref[g] + jnp.sum(p, axis=-1, keepdims=True)
        acc_ref[g] = alpha * acc_ref[g] + _dot(p.astype(BF16), vg)
        m_ref[g] = m_new

    @pl.when(j < n_steps)
    def _():
        bias = bias_ref[...]
        for g in range(N_KV_B):
            cols = slice(g * HEAD_DIM, (g + 1) * HEAD_DIM)
            kg = jnp.concatenate([r[:, g, :] for r in k_refs], axis=0).astype(BF16)
            vg = jnp.concatenate([r[:, g, :] for r in v_refs], axis=0).astype(BF16)
            update(g, kg, vg, bias)

    @pl.when(j == n_steps)
    def _():
        bias = bias_ref[:, :LANES]
        for g in range(N_KV_B):
            cols = slice(g * HEAD_DIM, (g + 1) * HEAD_DIM)
            update(g, kn_ref[:, cols], vn_ref[:, cols], bias)
            o_ref[g] = acc_ref[g] / l_ref[g]


def _sample_attn(page_table, qs, bias, k_new, v_new, cache_k, cache_v, layer):
    n_db, n_pages = page_table.shape
    n_steps = n_pages // SP_PAGES
    per_b3 = lambda b, j, pt: (b, 0, 0)
    per_b4 = lambda b, j, pt: (b, 0, 0, 0)
    grid_spec = pltpu.PrefetchScalarGridSpec(
        num_scalar_prefetch=1,
        grid=(n_db, n_steps + 1),
        in_specs=[
            pl.BlockSpec((None, N_KV_B, SA_ROWS, HEAD_DIM), per_b4),
            pl.BlockSpec((None, SP_TPAD, SP_KEYS), lambda b, j, pt: (b, 0, j)),
            pl.BlockSpec((None, LANES, D_KV_B), per_b3),
            pl.BlockSpec((None, LANES, D_KV_B), per_b3),
        ] + 2 * _page_specs((N_KV_B, HEAD_DIM), n_steps, layer),
        out_specs=pl.BlockSpec((None, N_KV_B, SA_ROWS, HEAD_DIM), per_b4),
        scratch_shapes=[
            pltpu.VMEM((N_KV_B, SA_ROWS, 1), F32),
            pltpu.VMEM((N_KV_B, SA_ROWS, 1), F32),
            pltpu.VMEM((N_KV_B, SA_ROWS, HEAD_DIM), F32),
        ],
    )
    return pl.pallas_call(
        functools.partial(_sample_attn_kernel, n_steps),
        grid_spec=grid_spec,
        out_shape=jax.ShapeDtypeStruct((n_db, N_KV_B, SA_ROWS, HEAD_DIM), F32),
        compiler_params=_params(("parallel", "arbitrary")),
        name="sample_attn",
    )(page_table, qs, bias, k_new, v_new, *([cache_k] * SP_PAGES), *([cache_v] * SP_PAGES))


OR_TM = 256


def _split_bf16(x):
    hi = x.astype(BF16)
    return hi, (x - hi.astype(F32)).astype(BF16)


def _out_router_kernel(n_prompt_tiles, a_ref, b_ref, xp_ref, xs_ref, wo_ref, nf_ref, wrh_ref, wrl_ref, br_ref,
                       xmid_ref, h_ref, idx_ref, gate_ref):
    x = _group_rows(n_prompt_tiles, xp_ref, xs_ref)
    x_mid = x + _dot(a_ref[...], wo_ref[:D_A, :]) + _dot(b_ref[...], wo_ref[D_A:, :])
    xmid_ref[...] = x_mid
    h = (x_mid * lax.rsqrt(jnp.mean(x_mid * x_mid, axis=-1, keepdims=True) + RMS_EPS)) * nf_ref[...]
    h_ref[...] = h
    h_hi, h_lo = _split_bf16(h)
    wr_hi = wrh_ref[...]
    logits = _dot(h_hi, wr_hi) + _dot(h_lo, wr_hi) + _dot(h_hi, wrl_ref[...]) + br_ref[...]
    lane = lax.broadcasted_iota(jnp.int32, logits.shape, 1)
    vals, idxs = [], []
    for _ in range(TOP_K_EXPERTS):
        m = jnp.max(logits, axis=-1, keepdims=True)
        i = jnp.min(jnp.where(logits == m, lane, LANES), axis=-1, keepdims=True)
        vals.append(m)
        idxs.append(i)
        logits = jnp.where(lane == i, NEG_INF, logits)
    exps = [jnp.exp(v - vals[0]) for v in vals]
    total = exps[0]
    for e in exps[1:]:
        total = total + e
    idx_out = jnp.zeros(logits.shape, jnp.int32)
    gate_out = jnp.zeros(logits.shape, F32)
    for k in range(TOP_K_EXPERTS):
        idx_out = jnp.where(lane == k, idxs[k], idx_out)
        gate_out = jnp.where(lane == k, exps[k] / total, gate_out)
    idx_ref[...] = idx_out
    gate_ref[...] = gate_out


def _out_router(a_out, b_out, x_p, x_s, w_out_b, norm_ffn, wr_hi, wr_lo, br_pad):
    d_model = x_p.shape[1]
    n_prompt_tiles = x_p.shape[0] // OR_TM
    n_rows = x_p.shape[0] + x_s.shape[0]
    row = lambda i: (i, 0)
    const = lambda i: (0, 0)
    return pl.pallas_call(
        functools.partial(_out_router_kernel, n_prompt_tiles),
        grid=(n_rows // OR_TM,),
        in_specs=[
            pl.BlockSpec((OR_TM, D_A), row),
            pl.BlockSpec((OR_TM, D_B), row),
        ] + _group_specs(n_prompt_tiles, OR_TM, d_model) + [
            pl.BlockSpec((D_A + D_B, d_model), const),
            pl.BlockSpec((1, d_model), const),
            pl.BlockSpec((d_model, LANES), const),
            pl.BlockSpec((d_model, LANES), const),
            pl.BlockSpec((1, LANES), const),
        ],
        out_specs=[
            pl.BlockSpec((OR_TM, d_model), row),
            pl.BlockSpec((OR_TM, d_model), row),
            pl.BlockSpec((OR_TM, LANES), row),
            pl.BlockSpec((OR_TM, LANES), row),
        ],
        out_shape=[
            jax.ShapeDtypeStruct((n_rows, d_model), F32),
            jax.ShapeDtypeStruct((n_rows, d_model), F32),
            jax.ShapeDtypeStruct((n_rows, LANES), jnp.int32),
            jax.ShapeDtypeStruct((n_rows, LANES), F32),
        ],
        compiler_params=_params(("parallel",)),
        name="out_router",
    )(a_out, b_out, x_p, x_s, w_out_b, norm_ffn, wr_hi, wr_lo, br_pad)


MOE_TM = 256
MOE_UNIT_TILES = 5
MOE_R = MOE_TM * MOE_UNIT_TILES
MOE_TF = 256
COMBINE_TM = 128
ROW_COPY_UNROLL = 8


def _row_copy(src_ref, src_row, dst_ref, dst_row, sem):
    return pltpu.make_async_copy(src_ref.at[pl.ds(src_row, 1), :], dst_ref.at[pl.ds(dst_row, 1), :], sem)


def _tile_rows(rows):
    return pl.cdiv(rows, MOE_TM) * MOE_TM


def _moe_kernel(ue_ref, ur_ref, ub_ref, tok_ref, tokn_ref, h_ref, wg_ref, wu_ref, bg_ref, bu_ref, wd_ref, bd_ref,
                ys_ref, land_ref, xb_ref, y_ref, wgb_ref, wub_ref, wdb_ref, gsem, ysem):
    u, f = pl.program_id(0), pl.program_id(1)
    n_units, n_f = pl.num_programs(0), pl.num_programs(1)
    rows = ur_ref[u]
    n_tiles = pl.cdiv(rows, MOE_TM)

    def gather(tokens_ref, n_rows, wait):
        def tile(m, c):
            def body(j, c2):
                r = m * MOE_TM + j
                cp = _row_copy(h_ref, tokens_ref[0, r], land_ref, r, gsem)
                cp.wait() if wait else cp.start()
                return c2
            return lax.fori_loop(0, MOE_TM, body, c, unroll=ROW_COPY_UNROLL)
        lax.fori_loop(0, n_rows // MOE_TM, tile, 0)

    def writeback(unit, n, wait):
        def body(m, c):
            r0 = pl.multiple_of(m * MOE_TM, MOE_TM)
            cp = pltpu.make_async_copy(y_ref.at[pl.ds(r0, MOE_TM), :],
                                       ys_ref.at[pl.ds(ub_ref[unit] * MOE_R + r0, MOE_TM), :], ysem)
            cp.wait() if wait else cp.start()
            return c
        lax.fori_loop(0, n, body, 0)

    @pl.when((f == 0) & (rows > 0))
    def _():
        @pl.when(u == 0)
        def _():
            gather(tok_ref, _tile_rows(rows), wait=False)

        gather(tok_ref, _tile_rows(rows), wait=True)

        def convert(m, c):
            r0 = pl.multiple_of(m * MOE_TM, MOE_TM)
            xb_ref[pl.ds(r0, MOE_TM), :] = land_ref[pl.ds(r0, MOE_TM), :].astype(BF16)
            return c
        lax.fori_loop(0, n_tiles, convert, 0)

        nxt = jnp.minimum(u + 1, n_units - 1)
        rows_next = jnp.where(u + 1 < n_units, ur_ref[nxt], 0)
        gather(tokn_ref, _tile_rows(rows_next), wait=False)

    @pl.when((f == 0) & (u > 0))
    def _():
        writeback(u - 1, pl.cdiv(ur_ref[jnp.maximum(u - 1, 0)], MOE_TM), wait=True)

    @pl.when(rows > 0)
    def _():
        wgb_ref[...] = wg_ref[...].astype(BF16)
        wub_ref[...] = wu_ref[...].astype(BF16)
        wdb_ref[...] = wd_ref[...].astype(BF16)

        def tile(m, c):
            r0 = pl.multiple_of(m * MOE_TM, MOE_TM)
            xm = xb_ref[pl.ds(r0, MOE_TM), :]
            gate = jnp.minimum(_dot(xm, wgb_ref[...]) + bg_ref[...], SWIGLU_LIMIT)
            up = jnp.clip(_dot(xm, wub_ref[...]) + bu_ref[...], -SWIGLU_LIMIT, SWIGLU_LIMIT)
            act = (up + 1.0) * (gate * jax.nn.sigmoid(SWIGLU_ALPHA * gate))
            y_ref[pl.ds(r0, MOE_TM), :] += _dot(act.astype(BF16), wdb_ref[...])
            return c

        @pl.when(f == 0)
        def _():
            def init(m, c):
                r0 = pl.multiple_of(m * MOE_TM, MOE_TM)
                y_ref[pl.ds(r0, MOE_TM), :] = jnp.broadcast_to(bd_ref[...], (MOE_TM, y_ref.shape[1]))
                return c
            lax.fori_loop(0, n_tiles, init, 0)

        lax.fori_loop(0, n_tiles, tile, 0)

        @pl.when(f == n_f - 1)
        def _():
            writeback(u, n_tiles, wait=False)

    @pl.when((f == n_f - 1) & (u == n_units - 1))
    def _():
        writeback(u, n_tiles, wait=True)


def _moe_experts(unit_expert, unit_rows, unit_block, slot_token, h, w_gate_up, b_gate_up, w_down, b_down):
    n_units = unit_expert.shape[0]
    d_model = h.shape[1]
    d_ff = w_down.shape[1]
    n_f = d_ff // MOE_TF

    def f_eff(u, f, ur):
        return jnp.where(ur[u] > 0, f, n_f - 1)

    smem_tokens = lambda index: pl.BlockSpec((None, 1, MOE_R), index, memory_space=pltpu.SMEM)
    grid_spec = pltpu.PrefetchScalarGridSpec(
        num_scalar_prefetch=3,
        grid=(n_units, n_f),
        in_specs=[
            smem_tokens(lambda u, f, ue, ur, ub: (ub[u], 0, 0)),
            smem_tokens(lambda u, f, ue, ur, ub: (ub[jnp.minimum(u + 1, n_units - 1)], 0, 0)),
            pl.BlockSpec(memory_space=pl.ANY),
            pl.BlockSpec((None, d_model, MOE_TF), lambda u, f, ue, ur, ub: (ue[u], 0, f_eff(u, f, ur))),
            pl.BlockSpec((None, d_model, MOE_TF), lambda u, f, ue, ur, ub: (ue[u], 0, n_f + f_eff(u, f, ur))),
            pl.BlockSpec((None, 1, MOE_TF), lambda u, f, ue, ur, ub: (ue[u], 0, f_eff(u, f, ur))),
            pl.BlockSpec((None, 1, MOE_TF), lambda u, f, ue, ur, ub: (ue[u], 0, n_f + f_eff(u, f, ur))),
            pl.BlockSpec((None, MOE_TF, d_model), lambda u, f, ue, ur, ub: (ue[u], f_eff(u, f, ur), 0)),
            pl.BlockSpec((None, 1, d_model), lambda u, f, ue, ur, ub: (ue[u], 0, 0)),
        ],
        out_specs=pl.BlockSpec(memory_space=pl.ANY),
        scratch_shapes=[
            pltpu.VMEM((MOE_R, d_model), F32),
            pltpu.VMEM((MOE_R, d_model), BF16),
            pltpu.VMEM((MOE_R, d_model), F32),
            pltpu.VMEM((d_model, MOE_TF), BF16),
            pltpu.VMEM((d_model, MOE_TF), BF16),
            pltpu.VMEM((MOE_TF, d_model), BF16),
            pltpu.SemaphoreType.DMA,
            pltpu.SemaphoreType.DMA,
        ],
    )
    tokens = slot_token.reshape(n_units, 1, MOE_R)
    return pl.pallas_call(
        _moe_kernel,
        grid_spec=grid_spec,
        out_shape=jax.ShapeDtypeStruct((n_units * MOE_R, d_model), F32),
        compiler_params=_params(("arbitrary", "arbitrary")),
        name="moe_experts",
    )(unit_expert, unit_rows, unit_block, tokens, tokens, h, w_gate_up, w_gate_up, b_gate_up, b_gate_up,
      w_down, b_down)


def _combine_kernel(pos_ref, posn_ref, gate_ref, xmid_ref, nfin_ref, ys_ref, o_ref, buf_ref, sem):
    i, n_tiles = pl.program_id(0), pl.num_programs(0)
    slot = i % 2

    def rows(p_ref, s, wait):
        for k in range(TOP_K_EXPERTS):
            def body(t, c):
                cp = _row_copy(ys_ref, p_ref[0, t * TOP_K_EXPERTS + k], buf_ref.at[s], k * COMBINE_TM + t,
                               sem.at[s])
                cp.wait() if wait else cp.start()
                return c
            lax.fori_loop(0, COMBINE_TM, body, 0, unroll=ROW_COPY_UNROLL)

    @pl.when(i == 0)
    def _():
        rows(pos_ref, 0, wait=False)

    @pl.when(i + 1 < n_tiles)
    def _():
        rows(posn_ref, 1 - slot, wait=False)

    rows(pos_ref, slot, wait=True)
    gates = gate_ref[...]
    out = xmid_ref[...]
    for k in range(TOP_K_EXPERTS):
        out = out + gates[:, k:k + 1] * buf_ref[slot, k * COMBINE_TM:(k + 1) * COMBINE_TM, :]
    o_ref[...] = (out * lax.rsqrt(jnp.mean(out * out, axis=-1, keepdims=True) + RMS_EPS)) * nfin_ref[...]


def _combine(pos, gates, x_mid, norm_final, ys, n_tokens):
    d_model = x_mid.shape[1]
    n_tiles = n_tokens // COMBINE_TM
    n = COMBINE_TM * TOP_K_EXPERTS
    pos3 = pos.reshape(n_tiles, 1, n)
    row = lambda i: (i, 0)
    return pl.pallas_call(
        _combine_kernel,
        grid=(n_tiles,),
        in_specs=[
            pl.BlockSpec((None, 1, n), lambda i: (i, 0, 0), memory_space=pltpu.SMEM),
            pl.BlockSpec((None, 1, n), lambda i: (jnp.minimum(i + 1, n_tiles - 1), 0, 0), memory_space=pltpu.SMEM),
            pl.BlockSpec((COMBINE_TM, LANES), row),
            pl.BlockSpec((COMBINE_TM, d_model), row),
            pl.BlockSpec((1, d_model), lambda i: (0, 0)),
            pl.BlockSpec(memory_space=pl.ANY),
        ],
        out_specs=pl.BlockSpec((COMBINE_TM, d_model), row),
        out_shape=jax.ShapeDtypeStruct((n_tokens, d_model), F32),
        scratch_shapes=[pltpu.VMEM((2, n, d_model), F32), pltpu.SemaphoreType.DMA((2,))],
        compiler_params=_params(("arbitrary",)),
        name="combine",
    )(pos3, pos3, gates, x_mid, norm_final, ys)


def _routing(top_idx, n_units, zero_row):
    flat_e = top_idx.reshape(-1)
    onehot = (flat_e[:, None] == jnp.arange(N_EXPERTS, dtype=jnp.int32)[None, :]).astype(jnp.int32)
    csum = jnp.cumsum(onehot, axis=0)
    rank = jnp.take_along_axis(csum, flat_e[:, None], axis=1)[:, 0] - 1
    counts = csum[-1]
    units_e = (counts + MOE_R - 1) // MOE_R
    units_end = jnp.cumsum(units_e)
    units_start = units_end - units_e
    pos = (units_start[flat_e] + rank // MOE_R) * MOE_R + rank % MOE_R
    u = jnp.arange(n_units, dtype=jnp.int32)
    total = units_end[-1]
    ue = jnp.minimum(jnp.sum((units_end[None, :] <= u[:, None]).astype(jnp.int32), axis=1), N_EXPERTS - 1)
    rows = jnp.clip(counts[ue] - (u - units_start[ue]) * MOE_R, 0, MOE_R)
    used = u < total
    last_e = ue[jnp.maximum(total - 1, 0)]
    unit_expert = jnp.where(used, ue, last_e).astype(jnp.int32)
    unit_rows = jnp.where(used, rows, 0).astype(jnp.int32)
    unit_block = jnp.minimum(u, total - 1).astype(jnp.int32)
    pos = pos.astype(jnp.int32)
    token = jnp.arange(flat_e.shape[0], dtype=jnp.int32) // TOP_K_EXPERTS
    slot_token = jnp.full((n_units * MOE_R,), zero_row, jnp.int32).at[pos].set(token, unique_indices=True)
    return pos, slot_token, unit_expert, unit_rows, unit_block


def kernel(x_prompt, x_sample, cache_k, cache_v, cache_idx_k, page_table, norm_mix, w_in, norm_v_a,
           w_spatial, b_spatial, w_out, norm_ffn, w_router, b_router, w_gate_up, b_gate_up, w_down,
           b_down, norm_final):
    n_b, seq, d_model = x_prompt.shape
    n_db, n_new, _ = x_sample.shape
    depth = w_in.shape[0]
    past = page_table.shape[1] * PAGE_SIZE
    n_prompt, n_sample = n_b * seq, n_db * n_new
    n_tokens = n_prompt + n_sample
    assert seq % IN_TM == 0 and seq % PA_TQ == 0 and CHUNK % n_new == 0 and n_new <= SP_TPAD
    assert n_sample % CHUNK == 0 and page_table.shape[1] % SP_PAGES == 0
    assert n_tokens % COMBINE_TM == 0
    n_rows = (n_tokens // IN_TM + 1) * IN_TM
    k_top_prompt = min(TOPK_MAX, seq // 4)
    k_top_sample = min(TOPK_MAX, (past + n_new) // 4)
    n_assign = n_tokens * TOP_K_EXPERTS
    n_units = N_EXPERTS + n_assign // MOE_R

    pos_rows = jnp.concatenate([jnp.arange(seq, dtype=jnp.int32),
                                past + jnp.arange(IN_TM, dtype=jnp.int32) % n_new])
    tables = _rope_tables(pos_rows, ROT_DIM, HEAD_DIM) + _rope_tables(pos_rows, IDX_ROT_DIM, IDX_DIM)

    x_p = x_prompt.reshape(n_prompt, d_model)
    x_s = jnp.pad(x_sample.reshape(n_sample, d_model), ((0, n_rows - n_tokens), (0, 0)))
    outs = []
    for l in range(depth):
        causal = jnp.tril(jnp.ones((CHUNK, CHUNK), bool))
        w_prompt = jnp.where(causal[None], w_spatial[l], 0.0)
        eye = jnp.eye(CHUNK // n_new, dtype=F32)
        w_decode = jax.vmap(lambda w: jnp.kron(eye, w))(w_prompt[:, :n_new, :n_new])
        wmix = jnp.stack([w_prompt, w_decode]).astype(BF16)
        b_decode = jnp.tile(b_spatial[l][:, :n_new], (1, CHUNK // n_new))
        bmix = jnp.stack([b_spatial[l].T, b_decode.T])
        w_in_b = jnp.pad(w_in[l], ((0, 0), (0, IN_COLS_PAD - IN_COLS))).astype(BF16)

        (a_out, q, k, kb, vb, vbb, qi, ki, kib, wi, va) = _in_proj(
            x_p, x_s, seq // IN_TM, norm_mix[l][None], w_in_b, norm_v_a[l][None], wmix, bmix, tables)

        b_prompt = _prompt_attn(n_b, seq, k_top_prompt, qi, wi, kib, q, kb, vbb)

        sl = slice(n_prompt, n_tokens)
        pad_t = ((0, 0), (0, 0), (0, SP_TPAD - n_new), (0, 0))
        qi_s = qi[sl].reshape(n_db, n_new, N_IDX_HEADS, IDX_DIM).transpose(0, 2, 1, 3)
        qs_idx = jnp.pad(qi_s, pad_t).reshape(n_db, N_IDX_HEADS * SP_TPAD, IDX_DIM)
        wi_s = wi[sl].reshape(n_db, n_new, N_IDX_HEADS).transpose(0, 2, 1)
        ws_idx = jnp.pad(wi_s, ((0, 0), (0, 0), (0, SP_TPAD - n_new))).reshape(n_db, N_IDX_HEADS * SP_TPAD, 1)
        pad_rows = ((0, 0), (0, LANES - n_new), (0, 0))
        ki_new = jnp.pad(kib[sl, :IDX_DIM].reshape(n_db, n_new, IDX_DIM), pad_rows)
        scores = _sample_scores(page_table, qs_idx, ws_idx, ki_new, cache_idx_k, l)
        width = scores.shape[-1]
        bias = _sample_mask(scores.reshape(n_db * SP_TPAD, width), k_top_sample, past, n_new)
        q_s = q[sl].reshape(n_db, n_new, N_KV_B, KV_GROUP, HEAD_DIM).transpose(0, 2, 3, 1, 4)
        qs_att = jnp.pad(q_s, ((0, 0), (0, 0), (0, 0), (0, SP_TPAD - n_new), (0, 0))).reshape(
            n_db, N_KV_B, SA_ROWS, HEAD_DIM)
        k_new = jnp.pad(kb[sl].reshape(n_db, n_new, D_KV_B), pad_rows)
        v_new = jnp.pad(vbb[sl].reshape(n_db, n_new, D_KV_B), pad_rows)
        o_s = _sample_attn(page_table, qs_att, bias.reshape(n_db, SP_TPAD, width), k_new, v_new,
                           cache_k, cache_v, l)
        b_sample = o_s.reshape(n_db, N_KV_B, KV_GROUP, SP_TPAD, HEAD_DIM)[:, :, :, :n_new]
        b_sample = b_sample.transpose(0, 3, 1, 2, 4).reshape(n_sample, D_B).astype(BF16)
        b_out = jnp.concatenate([b_prompt, b_sample, jnp.zeros((n_rows - n_tokens, D_B), BF16)])

        wr_pad = jnp.pad(w_router[l], ((0, 0), (0, LANES - N_EXPERTS)))
        wr_hi, wr_lo = _split_bf16(wr_pad)
        br_pad = jnp.pad(b_router[l], (0, LANES - N_EXPERTS), constant_values=NEG_INF)[None]
        x_mid, h, top_idx, gates = _out_router(a_out, b_out, x_p, x_s, w_out[l].astype(BF16),
                                               norm_ffn[l][None], wr_hi, wr_lo, br_pad)

        pos, slot_token, unit_expert, unit_rows, unit_block = _routing(
            top_idx[:n_tokens, :TOP_K_EXPERTS], n_units, zero_row=n_tokens)
        ys = _moe_experts(unit_expert, unit_rows, unit_block, slot_token, h, w_gate_up[l],
                          b_gate_up[l][:, None, :], w_down[l], b_down[l][:, None, :])
        last = l == depth - 1
        y = _combine(pos, gates, x_mid, norm_final[None] if last else jnp.ones((1, d_model), F32), ys, n_tokens)
        outs.append((k, vb, ki, va))
        if not last:
            raise NotImplementedError("multi-layer stacks need the un-normalised residual stream")

    def stack(j, lo, hi, shape):
        return jnp.stack([o[j][lo:hi].reshape(shape) for o in outs])

    y_prompt = y[:n_prompt].reshape(n_b, seq, d_model)
    y_sample = y[n_prompt:n_tokens].reshape(n_db, n_new, d_model)
    return (
        y_prompt, y_sample,
        stack(0, 0, n_prompt, (n_b, seq, N_KV_B, HEAD_DIM)),
        stack(1, 0, n_prompt, (n_b, seq, N_KV_B, HEAD_DIM)),
        stack(2, 0, n_prompt, (n_b, seq, IDX_DIM)),
        stack(0, n_prompt, n_tokens, (n_db, n_new, N_KV_B, HEAD_DIM)),
        stack(1, n_prompt, n_tokens, (n_db, n_new, N_KV_B, HEAD_DIM)),
        stack(2, n_prompt, n_tokens, (n_db, n_new, IDX_DIM)),
        stack(3, n_prompt, n_tokens, (n_db, n_new, N_HEADS_A, HEAD_DIM)),
    )
```

```python
import functools

import jax
import jax.numpy as jnp
from jax import lax
from jax.experimental import pallas as pl
from jax.experimental.pallas import tpu as pltpu

HEAD_DIM = 128
N_HEADS_A = 8
D_A = N_HEADS_A * HEAD_DIM
CHUNK = 128
N_HEADS_B = 8
N_KV_B = 2
KV_GROUP = N_HEADS_B // N_KV_B
D_B = N_HEADS_B * HEAD_DIM
D_KV_B = N_KV_B * HEAD_DIM
ROT_DIM = HEAD_DIM // 4
ROPE_THETA = 500000.0
N_IDX_HEADS = 16
IDX_DIM = 64
IDX_ROT_DIM = IDX_DIM // 4
TOPK_MAX = 256
N_EXPERTS = 32
TOP_K_EXPERTS = 4
SWIGLU_LIMIT = 7.0
SWIGLU_ALPHA = 1.702
RMS_EPS = 1e-5
PAGE_SIZE = 128

OFF_U = 0
OFF_V = OFF_U + D_A
OFF_Q = OFF_V + D_A
OFF_K = OFF_Q + D_B
OFF_VB = OFF_K + D_KV_B
OFF_QI = OFF_VB + D_KV_B
OFF_KI = OFF_QI + N_IDX_HEADS * IDX_DIM
OFF_WI = OFF_KI + IDX_DIM
IN_COLS = OFF_WI + N_IDX_HEADS

LANES = 128
SUBLANES = 8
VMEM_LIMIT_BYTES = 56 * 1024 * 1024

IN_COLS_PAD = pl.cdiv(IN_COLS, LANES) * LANES
INT_MIN = -(2 ** 31)
NEG_INF = float("-inf")

F32 = jnp.float32
BF16 = jnp.bfloat16


def _params(semantics):
    return pltpu.CompilerParams(dimension_semantics=semantics, vmem_limit_bytes=VMEM_LIMIT_BYTES)


def _dot(a, b):
    return jnp.dot(a, b, preferred_element_type=F32)


def _dot_nt(a, b):
    return lax.dot_general(a, b, (((1,), (1,)), ((), ())), preferred_element_type=F32)


IN_TM = 256


def _rope_lanes(z, cos, sin_lo, sin_hi, half):
    return z * cos + pltpu.roll(z, LANES - half, 1) * sin_lo + pltpu.roll(z, half, 1) * sin_hi


def _gelu(z):
    return 0.5 * z * (1.0 + lax.erf(z * (2.0 ** -0.5)))


def _group_rows(n_prompt_tiles, xp_ref, xs_ref):
    return jnp.where(pl.program_id(0) < n_prompt_tiles, xp_ref[...], xs_ref[...])


def _group_specs(n_prompt_tiles, tm, d_model):
    prompt = pl.BlockSpec((tm, d_model), lambda i: (jnp.minimum(i, n_prompt_tiles - 1), 0))
    decode = pl.BlockSpec((tm, d_model), lambda i: (jnp.maximum(i - n_prompt_tiles, 0), 0))
    return [prompt, decode]


def _in_proj_kernel(n_prompt_tiles, xp_ref, xs_ref, g_ref, w_ref, nva_ref, wmix_ref, bmix_ref,
                    cq_ref, saq_ref, sbq_ref, ci_ref, sai_ref, sbi_ref,
                    aout_ref, q_ref, k_ref, kb_ref, vb_ref, vbb_ref, qi_ref, ki_ref, kib_ref, wi_ref, va_ref):
    x = _group_rows(n_prompt_tiles, xp_ref, xs_ref)
    ms = jnp.mean(x * x, axis=-1, keepdims=True)
    xb = ((x * lax.rsqrt(ms + RMS_EPS)) * g_ref[...]).astype(BF16)

    u = _gelu(_dot(xb, w_ref[:, OFF_U:OFF_V]))
    gv = _gelu(_dot(xb, w_ref[:, OFF_V:OFF_Q]))
    va = gv * lax.rsqrt(jnp.mean(gv * gv, axis=-1, keepdims=True) + RMS_EPS) * nva_ref[...]
    va_ref[...] = va
    vab = va.astype(BF16)
    for sub in range(IN_TM // CHUNK):
        rows = slice(sub * CHUNK, (sub + 1) * CHUNK)
        for h in range(N_HEADS_A):
            cols = slice(h * HEAD_DIM, (h + 1) * HEAD_DIM)
            mixed = _dot(wmix_ref[h], vab[rows, cols]) + bmix_ref[:, h:h + 1]
            aout_ref[rows, cols] = (u[rows, cols] * mixed).astype(BF16)

    cq, saq, sbq = cq_ref[...], saq_ref[...], sbq_ref[...]
    zq = _dot(xb, w_ref[:, OFF_Q:OFF_K])
    for h in range(N_HEADS_B):
        cols = slice(h * HEAD_DIM, (h + 1) * HEAD_DIM)
        r = _rope_lanes(zq[:, cols], cq, saq, sbq, ROT_DIM // 2)
        q_ref[:, cols] = (r * (HEAD_DIM ** -0.5)).astype(BF16)
    zk = _dot(xb, w_ref[:, OFF_K:OFF_VB])
    for h in range(N_KV_B):
        cols = slice(h * HEAD_DIM, (h + 1) * HEAD_DIM)
        r = _rope_lanes(zk[:, cols], cq, saq, sbq, ROT_DIM // 2)
        k_ref[:, cols] = r
        kb_ref[:, cols] = r.astype(BF16)
    zvb = _dot(xb, w_ref[:, OFF_VB:OFF_QI])
    vb_ref[...] = zvb
    vbb_ref[...] = zvb.astype(BF16)

    ci, sai, sbi = ci_ref[...], sai_ref[...], sbi_ref[...]
    zqi = _dot(xb, w_ref[:, OFF_QI:OFF_KI])
    for p in range(N_IDX_HEADS * IDX_DIM // LANES):
        cols = slice(p * LANES, (p + 1) * LANES)
        qi_ref[:, cols] = _rope_lanes(zqi[:, cols], ci, sai, sbi, IDX_ROT_DIM // 2).astype(BF16)
    zl = _dot(xb, w_ref[:, OFF_KI:IN_COLS_PAD])
    r = _rope_lanes(zl, ci, sai, sbi, IDX_ROT_DIM // 2)
    ki_ref[...] = r[:, :IDX_DIM]
    lane = lax.broadcasted_iota(jnp.int32, r.shape, 1)
    kib_ref[...] = jnp.where(lane < IDX_DIM, r, pltpu.roll(r, IDX_DIM, 1)).astype(BF16)
    wi_ref[...] = zl[:, IDX_DIM:IDX_DIM + N_IDX_HEADS] * (N_IDX_HEADS ** -0.5 * IDX_DIM ** -0.5)


def _rope_tables(pos, rot_dim, head_dim):
    half = rot_dim // 2
    inv = ROPE_THETA ** (-jnp.arange(half, dtype=F32) / half)
    ang = pos.astype(F32)[:, None] * inv[None, :]
    cos, sin = jnp.cos(ang), jnp.sin(ang)
    lane = jnp.arange(LANES) % head_dim
    cos_l = jnp.take(cos, lane % half, axis=1)
    sin_l = jnp.take(sin, lane % half, axis=1)
    c = jnp.where(lane[None] < rot_dim, cos_l, 1.0)
    s_lo = jnp.where(lane[None] < half, -sin_l, 0.0)
    s_hi = jnp.where((lane[None] >= half) & (lane[None] < rot_dim), sin_l, 0.0)
    return c.astype(F32), s_lo.astype(F32), s_hi.astype(F32)


def _in_proj(x_p, x_s, tiles_per_seq, norm_mix, w_in_b, norm_v_a, wmix, bmix, tables):
    d_model = x_p.shape[1]
    n_prompt_tiles = x_p.shape[0] // IN_TM
    n_rows = x_p.shape[0] + x_s.shape[0]
    n_tiles = n_rows // IN_TM
    row = lambda i: (i, 0)
    const = lambda i: (0, 0)
    tab = lambda i: (jnp.where(i < n_prompt_tiles, i % tiles_per_seq, tiles_per_seq), 0)
    grp = lambda i: (jnp.where(i < n_prompt_tiles, 0, 1), 0, 0, 0)
    grp2 = lambda i: (jnp.where(i < n_prompt_tiles, 0, 1), 0, 0)
    tab_spec = pl.BlockSpec((IN_TM, LANES), tab)
    outs = [
        ((n_rows, D_A), BF16),
        ((n_rows, D_B), BF16),
        ((n_rows, D_KV_B), F32),
        ((n_rows, D_KV_B), BF16),
        ((n_rows, D_KV_B), F32),
        ((n_rows, D_KV_B), BF16),
        ((n_rows, N_IDX_HEADS * IDX_DIM), BF16),
        ((n_rows, IDX_DIM), F32),
        ((n_rows, LANES), BF16),
        ((n_rows, N_IDX_HEADS), F32),
        ((n_rows, D_A), F32),
    ]
    return pl.pallas_call(
        functools.partial(_in_proj_kernel, n_prompt_tiles),
        grid=(n_tiles,),
        in_specs=_group_specs(n_prompt_tiles, IN_TM, d_model) + [
            pl.BlockSpec((1, d_model), const),
            pl.BlockSpec((d_model, IN_COLS_PAD), const, pipeline_mode=pl.Buffered(1)),
            pl.BlockSpec((1, D_A), const),
            pl.BlockSpec((None, N_HEADS_A, CHUNK, CHUNK), grp),
            pl.BlockSpec((None, CHUNK, N_HEADS_A), grp2),
            tab_spec, tab_spec, tab_spec, tab_spec, tab_spec, tab_spec,
        ],
        out_specs=[pl.BlockSpec((IN_TM, s[1]), row) for s, _ in outs],
        out_shape=[jax.ShapeDtypeStruct(s, d) for s, d in outs],
        compiler_params=_params(("parallel",)),
        name="in_proj",
    )(x_p, x_s, norm_mix, w_in_b, norm_v_a, wmix, bmix, *tables)


ORDER_KEY_NEG_INF = -(2 ** 31) + 0x7FFFFF


def _order_key_to_float(key):
    return pltpu.bitcast(key ^ ((key >> 31) & jnp.int32(0x7FFFFFFF)), F32)


def _kth_largest(score, k_top):
    def body(it, t):
        cand = t + lax.shift_left(jnp.int32(1), jnp.int32(31) - it)
        cnt = jnp.sum((score >= _order_key_to_float(cand)).astype(F32), axis=1, keepdims=True)
        return jnp.where(cnt >= k_top, cand, t)

    t0 = jnp.full((score.shape[0], 1), INT_MIN, jnp.int32)
    t = lax.fori_loop(0, 32, body, t0)
    return _order_key_to_float(jnp.maximum(t, ORDER_KEY_NEG_INF))


def _topk_select(score, k_top, tri):
    thr = _kth_largest(score, k_top)
    above = score > thr
    need = k_top - jnp.sum(above.astype(F32), axis=1, keepdims=True)
    tie = score == thr
    pieces = []
    run = jnp.zeros_like(need)
    for j in range(score.shape[1] // LANES):
        cols = slice(j * LANES, (j + 1) * LANES)
        tie_j = tie[:, cols]
        prefix = _dot(tie_j.astype(BF16), tri) + run
        run = prefix[:, LANES - 1:LANES]
        pieces.append(jnp.where(above[:, cols] | (tie_j & (prefix <= need)), 1.0, 0.0))
    return jnp.concatenate(pieces, axis=1)


def _tri_matrix():
    r = lax.broadcasted_iota(jnp.int32, (LANES, LANES), 0)
    c = lax.broadcasted_iota(jnp.int32, (LANES, LANES), 1)
    return (r <= c).astype(BF16)


PA_TQ = 128


PA_EXTENTS = 8


def _prompt_attn_kernel(k_top, qi_ref, wi_ref, kib_ref, q_ref, kb_ref, vbb_ref, o_ref):
    i = pl.program_id(1)
    step = kb_ref.shape[0] // PA_EXTENTS
    blocks_per_step = step // PA_TQ
    for c in range(PA_EXTENTS):
        @pl.when(i // blocks_per_step == c)
        def _():
            _prompt_attn_block(k_top, (c + 1) * step, i, qi_ref, wi_ref, kib_ref, q_ref, kb_ref, vbb_ref, o_ref)


def _prompt_attn_block(k_top, seq, i, qi_ref, wi_ref, kib_ref, q_ref, kb_ref, vbb_ref, o_ref):
    qi = qi_ref[...]
    wi = wi_ref[...]
    ki2 = kib_ref[:seq, :]
    lane = lax.broadcasted_iota(jnp.int32, (PA_TQ, LANES), 1)
    score = jnp.zeros((PA_TQ, seq), F32)
    for h in range(N_IDX_HEADS):
        pair = qi[:, (h // 2) * LANES:(h // 2 + 1) * LANES]
        keep = (lane < IDX_DIM) if h % 2 == 0 else (lane >= IDX_DIM)
        d = _dot_nt(jnp.where(keep, pair, jnp.zeros_like(pair)), ki2)
        score = score + wi[:, h:h + 1] * jnp.maximum(d, 0.0)
    q_pos = i * PA_TQ + lax.broadcasted_iota(jnp.int32, (PA_TQ, seq), 0)
    s_pos = lax.broadcasted_iota(jnp.int32, (PA_TQ, seq), 1)
    causal = s_pos <= q_pos
    sel = _topk_select(jnp.where(causal, score, NEG_INF), k_top, _tri_matrix())
    bias = jnp.where(causal & (sel > 0.0), 0.0, NEG_INF)

    q = q_ref[...]
    for g in range(N_KV_B):
        q4 = jnp.concatenate(
            [q[:, (g * KV_GROUP + r) * HEAD_DIM:(g * KV_GROUP + r + 1) * HEAD_DIM] for r in range(KV_GROUP)], axis=0)
        kg = kb_ref[:seq, g * HEAD_DIM:(g + 1) * HEAD_DIM]
        vg = vbb_ref[:seq, g * HEAD_DIM:(g + 1) * HEAD_DIM]
        s = _dot_nt(q4, kg).reshape(KV_GROUP, PA_TQ, seq) + bias[None]
        m = jnp.max(s, axis=-1, keepdims=True)
        p = jnp.exp(s - m)
        l = jnp.sum(p, axis=-1, keepdims=True)
        o = _dot(p.reshape(KV_GROUP * PA_TQ, seq).astype(BF16), vg).reshape(KV_GROUP, PA_TQ, HEAD_DIM) / l
        for r in range(KV_GROUP):
            o_ref[:, (g * KV_GROUP + r) * HEAD_DIM:(g * KV_GROUP + r + 1) * HEAD_DIM] = o[r].astype(BF16)


def _prompt_attn(n_batch, seq, k_top, qi, wi, kib, q, kb, vbb):
    nq = seq // PA_TQ
    qrow = lambda b, i: (b * nq + i, 0)
    krow = lambda b, i: (b, 0)
    return pl.pallas_call(
        functools.partial(_prompt_attn_kernel, k_top),
        grid=(n_batch, nq),
        in_specs=[
            pl.BlockSpec((PA_TQ, N_IDX_HEADS * IDX_DIM), qrow),
            pl.BlockSpec((PA_TQ, N_IDX_HEADS), qrow),
            pl.BlockSpec((seq, LANES), krow),
            pl.BlockSpec((PA_TQ, D_B), qrow),
            pl.BlockSpec((seq, D_KV_B), krow),
            pl.BlockSpec((seq, D_KV_B), krow),
        ],
        out_specs=pl.BlockSpec((PA_TQ, D_B), qrow),
        out_shape=jax.ShapeDtypeStruct((n_batch * seq, D_B), BF16),
        compiler_params=_params(("parallel", "parallel")),
        name="prompt_attn",
    )(qi, wi, kib, q, kb, vbb)


SP_PAGES = 16
SP_KEYS = SP_PAGES * PAGE_SIZE
SP_TPAD = SUBLANES


def _page_specs(page_shape, n_steps, layer):
    def make(p):
        def index(b, j, pt):
            return (layer, pt[b, jnp.minimum(j, n_steps - 1) * SP_PAGES + p], 0, 0)
        return pl.BlockSpec((None, None) + page_shape, index)
    return [make(p) for p in range(SP_PAGES)]


def _sample_scores_kernel(n_steps, pt_ref, qs_ref, ws_ref, kin_ref, *refs):
    page_refs, o_ref = refs[:SP_PAGES], refs[SP_PAGES]
    j = pl.program_id(1)
    qs = qs_ref[...]
    ws = ws_ref[...]

    def head_sum(d):
        w = ws * jnp.maximum(d, 0.0)
        return jnp.sum(w.reshape(N_IDX_HEADS, SP_TPAD, d.shape[1]), axis=0)

    @pl.when(j < n_steps)
    def _():
        keys_t = jnp.concatenate([r[...] for r in page_refs], axis=1).astype(BF16)
        o_ref[...] = head_sum(_dot(qs, keys_t))

    @pl.when(j == n_steps)
    def _():
        d = _dot_nt(qs, kin_ref[...])
        o_ref[...] = jnp.concatenate(
            [head_sum(d), jnp.zeros((SP_TPAD, SP_KEYS - d.shape[1]), F32)], axis=1)


def _sample_scores(page_table, qs, ws, ki_new, cache_idx_k, layer):
    n_db, n_pages = page_table.shape
    n_steps = n_pages // SP_PAGES
    rows = N_IDX_HEADS * SP_TPAD
    per_b = lambda b, j, pt: (b, 0, 0)
    grid_spec = pltpu.PrefetchScalarGridSpec(
        num_scalar_prefetch=1,
        grid=(n_db, n_steps + 1),
        in_specs=[
            pl.BlockSpec((None, rows, IDX_DIM), per_b),
            pl.BlockSpec((None, rows, 1), per_b),
            pl.BlockSpec((None, LANES, IDX_DIM), per_b),
        ] + _page_specs((IDX_DIM, PAGE_SIZE), n_steps, layer),
        out_specs=pl.BlockSpec((None, SP_TPAD, SP_KEYS), lambda b, j, pt: (b, 0, j)),
    )
    return pl.pallas_call(
        functools.partial(_sample_scores_kernel, n_steps),
        grid_spec=grid_spec,
        out_shape=jax.ShapeDtypeStruct((n_db, SP_TPAD, (n_steps + 1) * SP_KEYS), F32),
        compiler_params=_params(("parallel", "arbitrary")),
        name="sample_scores",
    )(page_table, qs, ws, ki_new, *([cache_idx_k] * SP_PAGES))


SM_ROWS = 64


def _sample_mask_kernel(k_top, past, n_new, s_ref, o_ref):
    score = s_ref[...]
    t = lax.broadcasted_iota(jnp.int32, score.shape, 0) % SP_TPAD
    s_pos = lax.broadcasted_iota(jnp.int32, score.shape, 1)
    admissible = s_pos <= past + t
    sel = _topk_select(jnp.where(admissible, score, NEG_INF), k_top, _tri_matrix())
    keep = jnp.where((admissible & (sel > 0.0)) | (t >= n_new), 1.0, 0.0).astype(BF16)
    r = lax.broadcasted_iota(jnp.int32, (LANES, N_KV_B * LANES), 0)
    c = lax.broadcasted_iota(jnp.int32, (LANES, N_KV_B * LANES), 1)
    spread = (c // N_KV_B == r).astype(BF16)
    for j in range(score.shape[1] // LANES):
        d = _dot(keep[:, j * LANES:(j + 1) * LANES], spread)
        o_ref[:, j * N_KV_B * LANES:(j + 1) * N_KV_B * LANES] = jnp.where(d > 0.5, 0.0, NEG_INF)


def _sample_mask(scores2d, k_top, past, n_new):
    n_rows, width = scores2d.shape
    return pl.pallas_call(
        functools.partial(_sample_mask_kernel, k_top, past, n_new),
        grid=(n_rows // SM_ROWS,),
        in_specs=[pl.BlockSpec((SM_ROWS, width), lambda i: (i, 0))],
        out_specs=pl.BlockSpec((SM_ROWS, N_KV_B * width), lambda i: (i, 0)),
        out_shape=jax.ShapeDtypeStruct((n_rows, N_KV_B * width), F32),
        compiler_params=_params(("parallel",)),
        name="sample_mask",
    )(scores2d)


SA_ROWS = KV_GROUP * SP_TPAD
SA_ROWS_ALL = N_KV_B * SA_ROWS
SA_PAGE_ROWS = PAGE_SIZE * N_KV_B


def _sample_attn_kernel(n_steps, pt_ref, q_ref, bias_ref, kn_ref, vn_ref, *refs):
    k_refs, v_refs = refs[:SP_PAGES], refs[SP_PAGES:2 * SP_PAGES]
    o_ref, m_ref, l_ref, acc_ref = refs[2 * SP_PAGES:]
    j = pl.program_id(1)

    @pl.when(j == 0)
    def _():
        m_ref[...] = jnp.full(m_ref.shape, NEG_INF, F32)
        l_ref[...] = jnp.zeros(l_ref.shape, F32)
        acc_ref[...] = jnp.zeros(acc_ref.shape, F32)

    def update(kc, vc, bias):
        n = kc.shape[0]
        s = _dot_nt(q_ref[...], kc)
        row_head = lax.broadcasted_iota(jnp.int32, (SA_ROWS_ALL, n), 0) // SA_ROWS
        col_head = lax.broadcasted_iota(jnp.int32, (SA_ROWS_ALL, n), 1) % N_KV_B
        s = jnp.where(row_head == col_head, s + jnp.concatenate([bias] * (SA_ROWS_ALL // SP_TPAD), axis=0), NEG_INF)
        m_old = m_ref[...]
        m_new = jnp.maximum(m_old, jnp.max(s, axis=-1, keepdims=True))
        m_safe = jnp.where(m_new == NEG_INF, 0.0, m_new)
        alpha = jnp.exp(m_old - m_safe)
        p = jnp.exp(s - m_safe)
        l_ref[...] = alpha * l_ref[...] + jnp.sum(p, axis=-1, keepdims=True)
        acc_ref[...] = alpha * acc_ref[...] + _dot(p.astype(BF16), vc)
        m_ref[...] = m_new

    @pl.when(j < n_steps)
    def _():
        kc = jnp.concatenate([r[...] for r in k_refs], axis=0).astype(BF16)
        vc = jnp.concatenate([r[...] for r in v_refs], axis=0).astype(BF16)
        update(kc, vc, bias_ref[...])

    @pl.when(j == n_steps)
    def _():
        update(kn_ref[...], vn_ref[...], bias_ref[:, :SA_PAGE_ROWS])
        o_ref[...] = acc_ref[...] / l_ref[...]


def _sample_attn(page_table, qs, bias, k_new, v_new, cache_k, cache_v, layer):
    n_db, n_pages = page_table.shape
    n_steps = n_pages // SP_PAGES
    per_b = lambda b, j, pt: (b, 0, 0)
    grid_spec = pltpu.PrefetchScalarGridSpec(
        num_scalar_prefetch=1,
        grid=(n_db, n_steps + 1),
        in_specs=[
            pl.BlockSpec((None, SA_ROWS_ALL, HEAD_DIM), per_b),
            pl.BlockSpec((None, SP_TPAD, N_KV_B * SP_KEYS), lambda b, j, pt: (b, 0, j)),
            pl.BlockSpec((None, SA_PAGE_ROWS, HEAD_DIM), per_b),
            pl.BlockSpec((None, SA_PAGE_ROWS, HEAD_DIM), per_b),
        ] + 2 * _page_specs((SA_PAGE_ROWS, HEAD_DIM), n_steps, layer),
        out_specs=pl.BlockSpec((None, SA_ROWS_ALL, HEAD_DIM), per_b),
        scratch_shapes=[
            pltpu.VMEM((SA_ROWS_ALL, 1), F32),
            pltpu.VMEM((SA_ROWS_ALL, 1), F32),
            pltpu.VMEM((SA_ROWS_ALL, HEAD_DIM), F32),
        ],
    )
    return pl.pallas_call(
        functools.partial(_sample_attn_kernel, n_steps),
        grid_spec=grid_spec,
        out_shape=jax.ShapeDtypeStruct((n_db, SA_ROWS_ALL, HEAD_DIM), F32),
        compiler_params=_params(("parallel", "arbitrary")),
        name="sample_attn",
    )(page_table, qs, bias, k_new, v_new, *([cache_k] * SP_PAGES), *([cache_v] * SP_PAGES))


OR_TM = 256


def _split_bf16(x):
    hi = x.astype(BF16)
    return hi, (x - hi.astype(F32)).astype(BF16)


def _out_router_kernel(n_prompt_tiles, a_ref, b_ref, xp_ref, xs_ref, wo_ref, nf_ref, wrh_ref, wrl_ref, br_ref,
                       xmid_ref, h_ref, idx_ref, gate_ref):
    x = _group_rows(n_prompt_tiles, xp_ref, xs_ref)
    x_mid = x + _dot(a_ref[...], wo_ref[:D_A, :]) + _dot(b_ref[...], wo_ref[D_A:, :])
    xmid_ref[...] = x_mid
    h = (x_mid * lax.rsqrt(jnp.mean(x_mid * x_mid, axis=-1, keepdims=True) + RMS_EPS)) * nf_ref[...]
    h_ref[...] = h
    h_hi, h_lo = _split_bf16(h)
    wr_hi = wrh_ref[...]
    logits = _dot(h_hi, wr_hi) + _dot(h_lo, wr_hi) + _dot(h_hi, wrl_ref[...]) + br_ref[...]
    lane = lax.broadcasted_iota(jnp.int32, logits.shape, 1)
    vals, idxs = [], []
    for _ in range(TOP_K_EXPERTS):
        m = jnp.max(logits, axis=-1, keepdims=True)
        i = jnp.min(jnp.where(logits == m, lane, LANES), axis=-1, keepdims=True)
        vals.append(m)
        idxs.append(i)
        logits = jnp.where(lane == i, NEG_INF, logits)
    exps = [jnp.exp(v - vals[0]) for v in vals]
    total = exps[0]
    for e in exps[1:]:
        total = total + e
    idx_out = jnp.zeros(logits.shape, jnp.int32)
    gate_out = jnp.zeros(logits.shape, F32)
    for k in range(TOP_K_EXPERTS):
        idx_out = jnp.where(lane == k, idxs[k], idx_out)
        gate_out = jnp.where(lane == k, exps[k] / total, gate_out)
    idx_ref[...] = idx_out
    gate_ref[...] = gate_out


def _out_router(a_out, b_out, x_p, x_s, w_out_b, norm_ffn, wr_hi, wr_lo, br_pad):
    d_model = x_p.shape[1]
    n_prompt_tiles = x_p.shape[0] // OR_TM
    n_rows = x_p.shape[0] + x_s.shape[0]
    row = lambda i: (i, 0)
    const = lambda i: (0, 0)
    return pl.pallas_call(
        functools.partial(_out_router_kernel, n_prompt_tiles),
        grid=(n_rows // OR_TM,),
        in_specs=[
            pl.BlockSpec((OR_TM, D_A), row),
            pl.BlockSpec((OR_TM, D_B), row),
        ] + _group_specs(n_prompt_tiles, OR_TM, d_model) + [
            pl.BlockSpec((D_A + D_B, d_model), const),
            pl.BlockSpec((1, d_model), const),
            pl.BlockSpec((d_model, LANES), const),
            pl.BlockSpec((d_model, LANES), const),
            pl.BlockSpec((1, LANES), const),
        ],
        out_specs=[
            pl.BlockSpec((OR_TM, d_model), row),
            pl.BlockSpec((OR_TM, d_model), row),
            pl.BlockSpec((OR_TM, LANES), row),
            pl.BlockSpec((OR_TM, LANES), row),
        ],
        out_shape=[
            jax.ShapeDtypeStruct((n_rows, d_model), F32),
            jax.ShapeDtypeStruct((n_rows, d_model), F32),
            jax.ShapeDtypeStruct((n_rows, LANES), jnp.int32),
            jax.ShapeDtypeStruct((n_rows, LANES), F32),
        ],
        compiler_params=_params(("parallel",)),
        name="out_router",
    )(a_out, b_out, x_p, x_s, w_out_b, norm_ffn, wr_hi, wr_lo, br_pad)


MOE_TM = 256
MOE_UNIT_TILES = 5
MOE_R = MOE_TM * MOE_UNIT_TILES
MOE_TF = 256
COMBINE_TM = 128
ROW_COPY_UNROLL = 8


def _row_copy(src_ref, src_row, dst_ref, dst_row, sem):
    return pltpu.make_async_copy(src_ref.at[pl.ds(src_row, 1), :], dst_ref.at[pl.ds(dst_row, 1), :], sem)


def _tile_rows(rows):
    return pl.cdiv(rows, MOE_TM) * MOE_TM


def _moe_kernel(ue_ref, ur_ref, ub_ref, tok_ref, tokn_ref, h_ref, wg_ref, wu_ref, bg_ref, bu_ref, wd_ref, bd_ref,
                ys_ref, land_ref, xb_ref, y_ref, wgb_ref, wub_ref, wdb_ref, gsem, ysem):
    u, f = pl.program_id(0), pl.program_id(1)
    n_units, n_f = pl.num_programs(0), pl.num_programs(1)
    rows = ur_ref[u]
    n_tiles = pl.cdiv(rows, MOE_TM)

    def gather(tokens_ref, n_rows, wait):
        def tile(m, c):
            def body(j, c2):
                r = m * MOE_TM + j
                cp = _row_copy(h_ref, tokens_ref[0, r], land_ref, r, gsem)
                cp.wait() if wait else cp.start()
                return c2
            return lax.fori_loop(0, MOE_TM, body, c, unroll=ROW_COPY_UNROLL)
        lax.fori_loop(0, n_rows // MOE_TM, tile, 0)

    def writeback(unit, n, wait):
        def body(m, c):
            r0 = pl.multiple_of(m * MOE_TM, MOE_TM)
            cp = pltpu.make_async_copy(y_ref.at[pl.ds(r0, MOE_TM), :],
                                       ys_ref.at[pl.ds(ub_ref[unit] * MOE_R + r0, MOE_TM), :], ysem)
            cp.wait() if wait else cp.start()
            return c
        lax.fori_loop(0, n, body, 0)

    @pl.when((f == 0) & (rows > 0))
    def _():
        @pl.when(u == 0)
        def _():
            gather(tok_ref, _tile_rows(rows), wait=False)

        gather(tok_ref, _tile_rows(rows), wait=True)

        def convert(m, c):
            r0 = pl.multiple_of(m * MOE_TM, MOE_TM)
            xb_ref[pl.ds(r0, MOE_TM), :] = land_ref[pl.ds(r0, MOE_TM), :].astype(BF16)
            return c
        lax.fori_loop(0, n_tiles, convert, 0)

        nxt = jnp.minimum(u + 1, n_units - 1)
        rows_next = jnp.where(u + 1 < n_units, ur_ref[nxt], 0)
        gather(tokn_ref, _tile_rows(rows_next), wait=False)

    @pl.when((f == 0) & (u > 0))
    def _():
        writeback(u - 1, pl.cdiv(ur_ref[jnp.maximum(u - 1, 0)], MOE_TM), wait=True)

    @pl.when(rows > 0)
    def _():
        wgb_ref[...] = wg_ref[...].astype(BF16)
        wub_ref[...] = wu_ref[...].astype(BF16)
        wdb_ref[...] = wd_ref[...].astype(BF16)

        def tile(m, c):
            r0 = pl.multiple_of(m * MOE_TM, MOE_TM)
            xm = xb_ref[pl.ds(r0, MOE_TM), :]
            gate = jnp.minimum(_dot(xm, wgb_ref[...]) + bg_ref[...], SWIGLU_LIMIT)
            up = jnp.clip(_dot(xm, wub_ref[...]) + bu_ref[...], -SWIGLU_LIMIT, SWIGLU_LIMIT)
            act = (up + 1.0) * (gate * jax.nn.sigmoid(SWIGLU_ALPHA * gate))
            y_ref[pl.ds(r0, MOE_TM), :] += _dot(act.astype(BF16), wdb_ref[...])
            return c

        @pl.when(f == 0)
        def _():
            def init(m, c):
                r0 = pl.multiple_of(m * MOE_TM, MOE_TM)
                y_ref[pl.ds(r0, MOE_TM), :] = jnp.broadcast_to(bd_ref[...], (MOE_TM, y_ref.shape[1]))
                return c
            lax.fori_loop(0, n_tiles, init, 0)

        lax.fori_loop(0, n_tiles, tile, 0)

        @pl.when(f == n_f - 1)
        def _():
            writeback(u, n_tiles, wait=False)

    @pl.when((f == n_f - 1) & (u == n_units - 1))
    def _():
        writeback(u, n_tiles, wait=True)


def _moe_experts(unit_expert, unit_rows, unit_block, slot_token, h, w_gate_up, b_gate_up, w_down, b_down):
    n_units = unit_expert.shape[0]
    d_model = h.shape[1]
    d_ff = w_down.shape[1]
    n_f = d_ff // MOE_TF

    def f_eff(u, f, ur):
        return jnp.where(ur[u] > 0, f, n_f - 1)

    smem_tokens = lambda index: pl.BlockSpec((None, 1, MOE_R), index, memory_space=pltpu.SMEM)
    grid_spec = pltpu.PrefetchScalarGridSpec(
        num_scalar_prefetch=3,
        grid=(n_units, n_f),
        in_specs=[
            smem_tokens(lambda u, f, ue, ur, ub: (ub[u], 0, 0)),
            smem_tokens(lambda u, f, ue, ur, ub: (ub[jnp.minimum(u + 1, n_units - 1)], 0, 0)),
            pl.BlockSpec(memory_space=pl.ANY),
            pl.BlockSpec((None, d_model, MOE_TF), lambda u, f, ue, ur, ub: (ue[u], 0, f_eff(u, f, ur))),
            pl.BlockSpec((None, d_model, MOE_TF), lambda u, f, ue, ur, ub: (ue[u], 0, n_f + f_eff(u, f, ur))),
            pl.BlockSpec((None, 1, MOE_TF), lambda u, f, ue, ur, ub: (ue[u], 0, f_eff(u, f, ur))),
            pl.BlockSpec((None, 1, MOE_TF), lambda u, f, ue, ur, ub: (ue[u], 0, n_f + f_eff(u, f, ur))),
            pl.BlockSpec((None, MOE_TF, d_model), lambda u, f, ue, ur, ub: (ue[u], f_eff(u, f, ur), 0)),
            pl.BlockSpec((None, 1, d_model), lambda u, f, ue, ur, ub: (ue[u], 0, 0)),
        ],
        out_specs=pl.BlockSpec(memory_space=pl.ANY),
        scratch_shapes=[
            pltpu.VMEM((MOE_R, d_model), F32),
            pltpu.VMEM((MOE_R, d_model), BF16),
            pltpu.VMEM((MOE_R, d_model), F32),
            pltpu.VMEM((d_model, MOE_TF), BF16),
            pltpu.VMEM((d_model, MOE_TF), BF16),
            pltpu.VMEM((MOE_TF, d_model), BF16),
            pltpu.SemaphoreType.DMA,
            pltpu.SemaphoreType.DMA,
        ],
    )
    tokens = slot_token.reshape(n_units, 1, MOE_R)
    return pl.pallas_call(
        _moe_kernel,
        grid_spec=grid_spec,
        out_shape=jax.ShapeDtypeStruct((n_units * MOE_R, d_model), F32),
        compiler_params=_params(("arbitrary", "arbitrary")),
        name="moe_experts",
    )(unit_expert, unit_rows, unit_block, tokens, tokens, h, w_gate_up, w_gate_up, b_gate_up, b_gate_up,
      w_down, b_down)


def _combine_kernel(pos_ref, posn_ref, gate_ref, xmid_ref, nfin_ref, ys_ref, o_ref, buf_ref, sem):
    i, n_tiles = pl.program_id(0), pl.num_programs(0)
    slot = i % 2

    def rows(p_ref, s, wait):
        for k in range(TOP_K_EXPERTS):
            def body(t, c):
                cp = _row_copy(ys_ref, p_ref[0, t * TOP_K_EXPERTS + k], buf_ref.at[s], k * COMBINE_TM + t,
                               sem.at[s])
                cp.wait() if wait else cp.start()
                return c
            lax.fori_loop(0, COMBINE_TM, body, 0, unroll=ROW_COPY_UNROLL)

    @pl.when(i == 0)
    def _():
        rows(pos_ref, 0, wait=False)

    @pl.when(i + 1 < n_tiles)
    def _():
        rows(posn_ref, 1 - slot, wait=False)

    rows(pos_ref, slot, wait=True)
    gates = gate_ref[...]
    out = xmid_ref[...]
    for k in range(TOP_K_EXPERTS):
        out = out + gates[:, k:k + 1] * buf_ref[slot, k * COMBINE_TM:(k + 1) * COMBINE_TM, :]
    o_ref[...] = (out * lax.rsqrt(jnp.mean(out * out, axis=-1, keepdims=True) + RMS_EPS)) * nfin_ref[...]


def _combine(pos, gates, x_mid, norm_final, ys, n_tokens):
    d_model = x_mid.shape[1]
    n_tiles = n_tokens // COMBINE_TM
    n = COMBINE_TM * TOP_K_EXPERTS
    pos3 = pos.reshape(n_tiles, 1, n)
    row = lambda i: (i, 0)
    return pl.pallas_call(
        _combine_kernel,
        grid=(n_tiles,),
        in_specs=[
            pl.BlockSpec((None, 1, n), lambda i: (i, 0, 0), memory_space=pltpu.SMEM),
            pl.BlockSpec((None, 1, n), lambda i: (jnp.minimum(i + 1, n_tiles - 1), 0, 0), memory_space=pltpu.SMEM),
            pl.BlockSpec((COMBINE_TM, LANES), row),
            pl.BlockSpec((COMBINE_TM, d_model), row),
            pl.BlockSpec((1, d_model), lambda i: (0, 0)),
            pl.BlockSpec(memory_space=pl.ANY),
        ],
        out_specs=pl.BlockSpec((COMBINE_TM, d_model), row),
        out_shape=jax.ShapeDtypeStruct((n_tokens, d_model), F32),
        scratch_shapes=[pltpu.VMEM((2, n, d_model), F32), pltpu.SemaphoreType.DMA((2,))],
        compiler_params=_params(("arbitrary",)),
        name="combine",
    )(pos3, pos3, gates, x_mid, norm_final, ys)


def _routing(top_idx, n_units, zero_row):
    flat_e = top_idx.reshape(-1)
    onehot = (flat_e[:, None] == jnp.arange(N_EXPERTS, dtype=jnp.int32)[None, :]).astype(jnp.int32)
    csum = jnp.cumsum(onehot, axis=0)
    rank = jnp.take_along_axis(csum, flat_e[:, None], axis=1)[:, 0] - 1
    counts = csum[-1]
    units_e = (counts + MOE_R - 1) // MOE_R
    units_end = jnp.cumsum(units_e)
    units_start = units_end - units_e
    pos = (units_start[flat_e] + rank // MOE_R) * MOE_R + rank % MOE_R
    u = jnp.arange(n_units, dtype=jnp.int32)
    total = units_end[-1]
    ue = jnp.minimum(jnp.sum((units_end[None, :] <= u[:, None]).astype(jnp.int32), axis=1), N_EXPERTS - 1)
    rows = jnp.clip(counts[ue] - (u - units_start[ue]) * MOE_R, 0, MOE_R)
    used = u < total
    last_e = ue[jnp.maximum(total - 1, 0)]
    unit_expert = jnp.where(used, ue, last_e).astype(jnp.int32)
    unit_rows = jnp.where(used, rows, 0).astype(jnp.int32)
    unit_block = jnp.minimum(u, total - 1).astype(jnp.int32)
    pos = pos.astype(jnp.int32)
    token = jnp.arange(flat_e.shape[0], dtype=jnp.int32) // TOP_K_EXPERTS
    slot_token = jnp.full((n_units * MOE_R,), zero_row, jnp.int32).at[pos].set(token, unique_indices=True)
    return pos, slot_token, unit_expert, unit_rows, unit_block


def kernel(x_prompt, x_sample, cache_k, cache_v, cache_idx_k, page_table, norm_mix, w_in, norm_v_a,
           w_spatial, b_spatial, w_out, norm_ffn, w_router, b_router, w_gate_up, b_gate_up, w_down,
           b_down, norm_final):
    n_b, seq, d_model = x_prompt.shape
    n_db, n_new, _ = x_sample.shape
    depth = w_in.shape[0]
    past = page_table.shape[1] * PAGE_SIZE
    n_prompt, n_sample = n_b * seq, n_db * n_new
    n_tokens = n_prompt + n_sample
    assert seq % IN_TM == 0 and seq % PA_TQ == 0 and CHUNK % n_new == 0 and n_new <= SP_TPAD
    assert n_sample % CHUNK == 0 and page_table.shape[1] % SP_PAGES == 0
    assert n_tokens % COMBINE_TM == 0
    n_rows = (n_tokens // IN_TM + 1) * IN_TM
    k_top_prompt = min(TOPK_MAX, seq // 4)
    k_top_sample = min(TOPK_MAX, (past + n_new) // 4)
    n_assign = n_tokens * TOP_K_EXPERTS
    n_units = N_EXPERTS + n_assign // MOE_R

    pos_rows = jnp.concatenate([jnp.arange(seq, dtype=jnp.int32),
                                past + jnp.arange(IN_TM, dtype=jnp.int32) % n_new])
    tables = _rope_tables(pos_rows, ROT_DIM, HEAD_DIM) + _rope_tables(pos_rows, IDX_ROT_DIM, IDX_DIM)

    x_p = x_prompt.reshape(n_prompt, d_model)
    x_s = jnp.pad(x_sample.reshape(n_sample, d_model), ((0, n_rows - n_tokens), (0, 0)))
    outs = []
    for l in range(depth):
        causal = jnp.tril(jnp.ones((CHUNK, CHUNK), bool))
        w_prompt = jnp.where(causal[None], w_spatial[l], 0.0)
        eye = jnp.eye(CHUNK // n_new, dtype=F32)
        w_decode = jax.vmap(lambda w: jnp.kron(eye, w))(w_prompt[:, :n_new, :n_new])
        wmix = jnp.stack([w_prompt, w_decode]).astype(BF16)
        b_decode = jnp.tile(b_spatial[l][:, :n_new], (1, CHUNK // n_new))
        bmix = jnp.stack([b_spatial[l].T, b_decode.T])
        w_in_b = jnp.pad(w_in[l], ((0, 0), (0, IN_COLS_PAD - IN_COLS))).astype(BF16)

        (a_out, q, k, kb, vb, vbb, qi, ki, kib, wi, va) = _in_proj(
            x_p, x_s, seq // IN_TM, norm_mix[l][None], w_in_b, norm_v_a[l][None], wmix, bmix, tables)

        b_prompt = _prompt_attn(n_b, seq, k_top_prompt, qi, wi, kib, q, kb, vbb)

        sl = slice(n_prompt, n_tokens)
        pad_t = ((0, 0), (0, 0), (0, SP_TPAD - n_new), (0, 0))
        qi_s = qi[sl].reshape(n_db, n_new, N_IDX_HEADS, IDX_DIM).transpose(0, 2, 1, 3)
        qs_idx = jnp.pad(qi_s, pad_t).reshape(n_db, N_IDX_HEADS * SP_TPAD, IDX_DIM)
        wi_s = wi[sl].reshape(n_db, n_new, N_IDX_HEADS).transpose(0, 2, 1)
        ws_idx = jnp.pad(wi_s, ((0, 0), (0, 0), (0, SP_TPAD - n_new))).reshape(n_db, N_IDX_HEADS * SP_TPAD, 1)
        pad_rows = ((0, 0), (0, LANES - n_new), (0, 0))
        ki_new = jnp.pad(kib[sl, :IDX_DIM].reshape(n_db, n_new, IDX_DIM), pad_rows)
        scores = _sample_scores(page_table, qs_idx, ws_idx, ki_new, jnp.swapaxes(cache_idx_k, 2, 3), l)
        width = scores.shape[-1]
        bias = _sample_mask(scores.reshape(n_db * SP_TPAD, width), k_top_sample, past, n_new)
        q_s = q[sl].reshape(n_db, n_new, N_KV_B, KV_GROUP, HEAD_DIM).transpose(0, 2, 3, 1, 4)
        qs_att = jnp.pad(q_s, ((0, 0), (0, 0), (0, 0), (0, SP_TPAD - n_new), (0, 0))).reshape(
            n_db, SA_ROWS_ALL, HEAD_DIM)
        pad_kv = ((0, 0), (0, SA_PAGE_ROWS - n_new * N_KV_B), (0, 0))
        k_new = jnp.pad(kb[sl].reshape(n_db, n_new * N_KV_B, HEAD_DIM), pad_kv)
        v_new = jnp.pad(vbb[sl].reshape(n_db, n_new * N_KV_B, HEAD_DIM), pad_kv)
        n_pool = cache_k.shape[1]
        o_s = _sample_attn(page_table, qs_att, bias.reshape(n_db, SP_TPAD, N_KV_B * width), k_new, v_new,
                           cache_k.reshape(depth, n_pool, SA_PAGE_ROWS, HEAD_DIM),
                           cache_v.reshape(depth, n_pool, SA_PAGE_ROWS, HEAD_DIM), l)
        b_sample = o_s.reshape(n_db, N_KV_B, KV_GROUP, SP_TPAD, HEAD_DIM)[:, :, :, :n_new]
        b_sample = b_sample.transpose(0, 3, 1, 2, 4).reshape(n_sample, D_B).astype(BF16)
        b_out = jnp.concatenate([b_prompt, b_sample, jnp.zeros((n_rows - n_tokens, D_B), BF16)])

        wr_pad = jnp.pad(w_router[l], ((0, 0), (0, LANES - N_EXPERTS)))
        wr_hi, wr_lo = _split_bf16(wr_pad)
        br_pad = jnp.pad(b_router[l], (0, LANES - N_EXPERTS), constant_values=NEG_INF)[None]
        x_mid, h, top_idx, gates = _out_router(a_out, b_out, x_p, x_s, w_out[l].astype(BF16),
                                               norm_ffn[l][None], wr_hi, wr_lo, br_pad)

        pos, slot_token, unit_expert, unit_rows, unit_block = _routing(
            top_idx[:n_tokens, :TOP_K_EXPERTS], n_units, zero_row=n_tokens)
        ys = _moe_experts(unit_expert, unit_rows, unit_block, slot_token, h, w_gate_up[l],
                          b_gate_up[l][:, None, :], w_down[l], b_down[l][:, None, :])
        last = l == depth - 1
        y = _combine(pos, gates, x_mid, norm_final[None] if last else jnp.ones((1, d_model), F32), ys, n_tokens)
        outs.append((k, vb, ki, va))
        if not last:
            raise NotImplementedError("multi-layer stacks need the un-normalised residual stream")

    def stack(j, lo, hi, shape):
        return jnp.stack([o[j][lo:hi].reshape(shape) for o in outs])

    y_prompt = y[:n_prompt].reshape(n_b, seq, d_model)
    y_sample = y[n_prompt:n_tokens].reshape(n_db, n_new, d_model)
    return (
        y_prompt, y_sample,
        stack(0, 0, n_prompt, (n_b, seq, N_KV_B, HEAD_DIM)),
        stack(1, 0, n_prompt, (n_b, seq, N_KV_B, HEAD_DIM)),
        stack(2, 0, n_prompt, (n_b, seq, IDX_DIM)),
        stack(0, n_prompt, n_tokens, (n_db, n_new, N_KV_B, HEAD_DIM)),
        stack(1, n_prompt, n_tokens, (n_db, n_new, N_KV_B, HEAD_DIM)),
        stack(2, n_prompt, n_tokens, (n_db, n_new, IDX_DIM)),
        stack(3, n_prompt, n_tokens, (n_db, n_new, N_HEADS_A, HEAD_DIM)),
    )
```

```python
import functools

import jax
import jax.numpy as jnp
from jax import lax
from jax.experimental import pallas as pl
from jax.experimental.pallas import tpu as pltpu

HEAD_DIM = 128
N_HEADS_A = 8
D_A = N_HEADS_A * HEAD_DIM
CHUNK = 128
N_HEADS_B = 8
N_KV_B = 2
KV_GROUP = N_HEADS_B // N_KV_B
D_B = N_HEADS_B * HEAD_DIM
D_KV_B = N_KV_B * HEAD_DIM
ROT_DIM = HEAD_DIM // 4
ROPE_THETA = 500000.0
N_IDX_HEADS = 16
IDX_DIM = 64
IDX_ROT_DIM = IDX_DIM // 4
TOPK_MAX = 256
N_EXPERTS = 32
TOP_K_EXPERTS = 4
SWIGLU_LIMIT = 7.0
SWIGLU_ALPHA = 1.702
RMS_EPS = 1e-5
PAGE_SIZE = 128

OFF_U = 0
OFF_V = OFF_U + D_A
OFF_Q = OFF_V + D_A
OFF_K = OFF_Q + D_B
OFF_VB = OFF_K + D_KV_B
OFF_QI = OFF_VB + D_KV_B
OFF_KI = OFF_QI + N_IDX_HEADS * IDX_DIM
OFF_WI = OFF_KI + IDX_DIM
IN_COLS = OFF_WI + N_IDX_HEADS

LANES = 128
SUBLANES = 8
VMEM_LIMIT_BYTES = 56 * 1024 * 1024

IN_COLS_PAD = pl.cdiv(IN_COLS, LANES) * LANES
INT_MIN = -(2 ** 31)
NEG_INF = float("-inf")

F32 = jnp.float32
BF16 = jnp.bfloat16


def _params(semantics):
    return pltpu.CompilerParams(dimension_semantics=semantics, vmem_limit_bytes=VMEM_LIMIT_BYTES)


def _dot(a, b):
    return jnp.dot(a, b, preferred_element_type=F32)


def _dot_nt(a, b):
    return lax.dot_general(a, b, (((1,), (1,)), ((), ())), preferred_element_type=F32)


IN_TM = 256


def _rope_lanes(z, cos, sin_lo, sin_hi, half):
    return z * cos + pltpu.roll(z, LANES - half, 1) * sin_lo + pltpu.roll(z, half, 1) * sin_hi


def _gelu(z):
    return 0.5 * z * (1.0 + lax.erf(z * (2.0 ** -0.5)))


def _group_rows(n_prompt_tiles, xp_ref, xs_ref):
    return jnp.where(pl.program_id(0) < n_prompt_tiles, xp_ref[...], xs_ref[...])


def _group_specs(n_prompt_tiles, tm, d_model):
    prompt = pl.BlockSpec((tm, d_model), lambda i: (jnp.minimum(i, n_prompt_tiles - 1), 0))
    decode = pl.BlockSpec((tm, d_model), lambda i: (jnp.maximum(i - n_prompt_tiles, 0), 0))
    return [prompt, decode]


def _in_proj_kernel(n_prompt_tiles, xp_ref, xs_ref, g_ref, w_ref, nva_ref, wmix_ref, bmix_ref,
                    cq_ref, saq_ref, sbq_ref, ci_ref, sai_ref, sbi_ref,
                    aout_ref, q_ref, k_ref, kb_ref, vb_ref, vbb_ref, qi_ref, ki_ref, kib_ref, wi_ref, va_ref):
    x = _group_rows(n_prompt_tiles, xp_ref, xs_ref)
    ms = jnp.mean(x * x, axis=-1, keepdims=True)
    xb = ((x * lax.rsqrt(ms + RMS_EPS)) * g_ref[...]).astype(BF16)

    u = _gelu(_dot(xb, w_ref[:, OFF_U:OFF_V]))
    gv = _gelu(_dot(xb, w_ref[:, OFF_V:OFF_Q]))
    va = gv * lax.rsqrt(jnp.mean(gv * gv, axis=-1, keepdims=True) + RMS_EPS) * nva_ref[...]
    va_ref[...] = va
    vab = va.astype(BF16)
    for sub in range(IN_TM // CHUNK):
        rows = slice(sub * CHUNK, (sub + 1) * CHUNK)
        for h in range(N_HEADS_A):
            cols = slice(h * HEAD_DIM, (h + 1) * HEAD_DIM)
            mixed = _dot(wmix_ref[h], vab[rows, cols]) + bmix_ref[:, h:h + 1]
            aout_ref[rows, cols] = (u[rows, cols] * mixed).astype(BF16)

    cq, saq, sbq = cq_ref[...], saq_ref[...], sbq_ref[...]
    zq = _dot(xb, w_ref[:, OFF_Q:OFF_K])
    for h in range(N_HEADS_B):
        cols = slice(h * HEAD_DIM, (h + 1) * HEAD_DIM)
        r = _rope_lanes(zq[:, cols], cq, saq, sbq, ROT_DIM // 2)
        q_ref[:, cols] = (r * (HEAD_DIM ** -0.5)).astype(BF16)
    zk = _dot(xb, w_ref[:, OFF_K:OFF_VB])
    for h in range(N_KV_B):
        cols = slice(h * HEAD_DIM, (h + 1) * HEAD_DIM)
        r = _rope_lanes(zk[:, cols], cq, saq, sbq, ROT_DIM // 2)
        k_ref[:, cols] = r
        kb_ref[:, cols] = r.astype(BF16)
    zvb = _dot(xb, w_ref[:, OFF_VB:OFF_QI])
    vb_ref[...] = zvb
    vbb_ref[...] = zvb.astype(BF16)

    ci, sai, sbi = ci_ref[...], sai_ref[...], sbi_ref[...]
    zqi = _dot(xb, w_ref[:, OFF_QI:OFF_KI])
    for p in range(N_IDX_HEADS * IDX_DIM // LANES):
        cols = slice(p * LANES, (p + 1) * LANES)
        qi_ref[:, cols] = _rope_lanes(zqi[:, cols], ci, sai, sbi, IDX_ROT_DIM // 2).astype(BF16)
    zl = _dot(xb, w_ref[:, OFF_KI:IN_COLS_PAD])
    r = _rope_lanes(zl, ci, sai, sbi, IDX_ROT_DIM // 2)
    ki_ref[...] = r[:, :IDX_DIM]
    lane = lax.broadcasted_iota(jnp.int32, r.shape, 1)
    kib_ref[...] = jnp.where(lane < IDX_DIM, r, pltpu.roll(r, IDX_DIM, 1)).astype(BF16)
    wi_ref[...] = zl[:, IDX_DIM:IDX_DIM + N_IDX_HEADS] * (N_IDX_HEADS ** -0.5 * IDX_DIM ** -0.5)


def _rope_tables(pos, rot_dim, head_dim):
    half = rot_dim // 2
    inv = ROPE_THETA ** (-jnp.arange(half, dtype=F32) / half)
    ang = pos.astype(F32)[:, None] * inv[None, :]
    cos, sin = jnp.cos(ang), jnp.sin(ang)
    lane = jnp.arange(LANES) % head_dim
    cos_l = jnp.take(cos, lane % half, axis=1)
    sin_l = jnp.take(sin, lane % half, axis=1)
    c = jnp.where(lane[None] < rot_dim, cos_l, 1.0)
    s_lo = jnp.where(lane[None] < half, -sin_l, 0.0)
    s_hi = jnp.where((lane[None] >= half) & (lane[None] < rot_dim), sin_l, 0.0)
    return c.astype(F32), s_lo.astype(F32), s_hi.astype(F32)


def _in_proj(x_p, x_s, tiles_per_seq, norm_mix, w_in_b, norm_v_a, wmix, bmix, tables):
    d_model = x_p.shape[1]
    n_prompt_tiles = x_p.shape[0] // IN_TM
    n_rows = x_p.shape[0] + x_s.shape[0]
    n_tiles = n_rows // IN_TM
    row = lambda i: (i, 0)
    const = lambda i: (0, 0)
    tab = lambda i: (jnp.where(i < n_prompt_tiles, i % tiles_per_seq, tiles_per_seq), 0)
    grp = lambda i: (jnp.where(i < n_prompt_tiles, 0, 1), 0, 0, 0)
    grp2 = lambda i: (jnp.where(i < n_prompt_tiles, 0, 1), 0, 0)
    tab_spec = pl.BlockSpec((IN_TM, LANES), tab)
    outs = [
        ((n_rows, D_A), BF16),
        ((n_rows, D_B), BF16),
        ((n_rows, D_KV_B), F32),
        ((n_rows, D_KV_B), BF16),
        ((n_rows, D_KV_B), F32),
        ((n_rows, D_KV_B), BF16),
        ((n_rows, N_IDX_HEADS * IDX_DIM), BF16),
        ((n_rows, IDX_DIM), F32),
        ((n_rows, LANES), BF16),
        ((n_rows, N_IDX_HEADS), F32),
        ((n_rows, D_A), F32),
    ]
    return pl.pallas_call(
        functools.partial(_in_proj_kernel, n_prompt_tiles),
        grid=(n_tiles,),
        in_specs=_group_specs(n_prompt_tiles, IN_TM, d_model) + [
            pl.BlockSpec((1, d_model), const),
            pl.BlockSpec((d_model, IN_COLS_PAD), const, pipeline_mode=pl.Buffered(1)),
            pl.BlockSpec((1, D_A), const),
            pl.BlockSpec((None, N_HEADS_A, CHUNK, CHUNK), grp),
            pl.BlockSpec((None, CHUNK, N_HEADS_A), grp2),
            tab_spec, tab_spec, tab_spec, tab_spec, tab_spec, tab_spec,
        ],
        out_specs=[pl.BlockSpec((IN_TM, s[1]), row) for s, _ in outs],
        out_shape=[jax.ShapeDtypeStruct(s, d) for s, d in outs],
        compiler_params=_params(("parallel",)),
        name="in_proj",
    )(x_p, x_s, norm_mix, w_in_b, norm_v_a, wmix, bmix, *tables)


ORDER_KEY_NEG_INF = -(2 ** 31) + 0x7FFFFF
KTH_SEARCH_UNROLL = 4


def _order_key_to_float(key):
    return pltpu.bitcast(key ^ ((key >> 31) & jnp.int32(0x7FFFFFFF)), F32)


def _kth_largest(score, k_top):
    def body(it, t):
        cand = t + lax.shift_left(jnp.int32(1), jnp.int32(31) - it)
        cnt = jnp.sum((score >= _order_key_to_float(cand)).astype(F32), axis=1, keepdims=True)
        return jnp.where(cnt >= k_top, cand, t)

    t0 = jnp.full((score.shape[0], 1), INT_MIN, jnp.int32)
    t = lax.fori_loop(0, 32, body, t0, unroll=KTH_SEARCH_UNROLL)
    return _order_key_to_float(jnp.maximum(t, ORDER_KEY_NEG_INF))


def _topk_select(score, k_top, tri):
    thr = _kth_largest(score, k_top)
    above = score > thr
    need = k_top - jnp.sum(above.astype(F32), axis=1, keepdims=True)
    tie = score == thr
    pieces = []
    run = jnp.zeros_like(need)
    for j in range(score.shape[1] // LANES):
        cols = slice(j * LANES, (j + 1) * LANES)
        tie_j = tie[:, cols]
        prefix = _dot(tie_j.astype(BF16), tri) + run
        run = prefix[:, LANES - 1:LANES]
        pieces.append(jnp.where(above[:, cols] | (tie_j & (prefix <= need)), 1.0, 0.0))
    return jnp.concatenate(pieces, axis=1)


def _tri_matrix():
    r = lax.broadcasted_iota(jnp.int32, (LANES, LANES), 0)
    c = lax.broadcasted_iota(jnp.int32, (LANES, LANES), 1)
    return (r <= c).astype(BF16)


PA_TQ = 128


PA_EXTENTS = 8


def _prompt_attn_kernel(k_top, qi_ref, wi_ref, kib_ref, q_ref, kb_ref, vbb_ref, o_ref):
    i = pl.program_id(1)
    step = kb_ref.shape[0] // PA_EXTENTS
    blocks_per_step = step // PA_TQ
    for c in range(PA_EXTENTS):
        @pl.when(i // blocks_per_step == c)
        def _():
            _prompt_attn_block(k_top, (c + 1) * step, i, qi_ref, wi_ref, kib_ref, q_ref, kb_ref, vbb_ref, o_ref)


def _prompt_attn_block(k_top, seq, i, qi_ref, wi_ref, kib_ref, q_ref, kb_ref, vbb_ref, o_ref):
    qi = qi_ref[...]
    wi = wi_ref[...]
    ki2 = kib_ref[:seq, :]
    lane = lax.broadcasted_iota(jnp.int32, (PA_TQ, LANES), 1)
    score = jnp.zeros((PA_TQ, seq), F32)
    for h in range(N_IDX_HEADS):
        pair = qi[:, (h // 2) * LANES:(h // 2 + 1) * LANES]
        keep = (lane < IDX_DIM) if h % 2 == 0 else (lane >= IDX_DIM)
        d = _dot_nt(jnp.where(keep, pair, jnp.zeros_like(pair)), ki2)
        score = score + wi[:, h:h + 1] * jnp.maximum(d, 0.0)
    q_pos = i * PA_TQ + lax.broadcasted_iota(jnp.int32, (PA_TQ, seq), 0)
    s_pos = lax.broadcasted_iota(jnp.int32, (PA_TQ, seq), 1)
    causal = s_pos <= q_pos
    sel = _topk_select(jnp.where(causal, score, NEG_INF), k_top, _tri_matrix())
    bias = jnp.where(causal & (sel > 0.0), 0.0, NEG_INF)

    q = q_ref[...]
    for g in range(N_KV_B):
        q4 = jnp.concatenate(
            [q[:, (g * KV_GROUP + r) * HEAD_DIM:(g * KV_GROUP + r + 1) * HEAD_DIM] for r in range(KV_GROUP)], axis=0)
        kg = kb_ref[:seq, g * HEAD_DIM:(g + 1) * HEAD_DIM]
        vg = vbb_ref[:seq, g * HEAD_DIM:(g + 1) * HEAD_DIM]
        s = _dot_nt(q4, kg).reshape(KV_GROUP, PA_TQ, seq) + bias[None]
        m = jnp.max(s, axis=-1, keepdims=True)
        p = jnp.exp(s - m)
        l = jnp.sum(p, axis=-1, keepdims=True)
        o = _dot(p.reshape(KV_GROUP * PA_TQ, seq).astype(BF16), vg).reshape(KV_GROUP, PA_TQ, HEAD_DIM) / l
        for r in range(KV_GROUP):
            o_ref[:, (g * KV_GROUP + r) * HEAD_DIM:(g * KV_GROUP + r + 1) * HEAD_DIM] = o[r].astype(BF16)


def _prompt_attn(n_batch, seq, k_top, qi, wi, kib, q, kb, vbb):
    nq = seq // PA_TQ
    qrow = lambda b, i: (b * nq + i, 0)
    krow = lambda b, i: (b, 0)
    return pl.pallas_call(
        functools.partial(_prompt_attn_kernel, k_top),
        grid=(n_batch, nq),
        in_specs=[
            pl.BlockSpec((PA_TQ, N_IDX_HEADS * IDX_DIM), qrow),
            pl.BlockSpec((PA_TQ, N_IDX_HEADS), qrow),
            pl.BlockSpec((seq, LANES), krow),
            pl.BlockSpec((PA_TQ, D_B), qrow),
            pl.BlockSpec((seq, D_KV_B), krow),
            pl.BlockSpec((seq, D_KV_B), krow),
        ],
        out_specs=pl.BlockSpec((PA_TQ, D_B), qrow),
        out_shape=jax.ShapeDtypeStruct((n_batch * seq, D_B), BF16),
        compiler_params=_params(("parallel", "parallel")),
        name="prompt_attn",
    )(qi, wi, kib, q, kb, vbb)


SP_PAGES = 16
SP_KEYS = SP_PAGES * PAGE_SIZE
SP_TPAD = SUBLANES


def _page_specs(page_shape, n_steps, layer):
    def make(p):
        def index(b, j, pt):
            return (layer, pt[b, jnp.minimum(j, n_steps - 1) * SP_PAGES + p], 0, 0)
        return pl.BlockSpec((None, None) + page_shape, index)
    return [make(p) for p in range(SP_PAGES)]


def _sample_scores_kernel(n_steps, pt_ref, qs_ref, ws_ref, kin_ref, *refs):
    page_refs, o_ref = refs[:SP_PAGES], refs[SP_PAGES]
    j = pl.program_id(1)
    qs = qs_ref[...]
    ws = ws_ref[...]

    def head_sum(d):
        w = ws * jnp.maximum(d, 0.0)
        return jnp.sum(w.reshape(N_IDX_HEADS, SP_TPAD, d.shape[1]), axis=0)

    @pl.when(j < n_steps)
    def _():
        keys_t = jnp.concatenate([r[...] for r in page_refs], axis=1).astype(BF16)
        o_ref[...] = head_sum(_dot(qs, keys_t))

    @pl.when(j == n_steps)
    def _():
        d = _dot_nt(qs, kin_ref[...])
        o_ref[...] = jnp.concatenate(
            [head_sum(d), jnp.zeros((SP_TPAD, SP_KEYS - d.shape[1]), F32)], axis=1)


def _sample_scores(page_table, qs, ws, ki_new, cache_idx_k, layer):
    n_db, n_pages = page_table.shape
    n_steps = n_pages // SP_PAGES
    rows = N_IDX_HEADS * SP_TPAD
    per_b = lambda b, j, pt: (b, 0, 0)
    grid_spec = pltpu.PrefetchScalarGridSpec(
        num_scalar_prefetch=1,
        grid=(n_db, n_steps + 1),
        in_specs=[
            pl.BlockSpec((None, rows, IDX_DIM), per_b),
            pl.BlockSpec((None, rows, 1), per_b),
            pl.BlockSpec((None, LANES, IDX_DIM), per_b),
        ] + _page_specs((IDX_DIM, PAGE_SIZE), n_steps, layer),
        out_specs=pl.BlockSpec((None, SP_TPAD, SP_KEYS), lambda b, j, pt: (b, 0, j)),
    )
    return pl.pallas_call(
        functools.partial(_sample_scores_kernel, n_steps),
        grid_spec=grid_spec,
        out_shape=jax.ShapeDtypeStruct((n_db, SP_TPAD, (n_steps + 1) * SP_KEYS), F32),
        compiler_params=_params(("parallel", "arbitrary")),
        name="sample_scores",
    )(page_table, qs, ws, ki_new, *([cache_idx_k] * SP_PAGES))


SM_ROWS = 64


def _sample_mask_kernel(k_top, past, n_new, s_ref, o_ref):
    score = s_ref[...]
    t = lax.broadcasted_iota(jnp.int32, score.shape, 0) % SP_TPAD
    s_pos = lax.broadcasted_iota(jnp.int32, score.shape, 1)
    admissible = s_pos <= past + t
    sel = _topk_select(jnp.where(admissible, score, NEG_INF), k_top, _tri_matrix())
    keep = jnp.where((admissible & (sel > 0.0)) | (t >= n_new), 1.0, 0.0).astype(BF16)
    r = lax.broadcasted_iota(jnp.int32, (LANES, N_KV_B * LANES), 0)
    c = lax.broadcasted_iota(jnp.int32, (LANES, N_KV_B * LANES), 1)
    spread = (c // N_KV_B == r).astype(BF16)
    for j in range(score.shape[1] // LANES):
        d = _dot(keep[:, j * LANES:(j + 1) * LANES], spread)
        o_ref[:, j * N_KV_B * LANES:(j + 1) * N_KV_B * LANES] = jnp.where(d > 0.5, 0.0, NEG_INF)


def _sample_mask(scores2d, k_top, past, n_new):
    n_rows, width = scores2d.shape
    return pl.pallas_call(
        functools.partial(_sample_mask_kernel, k_top, past, n_new),
        grid=(n_rows // SM_ROWS,),
        in_specs=[pl.BlockSpec((SM_ROWS, width), lambda i: (i, 0))],
        out_specs=pl.BlockSpec((SM_ROWS, N_KV_B * width), lambda i: (i, 0)),
        out_shape=jax.ShapeDtypeStruct((n_rows, N_KV_B * width), F32),
        compiler_params=_params(("parallel",)),
        name="sample_mask",
    )(scores2d)


SA_ROWS = KV_GROUP * SP_TPAD
SA_ROWS_ALL = N_KV_B * SA_ROWS
SA_PAGE_ROWS = PAGE_SIZE * N_KV_B


def _sample_attn_kernel(n_steps, pt_ref, q_ref, bias_ref, kn_ref, vn_ref, *refs):
    k_refs, v_refs = refs[:SP_PAGES], refs[SP_PAGES:2 * SP_PAGES]
    o_ref, m_ref, l_ref, acc_ref = refs[2 * SP_PAGES:]
    j = pl.program_id(1)

    @pl.when(j == 0)
    def _():
        m_ref[...] = jnp.full(m_ref.shape, NEG_INF, F32)
        l_ref[...] = jnp.zeros(l_ref.shape, F32)
        acc_ref[...] = jnp.zeros(acc_ref.shape, F32)

    def update(kc, vc, bias):
        n = kc.shape[0]
        s = _dot_nt(q_ref[...], kc)
        row_head = lax.broadcasted_iota(jnp.int32, (SA_ROWS_ALL, n), 0) // SA_ROWS
        col_head = lax.broadcasted_iota(jnp.int32, (SA_ROWS_ALL, n), 1) % N_KV_B
        s = jnp.where(row_head == col_head, s + jnp.concatenate([bias] * (SA_ROWS_ALL // SP_TPAD), axis=0), NEG_INF)
        m_old = m_ref[...]
        m_new = jnp.maximum(m_old, jnp.max(s, axis=-1, keepdims=True))
        m_safe = jnp.where(m_new == NEG_INF, 0.0, m_new)
        alpha = jnp.exp(m_old - m_safe)
        p = jnp.exp(s - m_safe)
        l_ref[...] = alpha * l_ref[...] + jnp.sum(p, axis=-1, keepdims=True)
        acc_ref[...] = alpha * acc_ref[...] + _dot(p.astype(BF16), vc)
        m_ref[...] = m_new

    @pl.when(j < n_steps)
    def _():
        kc = jnp.concatenate([r[...] for r in k_refs], axis=0).astype(BF16)
        vc = jnp.concatenate([r[...] for r in v_refs], axis=0).astype(BF16)
        update(kc, vc, bias_ref[...])

    @pl.when(j == n_steps)
    def _():
        update(kn_ref[...], vn_ref[...], bias_ref[:, :SA_PAGE_ROWS])
        o_ref[...] = acc_ref[...] / l_ref[...]


def _sample_attn(page_table, qs, bias, k_new, v_new, cache_k, cache_v, layer):
    n_db, n_pages = page_table.shape
    n_steps = n_pages // SP_PAGES
    per_b = lambda b, j, pt: (b, 0, 0)
    grid_spec = pltpu.PrefetchScalarGridSpec(
        num_scalar_prefetch=1,
        grid=(n_db, n_steps + 1),
        in_specs=[
            pl.BlockSpec((None, SA_ROWS_ALL, HEAD_DIM), per_b),
            pl.BlockSpec((None, SP_TPAD, N_KV_B * SP_KEYS), lambda b, j, pt: (b, 0, j)),
            pl.BlockSpec((None, SA_PAGE_ROWS, HEAD_DIM), per_b),
            pl.BlockSpec((None, SA_PAGE_ROWS, HEAD_DIM), per_b),
        ] + 2 * _page_specs((SA_PAGE_ROWS, HEAD_DIM), n_steps, layer),
        out_specs=pl.BlockSpec((None, SA_ROWS_ALL, HEAD_DIM), per_b),
        scratch_shapes=[
            pltpu.VMEM((SA_ROWS_ALL, 1), F32),
            pltpu.VMEM((SA_ROWS_ALL, 1), F32),
            pltpu.VMEM((SA_ROWS_ALL, HEAD_DIM), F32),
        ],
    )
    return pl.pallas_call(
        functools.partial(_sample_attn_kernel, n_steps),
        grid_spec=grid_spec,
        out_shape=jax.ShapeDtypeStruct((n_db, SA_ROWS_ALL, HEAD_DIM), F32),
        compiler_params=_params(("parallel", "arbitrary")),
        name="sample_attn",
    )(page_table, qs, bias, k_new, v_new, *([cache_k] * SP_PAGES), *([cache_v] * SP_PAGES))


OR_TM = 256


def _split_bf16(x):
    hi = x.astype(BF16)
    return hi, (x - hi.astype(F32)).astype(BF16)


def _out_router_kernel(n_prompt_tiles, a_ref, b_ref, xp_ref, xs_ref, wo_ref, nf_ref, wrh_ref, wrl_ref, br_ref,
                       xmid_ref, h_ref, idx_ref, gate_ref):
    x = _group_rows(n_prompt_tiles, xp_ref, xs_ref)
    x_mid = x + _dot(a_ref[...], wo_ref[:D_A, :]) + _dot(b_ref[...], wo_ref[D_A:, :])
    xmid_ref[...] = x_mid
    h = (x_mid * lax.rsqrt(jnp.mean(x_mid * x_mid, axis=-1, keepdims=True) + RMS_EPS)) * nf_ref[...]
    h_ref[...] = h
    h_hi, h_lo = _split_bf16(h)
    wr_hi = wrh_ref[...]
    logits = _dot(h_hi, wr_hi) + _dot(h_lo, wr_hi) + _dot(h_hi, wrl_ref[...]) + br_ref[...]
    lane = lax.broadcasted_iota(jnp.int32, logits.shape, 1)
    vals, idxs = [], []
    for _ in range(TOP_K_EXPERTS):
        m = jnp.max(logits, axis=-1, keepdims=True)
        i = jnp.min(jnp.where(logits == m, lane, LANES), axis=-1, keepdims=True)
        vals.append(m)
        idxs.append(i)
        logits = jnp.where(lane == i, NEG_INF, logits)
    exps = [jnp.exp(v - vals[0]) for v in vals]
    total = exps[0]
    for e in exps[1:]:
        total = total + e
    idx_out = jnp.zeros(logits.shape, jnp.int32)
    gate_out = jnp.zeros(logits.shape, F32)
    for k in range(TOP_K_EXPERTS):
        idx_out = jnp.where(lane == k, idxs[k], idx_out)
        gate_out = jnp.where(lane == k, exps[k] / total, gate_out)
    idx_ref[...] = idx_out
    gate_ref[...] = gate_out


def _out_router(a_out, b_out, x_p, x_s, w_out_b, norm_ffn, wr_hi, wr_lo, br_pad):
    d_model = x_p.shape[1]
    n_prompt_tiles = x_p.shape[0] // OR_TM
    n_rows = x_p.shape[0] + x_s.shape[0]
    row = lambda i: (i, 0)
    const = lambda i: (0, 0)
    return pl.pallas_call(
        functools.partial(_out_router_kernel, n_prompt_tiles),
        grid=(n_rows // OR_TM,),
        in_specs=[
            pl.BlockSpec((OR_TM, D_A), row),
            pl.BlockSpec((OR_TM, D_B), row),
        ] + _group_specs(n_prompt_tiles, OR_TM, d_model) + [
            pl.BlockSpec((D_A + D_B, d_model), const),
            pl.BlockSpec((1, d_model), const),
            pl.BlockSpec((d_model, LANES), const),
            pl.BlockSpec((d_model, LANES), const),
            pl.BlockSpec((1, LANES), const),
        ],
        out_specs=[
            pl.BlockSpec((OR_TM, d_model), row),
            pl.BlockSpec((OR_TM, d_model), row),
            pl.BlockSpec((OR_TM, LANES), row),
            pl.BlockSpec((OR_TM, LANES), row),
        ],
        out_shape=[
            jax.ShapeDtypeStruct((n_rows, d_model), F32),
            jax.ShapeDtypeStruct((n_rows, d_model), F32),
            jax.ShapeDtypeStruct((n_rows, LANES), jnp.int32),
            jax.ShapeDtypeStruct((n_rows, LANES), F32),
        ],
        compiler_params=_params(("parallel",)),
        name="out_router",
    )(a_out, b_out, x_p, x_s, w_out_b, norm_ffn, wr_hi, wr_lo, br_pad)


MOE_TM = 256
MOE_UNIT_TILES = 5
MOE_R = MOE_TM * MOE_UNIT_TILES
MOE_TF = 256
COMBINE_TM = 128
ROW_COPY_UNROLL = 8


def _row_copy(src_ref, src_row, dst_ref, dst_row, sem):
    return pltpu.make_async_copy(src_ref.at[pl.ds(src_row, 1), :], dst_ref.at[pl.ds(dst_row, 1), :], sem)


def _tile_rows(rows):
    return pl.cdiv(rows, MOE_TM) * MOE_TM


def _moe_kernel(ue_ref, ur_ref, ub_ref, tok_ref, tokn_ref, h_ref, wg_ref, wu_ref, bg_ref, bu_ref, wd_ref, bd_ref,
                ys_ref, land_ref, xb_ref, y_ref, wgb_ref, wub_ref, wdb_ref, gsem, ysem):
    u, f = pl.program_id(0), pl.program_id(1)
    n_units, n_f = pl.num_programs(0), pl.num_programs(1)
    rows = ur_ref[u]
    n_tiles = pl.cdiv(rows, MOE_TM)

    def gather(tokens_ref, n_rows, wait):
        def tile(m, c):
            def body(j, c2):
                r = m * MOE_TM + j
                cp = _row_copy(h_ref, tokens_ref[0, r], land_ref, r, gsem)
                cp.wait() if wait else cp.start()
                return c2
            return lax.fori_loop(0, MOE_TM, body, c, unroll=ROW_COPY_UNROLL)
        lax.fori_loop(0, n_rows // MOE_TM, tile, 0)

    def writeback(unit, n, wait):
        def body(m, c):
            r0 = pl.multiple_of(m * MOE_TM, MOE_TM)
            cp = pltpu.make_async_copy(y_ref.at[pl.ds(r0, MOE_TM), :],
                                       ys_ref.at[pl.ds(ub_ref[unit] * MOE_R + r0, MOE_TM), :], ysem)
            cp.wait() if wait else cp.start()
            return c
        lax.fori_loop(0, n, body, 0)

    @pl.when((f == 0) & (rows > 0))
    def _():
        @pl.when(u == 0)
        def _():
            gather(tok_ref, _tile_rows(rows), wait=False)

        gather(tok_ref, _tile_rows(rows), wait=True)

        def convert(m, c):
            r0 = pl.multiple_of(m * MOE_TM, MOE_TM)
            xb_ref[pl.ds(r0, MOE_TM), :] = land_ref[pl.ds(r0, MOE_TM), :].astype(BF16)
            return c
        lax.fori_loop(0, n_tiles, convert, 0)

        nxt = jnp.minimum(u + 1, n_units - 1)
        rows_next = jnp.where(u + 1 < n_units, ur_ref[nxt], 0)
        gather(tokn_ref, _tile_rows(rows_next), wait=False)

    @pl.when((f == 0) & (u > 0))
    def _():
        writeback(u - 1, pl.cdiv(ur_ref[jnp.maximum(u - 1, 0)], MOE_TM), wait=True)

    @pl.when(rows > 0)
    def _():
        wgb_ref[...] = wg_ref[...].astype(BF16)
        wub_ref[...] = wu_ref[...].astype(BF16)
        wdb_ref[...] = wd_ref[...].astype(BF16)

        def tile(m, c):
            r0 = pl.multiple_of(m * MOE_TM, MOE_TM)
            xm = xb_ref[pl.ds(r0, MOE_TM), :]
            gate = jnp.minimum(_dot(xm, wgb_ref[...]) + bg_ref[...], SWIGLU_LIMIT)
            up = jnp.clip(_dot(xm, wub_ref[...]) + bu_ref[...], -SWIGLU_LIMIT, SWIGLU_LIMIT)
            act = (up + 1.0) * (gate * jax.nn.sigmoid(SWIGLU_ALPHA * gate))
            y_ref[pl.ds(r0, MOE_TM), :] += _dot(act.astype(BF16), wdb_ref[...])
            return c

        @pl.when(f == 0)
        def _():
            def init(m, c):
                r0 = pl.multiple_of(m * MOE_TM, MOE_TM)
                y_ref[pl.ds(r0, MOE_TM), :] = jnp.broadcast_to(bd_ref[...], (MOE_TM, y_ref.shape[1]))
                return c
            lax.fori_loop(0, n_tiles, init, 0)

        def tile_pair(p, c):
            tile(2 * p, c)
            return tile(2 * p + 1, c)

        lax.fori_loop(0, n_tiles // 2, tile_pair, 0)

        @pl.when(n_tiles % 2 == 1)
        def _():
            tile(n_tiles - 1, 0)

        @pl.when(f == n_f - 1)
        def _():
            writeback(u, n_tiles, wait=False)

    @pl.when((f == n_f - 1) & (u == n_units - 1))
    def _():
        writeback(u, n_tiles, wait=True)


def _moe_experts(unit_expert, unit_rows, unit_block, slot_token, h, w_gate_up, b_gate_up, w_down, b_down):
    n_units = unit_expert.shape[0]
    d_model = h.shape[1]
    d_ff = w_down.shape[1]
    n_f = d_ff // MOE_TF

    def f_eff(u, f, ur):
        return jnp.where(ur[u] > 0, f, n_f - 1)

    smem_tokens = lambda index: pl.BlockSpec((None, 1, MOE_R), index, memory_space=pltpu.SMEM)
    grid_spec = pltpu.PrefetchScalarGridSpec(
        num_scalar_prefetch=3,
        grid=(n_units, n_f),
        in_specs=[
            smem_tokens(lambda u, f, ue, ur, ub: (ub[u], 0, 0)),
            smem_tokens(lambda u, f, ue, ur, ub: (ub[jnp.minimum(u + 1, n_units - 1)], 0, 0)),
            pl.BlockSpec(memory_space=pl.ANY),
            pl.BlockSpec((None, d_model, MOE_TF), lambda u, f, ue, ur, ub: (ue[u], 0, f_eff(u, f, ur))),
            pl.BlockSpec((None, d_model, MOE_TF), lambda u, f, ue, ur, ub: (ue[u], 0, n_f + f_eff(u, f, ur))),
            pl.BlockSpec((None, 1, MOE_TF), lambda u, f, ue, ur, ub: (ue[u], 0, f_eff(u, f, ur))),
            pl.BlockSpec((None, 1, MOE_TF), lambda u, f, ue, ur, ub: (ue[u], 0, n_f + f_eff(u, f, ur))),
            pl.BlockSpec((None, MOE_TF, d_model), lambda u, f, ue, ur, ub: (ue[u], f_eff(u, f, ur), 0)),
            pl.BlockSpec((None, 1, d_model), lambda u, f, ue, ur, ub: (ue[u], 0, 0)),
        ],
        out_specs=pl.BlockSpec(memory_space=pl.ANY),
        scratch_shapes=[
            pltpu.VMEM((MOE_R, d_model), F32),
            pltpu.VMEM((MOE_R, d_model), BF16),
            pltpu.VMEM((MOE_R, d_model), F32),
            pltpu.VMEM((d_model, MOE_TF), BF16),
            pltpu.VMEM((d_model, MOE_TF), BF16),
            pltpu.VMEM((MOE_TF, d_model), BF16),
            pltpu.SemaphoreType.DMA,
            pltpu.SemaphoreType.DMA,
        ],
    )
    tokens = slot_token.reshape(n_units, 1, MOE_R)
    return pl.pallas_call(
        _moe_kernel,
        grid_spec=grid_spec,
        out_shape=jax.ShapeDtypeStruct((n_units * MOE_R, d_model), F32),
        compiler_params=_params(("arbitrary", "arbitrary")),
        name="moe_experts",
    )(unit_expert, unit_rows, unit_block, tokens, tokens, h, w_gate_up, w_gate_up, b_gate_up, b_gate_up,
      w_down, b_down)


def _combine_kernel(n_prompt_tiles, pos_ref, posn_ref, gate_ref, xmid_ref, nfin_ref, ys_ref, op_ref, os_ref,
                    buf_ref, sem):
    i, n_tiles = pl.program_id(0), pl.num_programs(0)
    slot = i % 2

    def rows(p_ref, s, wait):
        for k in range(TOP_K_EXPERTS):
            def body(t, c):
                cp = _row_copy(ys_ref, p_ref[0, t * TOP_K_EXPERTS + k], buf_ref.at[s], k * COMBINE_TM + t,
                               sem.at[s])
                cp.wait() if wait else cp.start()
                return c
            lax.fori_loop(0, COMBINE_TM, body, 0, unroll=ROW_COPY_UNROLL)

    @pl.when(i == 0)
    def _():
        rows(pos_ref, 0, wait=False)

    @pl.when(i + 1 < n_tiles)
    def _():
        rows(posn_ref, 1 - slot, wait=False)

    rows(pos_ref, slot, wait=True)
    gates = gate_ref[...]
    out = xmid_ref[...]
    for k in range(TOP_K_EXPERTS):
        out = out + gates[:, k:k + 1] * buf_ref[slot, k * COMBINE_TM:(k + 1) * COMBINE_TM, :]
    y = (out * lax.rsqrt(jnp.mean(out * out, axis=-1, keepdims=True) + RMS_EPS)) * nfin_ref[...]

    @pl.when(i < n_prompt_tiles)
    def _():
        op_ref[...] = y

    @pl.when(i >= n_prompt_tiles)
    def _():
        os_ref[...] = y


def _combine(pos, gates, x_mid, norm_final, ys, n_prompt, n_sample):
    d_model = x_mid.shape[1]
    n_prompt_tiles = n_prompt // COMBINE_TM
    n_tiles = n_prompt_tiles + n_sample // COMBINE_TM
    n = COMBINE_TM * TOP_K_EXPERTS
    pos3 = pos.reshape(n_tiles, 1, n)
    row = lambda i: (i, 0)
    return pl.pallas_call(
        functools.partial(_combine_kernel, n_prompt_tiles),
        grid=(n_tiles,),
        in_specs=[
            pl.BlockSpec((None, 1, n), lambda i: (i, 0, 0), memory_space=pltpu.SMEM),
            pl.BlockSpec((None, 1, n), lambda i: (jnp.minimum(i + 1, n_tiles - 1), 0, 0), memory_space=pltpu.SMEM),
            pl.BlockSpec((COMBINE_TM, LANES), row),
            pl.BlockSpec((COMBINE_TM, d_model), row),
            pl.BlockSpec((1, d_model), lambda i: (0, 0)),
            pl.BlockSpec(memory_space=pl.ANY),
        ],
        out_specs=[
            pl.BlockSpec((COMBINE_TM, d_model), lambda i: (jnp.minimum(i, n_prompt_tiles - 1), 0)),
            pl.BlockSpec((COMBINE_TM, d_model), lambda i: (jnp.maximum(i - n_prompt_tiles, 0), 0)),
        ],
        out_shape=[jax.ShapeDtypeStruct((n_prompt, d_model), F32), jax.ShapeDtypeStruct((n_sample, d_model), F32)],
        scratch_shapes=[pltpu.VMEM((2, n, d_model), F32), pltpu.SemaphoreType.DMA((2,))],
        compiler_params=_params(("arbitrary",)),
        name="combine",
    )(pos3, pos3, gates, x_mid, norm_final, ys)


def _routing(top_idx, n_units, zero_row):
    flat_e = top_idx.reshape(-1)
    onehot = (flat_e[:, None] == jnp.arange(N_EXPERTS, dtype=jnp.int32)[None, :]).astype(jnp.int32)
    csum = jnp.cumsum(onehot, axis=0)
    rank = jnp.take_along_axis(csum, flat_e[:, None], axis=1)[:, 0] - 1
    counts = csum[-1]
    units_e = (counts + MOE_R - 1) // MOE_R
    units_end = jnp.cumsum(units_e)
    units_start = units_end - units_e
    pos = (units_start[flat_e] + rank // MOE_R) * MOE_R + rank % MOE_R
    u = jnp.arange(n_units, dtype=jnp.int32)
    total = units_end[-1]
    ue = jnp.minimum(jnp.sum((units_end[None, :] <= u[:, None]).astype(jnp.int32), axis=1), N_EXPERTS - 1)
    rows = jnp.clip(counts[ue] - (u - units_start[ue]) * MOE_R, 0, MOE_R)
    used = u < total
    last_e = ue[jnp.maximum(total - 1, 0)]
    unit_expert = jnp.where(used, ue, last_e).astype(jnp.int32)
    unit_rows = jnp.where(used, rows, 0).astype(jnp.int32)
    unit_block = jnp.minimum(u, total - 1).astype(jnp.int32)
    pos = pos.astype(jnp.int32)
    token = jnp.arange(flat_e.shape[0], dtype=jnp.int32) // TOP_K_EXPERTS
    slot_token = jnp.full((n_units * MOE_R,), zero_row, jnp.int32).at[pos].set(token, unique_indices=True)
    return pos, slot_token, unit_expert, unit_rows, unit_block


def kernel(x_prompt, x_sample, cache_k, cache_v, cache_idx_k, page_table, norm_mix, w_in, norm_v_a,
           w_spatial, b_spatial, w_out, norm_ffn, w_router, b_router, w_gate_up, b_gate_up, w_down,
           b_down, norm_final):
    n_b, seq, d_model = x_prompt.shape
    n_db, n_new, _ = x_sample.shape
    depth = w_in.shape[0]
    past = page_table.shape[1] * PAGE_SIZE
    n_prompt, n_sample = n_b * seq, n_db * n_new
    n_tokens = n_prompt + n_sample
    assert seq % IN_TM == 0 and seq % PA_TQ == 0 and CHUNK % n_new == 0 and n_new <= SP_TPAD
    assert n_sample % CHUNK == 0 and page_table.shape[1] % SP_PAGES == 0
    assert n_prompt % COMBINE_TM == 0 and n_sample % COMBINE_TM == 0
    assert depth == 1, "the final norm is fused into the last layer's combine; deeper stacks are not wired up"
    n_rows = (n_tokens // IN_TM + 1) * IN_TM
    k_top_prompt = min(TOPK_MAX, seq // 4)
    k_top_sample = min(TOPK_MAX, (past + n_new) // 4)
    n_assign = n_tokens * TOP_K_EXPERTS
    n_units = N_EXPERTS + n_assign // MOE_R

    pos_rows = jnp.concatenate([jnp.arange(seq, dtype=jnp.int32),
                                past + jnp.arange(IN_TM, dtype=jnp.int32) % n_new])
    tables = _rope_tables(pos_rows, ROT_DIM, HEAD_DIM) + _rope_tables(pos_rows, IDX_ROT_DIM, IDX_DIM)

    x_p = x_prompt.reshape(n_prompt, d_model)
    x_s = jnp.pad(x_sample.reshape(n_sample, d_model), ((0, n_rows - n_tokens), (0, 0)))
    outs = []
    for l in range(depth):
        causal = jnp.tril(jnp.ones((CHUNK, CHUNK), bool))
        w_prompt = jnp.where(causal[None], w_spatial[l], 0.0)
        eye = jnp.eye(CHUNK // n_new, dtype=F32)
        w_decode = jax.vmap(lambda w: jnp.kron(eye, w))(w_prompt[:, :n_new, :n_new])
        wmix = jnp.stack([w_prompt, w_decode]).astype(BF16)
        b_decode = jnp.tile(b_spatial[l][:, :n_new], (1, CHUNK // n_new))
        bmix = jnp.stack([b_spatial[l].T, b_decode.T])
        w_in_b = jnp.pad(w_in[l], ((0, 0), (0, IN_COLS_PAD - IN_COLS))).astype(BF16)

        (a_out, q, k, kb, vb, vbb, qi, ki, kib, wi, va) = _in_proj(
            x_p, x_s, seq // IN_TM, norm_mix[l][None], w_in_b, norm_v_a[l][None], wmix, bmix, tables)

        b_prompt = _prompt_attn(n_b, seq, k_top_prompt, qi, wi, kib, q, kb, vbb)

        sl = slice(n_prompt, n_tokens)
        pad_t = ((0, 0), (0, 0), (0, SP_TPAD - n_new), (0, 0))
        qi_s = qi[sl].reshape(n_db, n_new, N_IDX_HEADS, IDX_DIM).transpose(0, 2, 1, 3)
        qs_idx = jnp.pad(qi_s, pad_t).reshape(n_db, N_IDX_HEADS * SP_TPAD, IDX_DIM)
        wi_s = wi[sl].reshape(n_db, n_new, N_IDX_HEADS).transpose(0, 2, 1)
        ws_idx = jnp.pad(wi_s, ((0, 0), (0, 0), (0, SP_TPAD - n_new))).reshape(n_db, N_IDX_HEADS * SP_TPAD, 1)
        pad_rows = ((0, 0), (0, LANES - n_new), (0, 0))
        ki_new = jnp.pad(kib[sl, :IDX_DIM].reshape(n_db, n_new, IDX_DIM), pad_rows)
        scores = _sample_scores(page_table, qs_idx, ws_idx, ki_new, jnp.swapaxes(cache_idx_k, 2, 3), l)
        width = scores.shape[-1]
        bias = _sample_mask(scores.reshape(n_db * SP_TPAD, width), k_top_sample, past, n_new)
        q_s = q[sl].reshape(n_db, n_new, N_KV_B, KV_GROUP, HEAD_DIM).transpose(0, 2, 3, 1, 4)
        qs_att = jnp.pad(q_s, ((0, 0), (0, 0), (0, 0), (0, SP_TPAD - n_new), (0, 0))).reshape(
            n_db, SA_ROWS_ALL, HEAD_DIM)
        pad_kv = ((0, 0), (0, SA_PAGE_ROWS - n_new * N_KV_B), (0, 0))
        k_new = jnp.pad(kb[sl].reshape(n_db, n_new * N_KV_B, HEAD_DIM), pad_kv)
        v_new = jnp.pad(vbb[sl].reshape(n_db, n_new * N_KV_B, HEAD_DIM), pad_kv)
        n_pool = cache_k.shape[1]
        o_s = _sample_attn(page_table, qs_att, bias.reshape(n_db, SP_TPAD, N_KV_B * width), k_new, v_new,
                           cache_k.reshape(depth, n_pool, SA_PAGE_ROWS, HEAD_DIM),
                           cache_v.reshape(depth, n_pool, SA_PAGE_ROWS, HEAD_DIM), l)
        b_sample = o_s.reshape(n_db, N_KV_B, KV_GROUP, SP_TPAD, HEAD_DIM)[:, :, :, :n_new]
        b_sample = b_sample.transpose(0, 3, 1, 2, 4).reshape(n_sample, D_B).astype(BF16)
        b_out = jnp.concatenate([b_prompt, b_sample, jnp.zeros((n_rows - n_tokens, D_B), BF16)])

        wr_pad = jnp.pad(w_router[l], ((0, 0), (0, LANES - N_EXPERTS)))
        wr_hi, wr_lo = _split_bf16(wr_pad)
        br_pad = jnp.pad(b_router[l], (0, LANES - N_EXPERTS), constant_values=NEG_INF)[None]
        x_mid, h, top_idx, gates = _out_router(a_out, b_out, x_p, x_s, w_out[l].astype(BF16),
                                               norm_ffn[l][None], wr_hi, wr_lo, br_pad)

        pos, slot_token, unit_expert, unit_rows, unit_block = _routing(
            top_idx[:n_tokens, :TOP_K_EXPERTS], n_units, zero_row=n_tokens)
        ys = _moe_experts(unit_expert, unit_rows, unit_block, slot_token, h, w_gate_up[l],
                          b_gate_up[l][:, None, :], w_down[l], b_down[l][:, None, :])
        y_p, y_s = _combine(pos, gates, x_mid, norm_final[None], ys, n_prompt, n_sample)
        outs.append((k, vb, ki, va))

    def stack(j, lo, hi, shape):
        return jnp.stack([o[j][lo:hi].reshape(shape) for o in outs])

    return (
        y_p.reshape(n_b, seq, d_model), y_s.reshape(n_db, n_new, d_model),
        stack(0, 0, n_prompt, (n_b, seq, N_KV_B, HEAD_DIM)),
        stack(1, 0, n_prompt, (n_b, seq, N_KV_B, HEAD_DIM)),
        stack(2, 0, n_prompt, (n_b, seq, IDX_DIM)),
        stack(0, n_prompt, n_tokens, (n_db, n_new, N_KV_B, HEAD_DIM)),
        stack(1, n_prompt, n_tokens, (n_db, n_new, N_KV_B, HEAD_DIM)),
        stack(2, n_prompt, n_tokens, (n_db, n_new, IDX_DIM)),
        stack(3, n_prompt, n_tokens, (n_db, n_new, N_HEADS_A, HEAD_DIM)),
    )
```

```python
import functools

import jax
import jax.numpy as jnp
from jax import lax
from jax.experimental import pallas as pl
from jax.experimental.pallas import tpu as pltpu

HEAD_DIM = 128
N_HEADS_A = 8
D_A = N_HEADS_A * HEAD_DIM
CHUNK = 128
N_HEADS_B = 8
N_KV_B = 2
KV_GROUP = N_HEADS_B // N_KV_B
D_B = N_HEADS_B * HEAD_DIM
D_KV_B = N_KV_B * HEAD_DIM
ROT_DIM = HEAD_DIM // 4
ROPE_THETA = 500000.0
N_IDX_HEADS = 16
IDX_DIM = 64
IDX_ROT_DIM = IDX_DIM // 4
TOPK_MAX = 256
N_EXPERTS = 32
TOP_K_EXPERTS = 4
SWIGLU_LIMIT = 7.0
SWIGLU_ALPHA = 1.702
RMS_EPS = 1e-5
PAGE_SIZE = 128

OFF_U = 0
OFF_V = OFF_U + D_A
OFF_Q = OFF_V + D_A
OFF_K = OFF_Q + D_B
OFF_VB = OFF_K + D_KV_B
OFF_QI = OFF_VB + D_KV_B
OFF_KI = OFF_QI + N_IDX_HEADS * IDX_DIM
OFF_WI = OFF_KI + IDX_DIM
IN_COLS = OFF_WI + N_IDX_HEADS

LANES = 128
SUBLANES = 8
VMEM_LIMIT_BYTES = 56 * 1024 * 1024

IN_COLS_PAD = pl.cdiv(IN_COLS, LANES) * LANES
INT_MIN = -(2 ** 31)
NEG_INF = float("-inf")

F32 = jnp.float32
BF16 = jnp.bfloat16


def _params(semantics):
    return pltpu.CompilerParams(dimension_semantics=semantics, vmem_limit_bytes=VMEM_LIMIT_BYTES)


def _dot(a, b):
    return jnp.dot(a, b, preferred_element_type=F32)


def _dot_nt(a, b):
    return lax.dot_general(a, b, (((1,), (1,)), ((), ())), preferred_element_type=F32)


IN_TM = 256


def _rope_lanes(z, cos, sin_lo, sin_hi, half):
    return z * cos + pltpu.roll(z, LANES - half, 1) * sin_lo + pltpu.roll(z, half, 1) * sin_hi


def _gelu(z):
    return 0.5 * z * (1.0 + lax.erf(z * (2.0 ** -0.5)))


def _group_rows(n_prompt_tiles, xp_ref, xs_ref):
    return jnp.where(pl.program_id(0) < n_prompt_tiles, xp_ref[...], xs_ref[...])


def _group_specs(n_prompt_tiles, tm, d_model):
    prompt = pl.BlockSpec((tm, d_model), lambda i: (jnp.minimum(i, n_prompt_tiles - 1), 0))
    decode = pl.BlockSpec((tm, d_model), lambda i: (jnp.maximum(i - n_prompt_tiles, 0), 0))
    return [prompt, decode]


def _in_proj_kernel(n_prompt_tiles, xp_ref, xs_ref, g_ref, w_ref, nva_ref, wmix_ref, bmix_ref,
                    cq_ref, saq_ref, sbq_ref, ci_ref, sai_ref, sbi_ref,
                    aout_ref, q_ref, k_ref, kb_ref, vb_ref, vbb_ref, qi_ref, ki_ref, kib_ref, wi_ref, va_ref):
    x = _group_rows(n_prompt_tiles, xp_ref, xs_ref)
    ms = jnp.mean(x * x, axis=-1, keepdims=True)
    xb = ((x * lax.rsqrt(ms + RMS_EPS)) * g_ref[...]).astype(BF16)

    u = _gelu(_dot(xb, w_ref[:, OFF_U:OFF_V]))
    gv = _gelu(_dot(xb, w_ref[:, OFF_V:OFF_Q]))
    va = gv * lax.rsqrt(jnp.mean(gv * gv, axis=-1, keepdims=True) + RMS_EPS) * nva_ref[...]
    va_ref[...] = va
    vab = va.astype(BF16)
    for sub in range(IN_TM // CHUNK):
        rows = slice(sub * CHUNK, (sub + 1) * CHUNK)
        for h in range(N_HEADS_A):
            cols = slice(h * HEAD_DIM, (h + 1) * HEAD_DIM)
            mixed = _dot(wmix_ref[h], vab[rows, cols]) + bmix_ref[:, h:h + 1]
            aout_ref[rows, cols] = (u[rows, cols] * mixed).astype(BF16)

    cq, saq, sbq = cq_ref[...], saq_ref[...], sbq_ref[...]
    zq = _dot(xb, w_ref[:, OFF_Q:OFF_K])
    for h in range(N_HEADS_B):
        cols = slice(h * HEAD_DIM, (h + 1) * HEAD_DIM)
        r = _rope_lanes(zq[:, cols], cq, saq, sbq, ROT_DIM // 2)
        q_ref[:, cols] = (r * (HEAD_DIM ** -0.5)).astype(BF16)
    zk = _dot(xb, w_ref[:, OFF_K:OFF_VB])
    for h in range(N_KV_B):
        cols = slice(h * HEAD_DIM, (h + 1) * HEAD_DIM)
        r = _rope_lanes(zk[:, cols], cq, saq, sbq, ROT_DIM // 2)
        k_ref[:, cols] = r
        kb_ref[:, cols] = r.astype(BF16)
    zvb = _dot(xb, w_ref[:, OFF_VB:OFF_QI])
    vb_ref[...] = zvb
    vbb_ref[...] = zvb.astype(BF16)

    ci, sai, sbi = ci_ref[...], sai_ref[...], sbi_ref[...]
    zqi = _dot(xb, w_ref[:, OFF_QI:OFF_KI])
    for p in range(N_IDX_HEADS * IDX_DIM // LANES):
        cols = slice(p * LANES, (p + 1) * LANES)
        qi_ref[:, cols] = _rope_lanes(zqi[:, cols], ci, sai, sbi, IDX_ROT_DIM // 2).astype(BF16)
    zl = _dot(xb, w_ref[:, OFF_KI:IN_COLS_PAD])
    r = _rope_lanes(zl, ci, sai, sbi, IDX_ROT_DIM // 2)
    ki_ref[...] = r[:, :IDX_DIM]
    lane = lax.broadcasted_iota(jnp.int32, r.shape, 1)
    kib_ref[...] = jnp.where(lane < IDX_DIM, r, pltpu.roll(r, IDX_DIM, 1)).astype(BF16)
    wi_ref[...] = zl[:, IDX_DIM:IDX_DIM + N_IDX_HEADS] * (N_IDX_HEADS ** -0.5 * IDX_DIM ** -0.5)


def _rope_tables(pos, rot_dim, head_dim):
    half = rot_dim // 2
    inv = ROPE_THETA ** (-jnp.arange(half, dtype=F32) / half)
    ang = pos.astype(F32)[:, None] * inv[None, :]
    cos, sin = jnp.cos(ang), jnp.sin(ang)
    lane = jnp.arange(LANES) % head_dim
    cos_l = jnp.take(cos, lane % half, axis=1)
    sin_l = jnp.take(sin, lane % half, axis=1)
    c = jnp.where(lane[None] < rot_dim, cos_l, 1.0)
    s_lo = jnp.where(lane[None] < half, -sin_l, 0.0)
    s_hi = jnp.where((lane[None] >= half) & (lane[None] < rot_dim), sin_l, 0.0)
    return c.astype(F32), s_lo.astype(F32), s_hi.astype(F32)


def _in_proj(x_p, x_s, tiles_per_seq, norm_mix, w_in_b, norm_v_a, wmix, bmix, tables):
    d_model = x_p.shape[1]
    n_prompt_tiles = x_p.shape[0] // IN_TM
    n_rows = x_p.shape[0] + x_s.shape[0]
    n_tiles = n_rows // IN_TM
    row = lambda i: (i, 0)
    const = lambda i: (0, 0)
    tab = lambda i: (jnp.where(i < n_prompt_tiles, i % tiles_per_seq, tiles_per_seq), 0)
    grp = lambda i: (jnp.where(i < n_prompt_tiles, 0, 1), 0, 0, 0)
    grp2 = lambda i: (jnp.where(i < n_prompt_tiles, 0, 1), 0, 0)
    tab_spec = pl.BlockSpec((IN_TM, LANES), tab)
    outs = [
        ((n_rows, D_A), BF16),
        ((n_rows, D_B), BF16),
        ((n_rows, D_KV_B), F32),
        ((n_rows, D_KV_B), BF16),
        ((n_rows, D_KV_B), F32),
        ((n_rows, D_KV_B), BF16),
        ((n_rows, N_IDX_HEADS * IDX_DIM), BF16),
        ((n_rows, IDX_DIM), F32),
        ((n_rows, LANES), BF16),
        ((n_rows, N_IDX_HEADS), F32),
        ((n_rows, D_A), F32),
    ]
    return pl.pallas_call(
        functools.partial(_in_proj_kernel, n_prompt_tiles),
        grid=(n_tiles,),
        in_specs=_group_specs(n_prompt_tiles, IN_TM, d_model) + [
            pl.BlockSpec((1, d_model), const),
            pl.BlockSpec((d_model, IN_COLS_PAD), const, pipeline_mode=pl.Buffered(1)),
            pl.BlockSpec((1, D_A), const),
            pl.BlockSpec((None, N_HEADS_A, CHUNK, CHUNK), grp),
            pl.BlockSpec((None, CHUNK, N_HEADS_A), grp2),
            tab_spec, tab_spec, tab_spec, tab_spec, tab_spec, tab_spec,
        ],
        out_specs=[pl.BlockSpec((IN_TM, s[1]), row) for s, _ in outs],
        out_shape=[jax.ShapeDtypeStruct(s, d) for s, d in outs],
        compiler_params=_params(("parallel",)),
        name="in_proj",
    )(x_p, x_s, norm_mix, w_in_b, norm_v_a, wmix, bmix, *tables)


ORDER_KEY_NEG_INF = -(2 ** 31) + 0x7FFFFF
KTH_SEARCH_UNROLL = 4


def _order_key_to_float(key):
    return pltpu.bitcast(key ^ ((key >> 31) & jnp.int32(0x7FFFFFFF)), F32)


def _kth_largest(score, k_top):
    def body(it, t):
        cand = t + lax.shift_left(jnp.int32(1), jnp.int32(31) - it)
        cnt = jnp.sum((score >= _order_key_to_float(cand)).astype(F32), axis=1, keepdims=True)
        return jnp.where(cnt >= k_top, cand, t)

    t0 = jnp.full((score.shape[0], 1), INT_MIN, jnp.int32)
    t = lax.fori_loop(0, 32, body, t0, unroll=KTH_SEARCH_UNROLL)
    return _order_key_to_float(jnp.maximum(t, ORDER_KEY_NEG_INF))


def _topk_select(score, k_top, tri):
    thr = _kth_largest(score, k_top)
    above = score > thr
    need = k_top - jnp.sum(above.astype(F32), axis=1, keepdims=True)
    tie = score == thr
    pieces = []
    run = jnp.zeros_like(need)
    for j in range(score.shape[1] // LANES):
        cols = slice(j * LANES, (j + 1) * LANES)
        tie_j = tie[:, cols]
        prefix = _dot(tie_j.astype(BF16), tri) + run
        run = prefix[:, LANES - 1:LANES]
        pieces.append(jnp.where(above[:, cols] | (tie_j & (prefix <= need)), 1.0, 0.0))
    return jnp.concatenate(pieces, axis=1)


def _tri_matrix():
    r = lax.broadcasted_iota(jnp.int32, (LANES, LANES), 0)
    c = lax.broadcasted_iota(jnp.int32, (LANES, LANES), 1)
    return (r <= c).astype(BF16)


PA_TQ = 128


PA_EXTENTS = 8


def _prompt_attn_kernel(k_top, qi_ref, wi_ref, kib_ref, q_ref, kb_ref, vbb_ref, o_ref):
    i = pl.program_id(1)
    step = kb_ref.shape[0] // PA_EXTENTS
    blocks_per_step = step // PA_TQ
    for c in range(PA_EXTENTS):
        @pl.when(i // blocks_per_step == c)
        def _():
            _prompt_attn_block(k_top, (c + 1) * step, i, qi_ref, wi_ref, kib_ref, q_ref, kb_ref, vbb_ref, o_ref)


def _prompt_attn_block(k_top, seq, i, qi_ref, wi_ref, kib_ref, q_ref, kb_ref, vbb_ref, o_ref):
    qi = qi_ref[...]
    wi = wi_ref[...]
    ki2 = kib_ref[:seq, :]
    lane = lax.broadcasted_iota(jnp.int32, (PA_TQ, LANES), 1)
    score = jnp.zeros((PA_TQ, seq), F32)
    for h in range(N_IDX_HEADS):
        pair = qi[:, (h // 2) * LANES:(h // 2 + 1) * LANES]
        keep = (lane < IDX_DIM) if h % 2 == 0 else (lane >= IDX_DIM)
        d = _dot_nt(jnp.where(keep, pair, jnp.zeros_like(pair)), ki2)
        score = score + wi[:, h:h + 1] * jnp.maximum(d, 0.0)
    q_pos = i * PA_TQ + lax.broadcasted_iota(jnp.int32, (PA_TQ, seq), 0)
    s_pos = lax.broadcasted_iota(jnp.int32, (PA_TQ, seq), 1)
    causal = s_pos <= q_pos
    sel = _topk_select(jnp.where(causal, score, NEG_INF), k_top, _tri_matrix())
    bias = jnp.where(causal & (sel > 0.0), 0.0, NEG_INF)

    q = q_ref[...]
    for g in range(N_KV_B):
        q4 = jnp.concatenate(
            [q[:, (g * KV_GROUP + r) * HEAD_DIM:(g * KV_GROUP + r + 1) * HEAD_DIM] for r in range(KV_GROUP)], axis=0)
        kg = kb_ref[:seq, g * HEAD_DIM:(g + 1) * HEAD_DIM]
        vg = vbb_ref[:seq, g * HEAD_DIM:(g + 1) * HEAD_DIM]
        s = _dot_nt(q4, kg).reshape(KV_GROUP, PA_TQ, seq) + bias[None]
        m = jnp.max(s, axis=-1, keepdims=True)
        p = jnp.exp(s - m)
        l = jnp.sum(p, axis=-1, keepdims=True)
        o = _dot(p.reshape(KV_GROUP * PA_TQ, seq).astype(BF16), vg).reshape(KV_GROUP, PA_TQ, HEAD_DIM) / l
        for r in range(KV_GROUP):
            o_ref[:, (g * KV_GROUP + r) * HEAD_DIM:(g * KV_GROUP + r + 1) * HEAD_DIM] = o[r].astype(BF16)


def _prompt_attn(n_batch, seq, k_top, qi, wi, kib, q, kb, vbb):
    nq = seq // PA_TQ
    qrow = lambda b, i: (b * nq + i, 0)
    krow = lambda b, i: (b, 0)
    return pl.pallas_call(
        functools.partial(_prompt_attn_kernel, k_top),
        grid=(n_batch, nq),
        in_specs=[
            pl.BlockSpec((PA_TQ, N_IDX_HEADS * IDX_DIM), qrow),
            pl.BlockSpec((PA_TQ, N_IDX_HEADS), qrow),
            pl.BlockSpec((seq, LANES), krow),
            pl.BlockSpec((PA_TQ, D_B), qrow),
            pl.BlockSpec((seq, D_KV_B), krow),
            pl.BlockSpec((seq, D_KV_B), krow),
        ],
        out_specs=pl.BlockSpec((PA_TQ, D_B), qrow),
        out_shape=jax.ShapeDtypeStruct((n_batch * seq, D_B), BF16),
        compiler_params=_params(("parallel", "parallel")),
        name="prompt_attn",
    )(qi, wi, kib, q, kb, vbb)


SP_PAGES = 16
SP_KEYS = SP_PAGES * PAGE_SIZE
SP_TPAD = SUBLANES


def _page_specs(page_shape, n_steps, layer):
    def make(p):
        def index(b, j, pt):
            return (layer, pt[b, jnp.minimum(j, n_steps - 1) * SP_PAGES + p], 0, 0)
        return pl.BlockSpec((None, None) + page_shape, index)
    return [make(p) for p in range(SP_PAGES)]


def _sample_scores_kernel(n_steps, pt_ref, qs_ref, ws_ref, kin_ref, *refs):
    page_refs, o_ref = refs[:SP_PAGES], refs[SP_PAGES]
    j = pl.program_id(1)
    qs = qs_ref[...]
    ws = ws_ref[...]

    def head_sum(d):
        w = ws * jnp.maximum(d, 0.0)
        return jnp.sum(w.reshape(N_IDX_HEADS, SP_TPAD, d.shape[1]), axis=0)

    @pl.when(j < n_steps)
    def _():
        keys_t = jnp.concatenate([r[...] for r in page_refs], axis=1).astype(BF16)
        o_ref[...] = head_sum(_dot(qs, keys_t))

    @pl.when(j == n_steps)
    def _():
        d = _dot_nt(qs, kin_ref[...])
        o_ref[...] = jnp.concatenate(
            [head_sum(d), jnp.zeros((SP_TPAD, SP_KEYS - d.shape[1]), F32)], axis=1)


def _sample_scores(page_table, qs, ws, ki_new, cache_idx_k, layer):
    n_db, n_pages = page_table.shape
    n_steps = n_pages // SP_PAGES
    rows = N_IDX_HEADS * SP_TPAD
    per_b = lambda b, j, pt: (b, 0, 0)
    grid_spec = pltpu.PrefetchScalarGridSpec(
        num_scalar_prefetch=1,
        grid=(n_db, n_steps + 1),
        in_specs=[
            pl.BlockSpec((None, rows, IDX_DIM), per_b),
            pl.BlockSpec((None, rows, 1), per_b),
            pl.BlockSpec((None, LANES, IDX_DIM), per_b),
        ] + _page_specs((IDX_DIM, PAGE_SIZE), n_steps, layer),
        out_specs=pl.BlockSpec((None, SP_TPAD, SP_KEYS), lambda b, j, pt: (b, 0, j)),
    )
    return pl.pallas_call(
        functools.partial(_sample_scores_kernel, n_steps),
        grid_spec=grid_spec,
        out_shape=jax.ShapeDtypeStruct((n_db, SP_TPAD, (n_steps + 1) * SP_KEYS), F32),
        compiler_params=_params(("parallel", "arbitrary")),
        name="sample_scores",
    )(page_table, qs, ws, ki_new, *([cache_idx_k] * SP_PAGES))


SM_ROWS = 64


def _sample_mask_kernel(k_top, past, n_new, s_ref, o_ref):
    score = s_ref[...]
    t = lax.broadcasted_iota(jnp.int32, score.shape, 0) % SP_TPAD
    s_pos = lax.broadcasted_iota(jnp.int32, score.shape, 1)
    admissible = s_pos <= past + t
    sel = _topk_select(jnp.where(admissible, score, NEG_INF), k_top, _tri_matrix())
    keep = jnp.where((admissible & (sel > 0.0)) | (t >= n_new), 1.0, 0.0).astype(BF16)
    r = lax.broadcasted_iota(jnp.int32, (LANES, N_KV_B * LANES), 0)
    c = lax.broadcasted_iota(jnp.int32, (LANES, N_KV_B * LANES), 1)
    spread = (c // N_KV_B == r).astype(BF16)
    for j in range(score.shape[1] // LANES):
        d = _dot(keep[:, j * LANES:(j + 1) * LANES], spread)
        o_ref[:, j * N_KV_B * LANES:(j + 1) * N_KV_B * LANES] = jnp.where(d > 0.5, 0.0, NEG_INF)


def _sample_mask(scores2d, k_top, past, n_new):
    n_rows, width = scores2d.shape
    return pl.pallas_call(
        functools.partial(_sample_mask_kernel, k_top, past, n_new),
        grid=(n_rows // SM_ROWS,),
        in_specs=[pl.BlockSpec((SM_ROWS, width), lambda i: (i, 0))],
        out_specs=pl.BlockSpec((SM_ROWS, N_KV_B * width), lambda i: (i, 0)),
        out_shape=jax.ShapeDtypeStruct((n_rows, N_KV_B * width), F32),
        compiler_params=_params(("parallel",)),
        name="sample_mask",
    )(scores2d)


SA_ROWS = KV_GROUP * SP_TPAD
SA_ROWS_ALL = N_KV_B * SA_ROWS
SA_PAGE_ROWS = PAGE_SIZE * N_KV_B


def _sample_attn_kernel(n_steps, pt_ref, q_ref, bias_ref, kn_ref, vn_ref, *refs):
    k_refs, v_refs = refs[:SP_PAGES], refs[SP_PAGES:2 * SP_PAGES]
    o_ref, m_ref, l_ref, acc_ref = refs[2 * SP_PAGES:]
    j = pl.program_id(1)

    @pl.when(j == 0)
    def _():
        m_ref[...] = jnp.full(m_ref.shape, NEG_INF, F32)
        l_ref[...] = jnp.zeros(l_ref.shape, F32)
        acc_ref[...] = jnp.zeros(acc_ref.shape, F32)

    def update(kc, vc, bias):
        n = kc.shape[0]
        s = _dot_nt(q_ref[...], kc)
        row_head = lax.broadcasted_iota(jnp.int32, (SA_ROWS_ALL, n), 0) // SA_ROWS
        col_head = lax.broadcasted_iota(jnp.int32, (SA_ROWS_ALL, n), 1) % N_KV_B
        s = jnp.where(row_head == col_head, s + jnp.concatenate([bias] * (SA_ROWS_ALL // SP_TPAD), axis=0), NEG_INF)
        m_old = m_ref[...]
        m_new = jnp.maximum(m_old, jnp.max(s, axis=-1, keepdims=True))
        m_safe = jnp.where(m_new == NEG_INF, 0.0, m_new)
        alpha = jnp.exp(m_old - m_safe)
        p = jnp.exp(s - m_safe)
        l_ref[...] = alpha * l_ref[...] + jnp.sum(p, axis=-1, keepdims=True)
        acc_ref[...] = alpha * acc_ref[...] + _dot(p.astype(BF16), vc)
        m_ref[...] = m_new

    @pl.when(j < n_steps)
    def _():
        kc = jnp.concatenate([r[...] for r in k_refs], axis=0).astype(BF16)
        vc = jnp.concatenate([r[...] for r in v_refs], axis=0).astype(BF16)
        update(kc, vc, bias_ref[...])

    @pl.when(j == n_steps)
    def _():
        update(kn_ref[...], vn_ref[...], bias_ref[:, :SA_PAGE_ROWS])
        o_ref[...] = acc_ref[...] / l_ref[...]


def _sample_attn(page_table, qs, bias, k_new, v_new, cache_k, cache_v, layer):
    n_db, n_pages = page_table.shape
    n_steps = n_pages // SP_PAGES
    per_b = lambda b, j, pt: (b, 0, 0)
    grid_spec = pltpu.PrefetchScalarGridSpec(
        num_scalar_prefetch=1,
        grid=(n_db, n_steps + 1),
        in_specs=[
            pl.BlockSpec((None, SA_ROWS_ALL, HEAD_DIM), per_b),
            pl.BlockSpec((None, SP_TPAD, N_KV_B * SP_KEYS), lambda b, j, pt: (b, 0, j)),
            pl.BlockSpec((None, SA_PAGE_ROWS, HEAD_DIM), per_b),
            pl.BlockSpec((None, SA_PAGE_ROWS, HEAD_DIM), per_b),
        ] + 2 * _page_specs((SA_PAGE_ROWS, HEAD_DIM), n_steps, layer),
        out_specs=pl.BlockSpec((None, SA_ROWS_ALL, HEAD_DIM), per_b),
        scratch_shapes=[
            pltpu.VMEM((SA_ROWS_ALL, 1), F32),
            pltpu.VMEM((SA_ROWS_ALL, 1), F32),
            pltpu.VMEM((SA_ROWS_ALL, HEAD_DIM), F32),
        ],
    )
    return pl.pallas_call(
        functools.partial(_sample_attn_kernel, n_steps),
        grid_spec=grid_spec,
        out_shape=jax.ShapeDtypeStruct((n_db, SA_ROWS_ALL, HEAD_DIM), F32),
        compiler_params=_params(("parallel", "arbitrary")),
        name="sample_attn",
    )(page_table, qs, bias, k_new, v_new, *([cache_k] * SP_PAGES), *([cache_v] * SP_PAGES))


OR_TM = 256


def _split_bf16(x):
    hi = x.astype(BF16)
    return hi, (x - hi.astype(F32)).astype(BF16)


def _out_router_kernel(n_prompt_tiles, a_ref, b_ref, xp_ref, xs_ref, wo_ref, nf_ref, wrh_ref, wrl_ref, br_ref,
                       xmid_ref, h_ref, idx_ref, gate_ref):
    x = _group_rows(n_prompt_tiles, xp_ref, xs_ref)
    x_mid = x + _dot(a_ref[...], wo_ref[:D_A, :]) + _dot(b_ref[...], wo_ref[D_A:, :])
    xmid_ref[...] = x_mid
    h = (x_mid * lax.rsqrt(jnp.mean(x_mid * x_mid, axis=-1, keepdims=True) + RMS_EPS)) * nf_ref[...]
    h_ref[...] = h
    h_hi, h_lo = _split_bf16(h)
    wr_hi = wrh_ref[...]
    logits = _dot(h_hi, wr_hi) + _dot(h_lo, wr_hi) + _dot(h_hi, wrl_ref[...]) + br_ref[...]
    lane = lax.broadcasted_iota(jnp.int32, logits.shape, 1)
    vals, idxs = [], []
    for _ in range(TOP_K_EXPERTS):
        m = jnp.max(logits, axis=-1, keepdims=True)
        i = jnp.min(jnp.where(logits == m, lane, LANES), axis=-1, keepdims=True)
        vals.append(m)
        idxs.append(i)
        logits = jnp.where(lane == i, NEG_INF, logits)
    exps = [jnp.exp(v - vals[0]) for v in vals]
    total = exps[0]
    for e in exps[1:]:
        total = total + e
    idx_out = jnp.zeros(logits.shape, jnp.int32)
    gate_out = jnp.zeros(logits.shape, F32)
    for k in range(TOP_K_EXPERTS):
        idx_out = jnp.where(lane == k, idxs[k], idx_out)
        gate_out = jnp.where(lane == k, exps[k] / total, gate_out)
    idx_ref[...] = idx_out
    gate_ref[...] = gate_out


def _out_router(a_out, b_out, x_p, x_s, w_out_b, norm_ffn, wr_hi, wr_lo, br_pad):
    d_model = x_p.shape[1]
    n_prompt_tiles = x_p.shape[0] // OR_TM
    n_rows = x_p.shape[0] + x_s.shape[0]
    row = lambda i: (i, 0)
    const = lambda i: (0, 0)
    return pl.pallas_call(
        functools.partial(_out_router_kernel, n_prompt_tiles),
        grid=(n_rows // OR_TM,),
        in_specs=[
            pl.BlockSpec((OR_TM, D_A), row),
            pl.BlockSpec((OR_TM, D_B), row),
        ] + _group_specs(n_prompt_tiles, OR_TM, d_model) + [
            pl.BlockSpec((D_A + D_B, d_model), const),
            pl.BlockSpec((1, d_model), const),
            pl.BlockSpec((d_model, LANES), const),
            pl.BlockSpec((d_model, LANES), const),
            pl.BlockSpec((1, LANES), const),
        ],
        out_specs=[
            pl.BlockSpec((OR_TM, d_model), row),
            pl.BlockSpec((OR_TM, d_model), row),
            pl.BlockSpec((OR_TM, LANES), row),
            pl.BlockSpec((OR_TM, LANES), row),
        ],
        out_shape=[
            jax.ShapeDtypeStruct((n_rows, d_model), F32),
            jax.ShapeDtypeStruct((n_rows, d_model), F32),
            jax.ShapeDtypeStruct((n_rows, LANES), jnp.int32),
            jax.ShapeDtypeStruct((n_rows, LANES), F32),
        ],
        compiler_params=_params(("parallel",)),
        name="out_router",
    )(a_out, b_out, x_p, x_s, w_out_b, norm_ffn, wr_hi, wr_lo, br_pad)


MOE_TM = 256
MOE_UNIT_TILES = 5
MOE_R = MOE_TM * MOE_UNIT_TILES
MOE_FULL_PAIRS = MOE_UNIT_TILES // 2
MOE_TF = 256
COMBINE_TM = 128
ROW_COPY_UNROLL = 8


def _row_copy(src_ref, src_row, dst_ref, dst_row, sem):
    return pltpu.make_async_copy(src_ref.at[pl.ds(src_row, 1), :], dst_ref.at[pl.ds(dst_row, 1), :], sem)


def _tile_rows(rows):
    return pl.cdiv(rows, MOE_TM) * MOE_TM


def _spreads_gather(rows):
    return pl.cdiv(rows, MOE_TM) // 2 == MOE_FULL_PAIRS


def _moe_kernel(n_f, ue_ref, ur_ref, ub_ref, tok_ref, tokn_ref, h_ref, wg_ref, wu_ref, bg_ref, bu_ref, wd_ref,
                bd_ref, ys_ref, land_ref, xb_ref, y_ref, wgb_ref, wub_ref, wdb_ref, gsem, ysem):
    u, f = pl.program_id(0), pl.program_id(1)
    n_units = pl.num_programs(0)
    rows = ur_ref[u]
    n_tiles = pl.cdiv(rows, MOE_TM)
    rows_next = jnp.where(u + 1 < n_units, ur_ref[jnp.minimum(u + 1, n_units - 1)], 0)
    spread = _spreads_gather(rows) & (rows_next > 0)
    gather_chunk = MOE_R // (n_f * MOE_FULL_PAIRS)

    def gather(tokens_ref, n_rows, wait):
        def tile(m, c):
            def body(j, c2):
                r = m * MOE_TM + j
                cp = _row_copy(h_ref, tokens_ref[0, r], land_ref, r, gsem)
                cp.wait() if wait else cp.start()
                return c2
            return lax.fori_loop(0, MOE_TM, body, c, unroll=ROW_COPY_UNROLL)
        lax.fori_loop(0, n_rows // MOE_TM, tile, 0)

    def writeback(unit, n, wait):
        def body(m, c):
            r0 = pl.multiple_of(m * MOE_TM, MOE_TM)
            cp = pltpu.make_async_copy(y_ref.at[pl.ds(r0, MOE_TM), :],
                                       ys_ref.at[pl.ds(ub_ref[unit] * MOE_R + r0, MOE_TM), :], ysem)
            cp.wait() if wait else cp.start()
            return c
        lax.fori_loop(0, n, body, 0)

    @pl.when((f == 0) & (rows > 0))
    def _():
        @pl.when(u == 0)
        def _():
            gather(tok_ref, _tile_rows(rows), wait=False)

        by_spread = (u > 0) & _spreads_gather(ur_ref[jnp.maximum(u - 1, 0)])
        gather(tok_ref, jnp.where(by_spread, MOE_R, _tile_rows(rows)), wait=True)

        def convert(m, c):
            r0 = pl.multiple_of(m * MOE_TM, MOE_TM)
            xb_ref[pl.ds(r0, MOE_TM), :] = land_ref[pl.ds(r0, MOE_TM), :].astype(BF16)
            return c
        lax.fori_loop(0, n_tiles, convert, 0)

        @pl.when(jnp.logical_not(spread))
        def _():
            gather(tokn_ref, _tile_rows(rows_next), wait=False)

    @pl.when((f == 0) & (u > 0))
    def _():
        writeback(u - 1, pl.cdiv(ur_ref[jnp.maximum(u - 1, 0)], MOE_TM), wait=True)

    @pl.when(rows > 0)
    def _():
        wgb_ref[...] = wg_ref[...].astype(BF16)
        wub_ref[...] = wu_ref[...].astype(BF16)
        wdb_ref[...] = wd_ref[...].astype(BF16)

        def tile(m, c):
            r0 = pl.multiple_of(m * MOE_TM, MOE_TM)
            xm = xb_ref[pl.ds(r0, MOE_TM), :]
            gate = jnp.minimum(_dot(xm, wgb_ref[...]) + bg_ref[...], SWIGLU_LIMIT)
            up = jnp.clip(_dot(xm, wub_ref[...]) + bu_ref[...], -SWIGLU_LIMIT, SWIGLU_LIMIT)
            act = (up + 1.0) * (gate * jax.nn.sigmoid(SWIGLU_ALPHA * gate))
            y_ref[pl.ds(r0, MOE_TM), :] += _dot(act.astype(BF16), wdb_ref[...])
            return c

        @pl.when(f == 0)
        def _():
            def init(m, c):
                r0 = pl.multiple_of(m * MOE_TM, MOE_TM)
                y_ref[pl.ds(r0, MOE_TM), :] = jnp.broadcast_to(bd_ref[...], (MOE_TM, y_ref.shape[1]))
                return c
            lax.fori_loop(0, n_tiles, init, 0)

        def tile_pair(issue, p, c):
            if issue:
                base = (f * MOE_FULL_PAIRS + p) * gather_chunk
                for j in range(gather_chunk):
                    _row_copy(h_ref, tokn_ref[0, base + j], land_ref, base + j, gsem).start()
            tile(2 * p, c)
            return tile(2 * p + 1, c)

        @pl.when(spread)
        def _():
            lax.fori_loop(0, MOE_FULL_PAIRS, functools.partial(tile_pair, True), 0)

        @pl.when(jnp.logical_not(spread))
        def _():
            lax.fori_loop(0, n_tiles // 2, functools.partial(tile_pair, False), 0)

        @pl.when(n_tiles % 2 == 1)
        def _():
            tile(n_tiles - 1, 0)

        @pl.when(f == n_f - 1)
        def _():
            writeback(u, n_tiles, wait=False)

    @pl.when((f == n_f - 1) & (u == n_units - 1))
    def _():
        writeback(u, n_tiles, wait=True)


def _moe_experts(unit_expert, unit_rows, unit_block, slot_token, h, w_gate_up, b_gate_up, w_down, b_down):
    n_units = unit_expert.shape[0]
    d_model = h.shape[1]
    d_ff = w_down.shape[1]
    n_f = d_ff // MOE_TF

    def f_eff(u, f, ur):
        return jnp.where(ur[u] > 0, f, n_f - 1)

    smem_tokens = lambda index: pl.BlockSpec((None, 1, MOE_R), index, memory_space=pltpu.SMEM)
    grid_spec = pltpu.PrefetchScalarGridSpec(
        num_scalar_prefetch=3,
        grid=(n_units, n_f),
        in_specs=[
            smem_tokens(lambda u, f, ue, ur, ub: (ub[u], 0, 0)),
            smem_tokens(lambda u, f, ue, ur, ub: (ub[jnp.minimum(u + 1, n_units - 1)], 0, 0)),
            pl.BlockSpec(memory_space=pl.ANY),
            pl.BlockSpec((None, d_model, MOE_TF), lambda u, f, ue, ur, ub: (ue[u], 0, f_eff(u, f, ur))),
            pl.BlockSpec((None, d_model, MOE_TF), lambda u, f, ue, ur, ub: (ue[u], 0, n_f + f_eff(u, f, ur))),
            pl.BlockSpec((None, 1, MOE_TF), lambda u, f, ue, ur, ub: (ue[u], 0, f_eff(u, f, ur))),
            pl.BlockSpec((None, 1, MOE_TF), lambda u, f, ue, ur, ub: (ue[u], 0, n_f + f_eff(u, f, ur))),
            pl.BlockSpec((None, MOE_TF, d_model), lambda u, f, ue, ur, ub: (ue[u], f_eff(u, f, ur), 0)),
            pl.BlockSpec((None, 1, d_model), lambda u, f, ue, ur, ub: (ue[u], 0, 0)),
        ],
        out_specs=pl.BlockSpec(memory_space=pl.ANY),
        scratch_shapes=[
            pltpu.VMEM((MOE_R, d_model), F32),
            pltpu.VMEM((MOE_R, d_model), BF16),
            pltpu.VMEM((MOE_R, d_model), F32),
            pltpu.VMEM((d_model, MOE_TF), BF16),
            pltpu.VMEM((d_model, MOE_TF), BF16),
            pltpu.VMEM((MOE_TF, d_model), BF16),
            pltpu.SemaphoreType.DMA,
            pltpu.SemaphoreType.DMA,
        ],
    )
    tokens = slot_token.reshape(n_units, 1, MOE_R)
    assert MOE_R % (n_f * MOE_FULL_PAIRS) == 0
    return pl.pallas_call(
        functools.partial(_moe_kernel, n_f),
        grid_spec=grid_spec,
        out_shape=jax.ShapeDtypeStruct((n_units * MOE_R, d_model), F32),
        compiler_params=_params(("arbitrary", "arbitrary")),
        name="moe_experts",
    )(unit_expert, unit_rows, unit_block, tokens, tokens, h, w_gate_up, w_gate_up, b_gate_up, b_gate_up,
      w_down, b_down)


def _combine_kernel(n_prompt_tiles, pos_ref, posn_ref, gate_ref, xmid_ref, nfin_ref, ys_ref, op_ref, os_ref,
                    buf_ref, sem):
    i, n_tiles = pl.program_id(0), pl.num_programs(0)
    slot = i % 2

    def rows(p_ref, s, wait):
        for k in range(TOP_K_EXPERTS):
            def body(t, c):
                cp = _row_copy(ys_ref, p_ref[0, t * TOP_K_EXPERTS + k], buf_ref.at[s], k * COMBINE_TM + t,
                               sem.at[s])
                cp.wait() if wait else cp.start()
                return c
            lax.fori_loop(0, COMBINE_TM, body, 0, unroll=ROW_COPY_UNROLL)

    @pl.when(i == 0)
    def _():
        rows(pos_ref, 0, wait=False)

    @pl.when(i + 1 < n_tiles)
    def _():
        rows(posn_ref, 1 - slot, wait=False)

    rows(pos_ref, slot, wait=True)
    gates = gate_ref[...]
    out = xmid_ref[...]
    for k in range(TOP_K_EXPERTS):
        out = out + gates[:, k:k + 1] * buf_ref[slot, k * COMBINE_TM:(k + 1) * COMBINE_TM, :]
    y = (out * lax.rsqrt(jnp.mean(out * out, axis=-1, keepdims=True) + RMS_EPS)) * nfin_ref[...]

    @pl.when(i < n_prompt_tiles)
    def _():
        op_ref[...] = y

    @pl.when(i >= n_prompt_tiles)
    def _():
        os_ref[...] = y


def _combine(pos, gates, x_mid, norm_final, ys, n_prompt, n_sample):
    d_model = x_mid.shape[1]
    n_prompt_tiles = n_prompt // COMBINE_TM
    n_tiles = n_prompt_tiles + n_sample // COMBINE_TM
    n = COMBINE_TM * TOP_K_EXPERTS
    pos3 = pos.reshape(n_tiles, 1, n)
    row = lambda i: (i, 0)
    return pl.pallas_call(
        functools.partial(_combine_kernel, n_prompt_tiles),
        grid=(n_tiles,),
        in_specs=[
            pl.BlockSpec((None, 1, n), lambda i: (i, 0, 0), memory_space=pltpu.SMEM),
            pl.BlockSpec((None, 1, n), lambda i: (jnp.minimum(i + 1, n_tiles - 1), 0, 0), memory_space=pltpu.SMEM),
            pl.BlockSpec((COMBINE_TM, LANES), row),
            pl.BlockSpec((COMBINE_TM, d_model), row),
            pl.BlockSpec((1, d_model), lambda i: (0, 0)),
            pl.BlockSpec(memory_space=pl.ANY),
        ],
        out_specs=[
            pl.BlockSpec((COMBINE_TM, d_model), lambda i: (jnp.minimum(i, n_prompt_tiles - 1), 0)),
            pl.BlockSpec((COMBINE_TM, d_model), lambda i: (jnp.maximum(i - n_prompt_tiles, 0), 0)),
        ],
        out_shape=[jax.ShapeDtypeStruct((n_prompt, d_model), F32), jax.ShapeDtypeStruct((n_sample, d_model), F32)],
        scratch_shapes=[pltpu.VMEM((2, n, d_model), F32), pltpu.SemaphoreType.DMA((2,))],
        compiler_params=_params(("arbitrary",)),
        name="combine",
    )(pos3, pos3, gates, x_mid, norm_final, ys)


def _routing(top_idx, n_units, zero_row):
    flat_e = top_idx.reshape(-1)
    onehot = (flat_e[:, None] == jnp.arange(N_EXPERTS, dtype=jnp.int32)[None, :]).astype(jnp.int32)
    csum = jnp.cumsum(onehot, axis=0)
    rank = jnp.take_along_axis(csum, flat_e[:, None], axis=1)[:, 0] - 1
    counts = csum[-1]
    units_e = (counts + MOE_R - 1) // MOE_R
    units_end = jnp.cumsum(units_e)
    units_start = units_end - units_e
    pos = (units_start[flat_e] + rank // MOE_R) * MOE_R + rank % MOE_R
    u = jnp.arange(n_units, dtype=jnp.int32)
    total = units_end[-1]
    ue = jnp.minimum(jnp.sum((units_end[None, :] <= u[:, None]).astype(jnp.int32), axis=1), N_EXPERTS - 1)
    rows = jnp.clip(counts[ue] - (u - units_start[ue]) * MOE_R, 0, MOE_R)
    used = u < total
    last_e = ue[jnp.maximum(total - 1, 0)]
    unit_expert = jnp.where(used, ue, last_e).astype(jnp.int32)
    unit_rows = jnp.where(used, rows, 0).astype(jnp.int32)
    unit_block = jnp.minimum(u, total - 1).astype(jnp.int32)
    pos = pos.astype(jnp.int32)
    token = jnp.arange(flat_e.shape[0], dtype=jnp.int32) // TOP_K_EXPERTS
    slot_token = jnp.full((n_units * MOE_R,), zero_row, jnp.int32).at[pos].set(token, unique_indices=True)
    return pos, slot_token, unit_expert, unit_rows, unit_block


def kernel(x_prompt, x_sample, cache_k, cache_v, cache_idx_k, page_table, norm_mix, w_in, norm_v_a,
           w_spatial, b_spatial, w_out, norm_ffn, w_router, b_router, w_gate_up, b_gate_up, w_down,
           b_down, norm_final):
    n_b, seq, d_model = x_prompt.shape
    n_db, n_new, _ = x_sample.shape
    depth = w_in.shape[0]
    past = page_table.shape[1] * PAGE_SIZE
    n_prompt, n_sample = n_b * seq, n_db * n_new
    n_tokens = n_prompt + n_sample
    assert seq % IN_TM == 0 and seq % PA_TQ == 0 and CHUNK % n_new == 0 and n_new <= SP_TPAD
    assert n_sample % CHUNK == 0 and page_table.shape[1] % SP_PAGES == 0
    assert n_prompt % COMBINE_TM == 0 and n_sample % COMBINE_TM == 0
    assert depth == 1, "the final norm is fused into the last layer's combine; deeper stacks are not wired up"
    n_rows = (n_tokens // IN_TM + 1) * IN_TM
    k_top_prompt = min(TOPK_MAX, seq // 4)
    k_top_sample = min(TOPK_MAX, (past + n_new) // 4)
    n_assign = n_tokens * TOP_K_EXPERTS
    n_units = N_EXPERTS + n_assign // MOE_R

    pos_rows = jnp.concatenate([jnp.arange(seq, dtype=jnp.int32),
                                past + jnp.arange(IN_TM, dtype=jnp.int32) % n_new])
    tables = _rope_tables(pos_rows, ROT_DIM, HEAD_DIM) + _rope_tables(pos_rows, IDX_ROT_DIM, IDX_DIM)

    x_p = x_prompt.reshape(n_prompt, d_model)
    x_s = jnp.pad(x_sample.reshape(n_sample, d_model), ((0, n_rows - n_tokens), (0, 0)))
    outs = []
    for l in range(depth):
        causal = jnp.tril(jnp.ones((CHUNK, CHUNK), bool))
        w_prompt = jnp.where(causal[None], w_spatial[l], 0.0)
        eye = jnp.eye(CHUNK // n_new, dtype=F32)
        w_decode = jax.vmap(lambda w: jnp.kron(eye, w))(w_prompt[:, :n_new, :n_new])
        wmix = jnp.stack([w_prompt, w_decode]).astype(BF16)
        b_decode = jnp.tile(b_spatial[l][:, :n_new], (1, CHUNK // n_new))
        bmix = jnp.stack([b_spatial[l].T, b_decode.T])
        w_in_b = jnp.pad(w_in[l], ((0, 0), (0, IN_COLS_PAD - IN_COLS))).astype(BF16)

        (a_out, q, k, kb, vb, vbb, qi, ki, kib, wi, va) = _in_proj(
            x_p, x_s, seq // IN_TM, norm_mix[l][None], w_in_b, norm_v_a[l][None], wmix, bmix, tables)

        b_prompt = _prompt_attn(n_b, seq, k_top_prompt, qi, wi, kib, q, kb, vbb)

        sl = slice(n_prompt, n_tokens)
        pad_t = ((0, 0), (0, 0), (0, SP_TPAD - n_new), (0, 0))
        qi_s = qi[sl].reshape(n_db, n_new, N_IDX_HEADS, IDX_DIM).transpose(0, 2, 1, 3)
        qs_idx = jnp.pad(qi_s, pad_t).reshape(n_db, N_IDX_HEADS * SP_TPAD, IDX_DIM)
        wi_s = wi[sl].reshape(n_db, n_new, N_IDX_HEADS).transpose(0, 2, 1)
        ws_idx = jnp.pad(wi_s, ((0, 0), (0, 0), (0, SP_TPAD - n_new))).reshape(n_db, N_IDX_HEADS * SP_TPAD, 1)
        pad_rows = ((0, 0), (0, LANES - n_new), (0, 0))
        ki_new = jnp.pad(kib[sl, :IDX_DIM].reshape(n_db, n_new, IDX_DIM), pad_rows)
        scores = _sample_scores(page_table, qs_idx, ws_idx, ki_new, jnp.swapaxes(cache_idx_k, 2, 3), l)
        width = scores.shape[-1]
        bias = _sample_mask(scores.reshape(n_db * SP_TPAD, width), k_top_sample, past, n_new)
        q_s = q[sl].reshape(n_db, n_new, N_KV_B, KV_GROUP, HEAD_DIM).transpose(0, 2, 3, 1, 4)
        qs_att = jnp.pad(q_s, ((0, 0), (0, 0), (0, 0), (0, SP_TPAD - n_new), (0, 0))).reshape(
            n_db, SA_ROWS_ALL, HEAD_DIM)
        pad_kv = ((0, 0), (0, SA_PAGE_ROWS - n_new * N_KV_B), (0, 0))
        k_new = jnp.pad(kb[sl].reshape(n_db, n_new * N_KV_B, HEAD_DIM), pad_kv)
        v_new = jnp.pad(vbb[sl].reshape(n_db, n_new * N_KV_B, HEAD_DIM), pad_kv)
        n_pool = cache_k.shape[1]
        o_s = _sample_attn(page_table, qs_att, bias.reshape(n_db, SP_TPAD, N_KV_B * width), k_new, v_new,
                           cache_k.reshape(depth, n_pool, SA_PAGE_ROWS, HEAD_DIM),
                           cache_v.reshape(depth, n_pool, SA_PAGE_ROWS, HEAD_DIM), l)
        b_sample = o_s.reshape(n_db, N_KV_B, KV_GROUP, SP_TPAD, HEAD_DIM)[:, :, :, :n_new]
        b_sample = b_sample.transpose(0, 3, 1, 2, 4).reshape(n_sample, D_B).astype(BF16)
        b_out = jnp.concatenate([b_prompt, b_sample, jnp.zeros((n_rows - n_tokens, D_B), BF16)])

        wr_pad = jnp.pad(w_router[l], ((0, 0), (0, LANES - N_EXPERTS)))
        wr_hi, wr_lo = _split_bf16(wr_pad)
        br_pad = jnp.pad(b_router[l], (0, LANES - N_EXPERTS), constant_values=NEG_INF)[None]
        x_mid, h, top_idx, gates = _out_router(a_out, b_out, x_p, x_s, w_out[l].astype(BF16),
                                               norm_ffn[l][None], wr_hi, wr_lo, br_pad)

        pos, slot_token, unit_expert, unit_rows, unit_block = _routing(
            top_idx[:n_tokens, :TOP_K_EXPERTS], n_units, zero_row=n_tokens)
        ys = _moe_experts(unit_expert, unit_rows, unit_block, slot_token, h, w_gate_up[l],
                          b_gate_up[l][:, None, :], w_down[l], b_down[l][:, None, :])
        y_p, y_s = _combine(pos, gates, x_mid, norm_final[None], ys, n_prompt, n_sample)
        outs.append((k, vb, ki, va))

    def stack(j, lo, hi, shape):
        return jnp.stack([o[j][lo:hi].reshape(shape) for o in outs])

    return (
        y_p.reshape(n_b, seq, d_model), y_s.reshape(n_db, n_new, d_model),
        stack(0, 0, n_prompt, (n_b, seq, N_KV_B, HEAD_DIM)),
        stack(1, 0, n_prompt, (n_b, seq, N_KV_B, HEAD_DIM)),
        stack(2, 0, n_prompt, (n_b, seq, IDX_DIM)),
        stack(0, n_prompt, n_tokens, (n_db, n_new, N_KV_B, HEAD_DIM)),
        stack(1, n_prompt, n_tokens, (n_db, n_new, N_KV_B, HEAD_DIM)),
        stack(2, n_prompt, n_tokens, (n_db, n_new, IDX_DIM)),
        stack(3, n_prompt, n_tokens, (n_db, n_new, N_HEADS_A, HEAD_DIM)),
    )
```

```python
import functools

import jax
import jax.numpy as jnp
from jax import lax
from jax.experimental import pallas as pl
from jax.experimental.pallas import tpu as pltpu

HEAD_DIM = 128
N_HEADS_A = 8
D_A = N_HEADS_A * HEAD_DIM
CHUNK = 128
N_HEADS_B = 8
N_KV_B = 2
KV_GROUP = N_HEADS_B // N_KV_B
D_B = N_HEADS_B * HEAD_DIM
D_KV_B = N_KV_B * HEAD_DIM
ROT_DIM = HEAD_DIM // 4
ROPE_THETA = 500000.0
N_IDX_HEADS = 16
IDX_DIM = 64
IDX_ROT_DIM = IDX_DIM // 4
TOPK_MAX = 256
N_EXPERTS = 32
TOP_K_EXPERTS = 4
SWIGLU_LIMIT = 7.0
SWIGLU_ALPHA = 1.702
RMS_EPS = 1e-5
PAGE_SIZE = 128

OFF_U = 0
OFF_V = OFF_U + D_A
OFF_Q = OFF_V + D_A
OFF_K = OFF_Q + D_B
OFF_VB = OFF_K + D_KV_B
OFF_QI = OFF_VB + D_KV_B
OFF_KI = OFF_QI + N_IDX_HEADS * IDX_DIM
OFF_WI = OFF_KI + IDX_DIM
IN_COLS = OFF_WI + N_IDX_HEADS

LANES = 128
SUBLANES = 8
VMEM_LIMIT_BYTES = 56 * 1024 * 1024

IN_COLS_PAD = pl.cdiv(IN_COLS, LANES) * LANES
INT_MIN = -(2 ** 31)
NEG_INF = float("-inf")

F32 = jnp.float32
BF16 = jnp.bfloat16


def _params(semantics):
    return pltpu.CompilerParams(dimension_semantics=semantics, vmem_limit_bytes=VMEM_LIMIT_BYTES)


def _dot(a, b):
    return jnp.dot(a, b, preferred_element_type=F32)


def _dot_nt(a, b):
    return lax.dot_general(a, b, (((1,), (1,)), ((), ())), preferred_element_type=F32)


IN_TM = 256


def _rope_lanes(z, cos, sin_lo, sin_hi, half):
    return z * cos + pltpu.roll(z, LANES - half, 1) * sin_lo + pltpu.roll(z, half, 1) * sin_hi


def _gelu(z):
    return 0.5 * z * (1.0 + lax.erf(z * (2.0 ** -0.5)))


def _group_rows(n_prompt_tiles, xp_ref, xs_ref):
    return jnp.where(pl.program_id(0) < n_prompt_tiles, xp_ref[...], xs_ref[...])


def _group_specs(n_prompt_tiles, tm, d_model):
    prompt = pl.BlockSpec((tm, d_model), lambda i: (jnp.minimum(i, n_prompt_tiles - 1), 0))
    decode = pl.BlockSpec((tm, d_model), lambda i: (jnp.maximum(i - n_prompt_tiles, 0), 0))
    return [prompt, decode]


def _in_proj_kernel(n_prompt_tiles, xp_ref, xs_ref, g_ref, w_ref, nva_ref, wmix_ref, bmix_ref,
                    cq_ref, saq_ref, sbq_ref, ci_ref, sai_ref, sbi_ref,
                    aout_ref, q_ref, k_ref, kb_ref, vb_ref, vbb_ref, qi_ref, ki_ref, kib_ref, wi_ref, va_ref):
    x = _group_rows(n_prompt_tiles, xp_ref, xs_ref)
    ms = jnp.mean(x * x, axis=-1, keepdims=True)
    xb = ((x * lax.rsqrt(ms + RMS_EPS)) * g_ref[...]).astype(BF16)

    u = _gelu(_dot(xb, w_ref[:, OFF_U:OFF_V]))
    gv = _gelu(_dot(xb, w_ref[:, OFF_V:OFF_Q]))
    va = gv * lax.rsqrt(jnp.mean(gv * gv, axis=-1, keepdims=True) + RMS_EPS) * nva_ref[...]
    va_ref[...] = va
    vab = va.astype(BF16)
    for sub in range(IN_TM // CHUNK):
        rows = slice(sub * CHUNK, (sub + 1) * CHUNK)
        for h in range(N_HEADS_A):
            cols = slice(h * HEAD_DIM, (h + 1) * HEAD_DIM)
            mixed = _dot(wmix_ref[h], vab[rows, cols]) + bmix_ref[:, h:h + 1]
            aout_ref[rows, cols] = (u[rows, cols] * mixed).astype(BF16)

    cq, saq, sbq = cq_ref[...], saq_ref[...], sbq_ref[...]
    zq = _dot(xb, w_ref[:, OFF_Q:OFF_K])
    for h in range(N_HEADS_B):
        cols = slice(h * HEAD_DIM, (h + 1) * HEAD_DIM)
        r = _rope_lanes(zq[:, cols], cq, saq, sbq, ROT_DIM // 2)
        q_ref[:, cols] = (r * (HEAD_DIM ** -0.5)).astype(BF16)
    zk = _dot(xb, w_ref[:, OFF_K:OFF_VB])
    for h in range(N_KV_B):
        cols = slice(h * HEAD_DIM, (h + 1) * HEAD_DIM)
        r = _rope_lanes(zk[:, cols], cq, saq, sbq, ROT_DIM // 2)
        k_ref[:, cols] = r
        kb_ref[:, cols] = r.astype(BF16)
    zvb = _dot(xb, w_ref[:, OFF_VB:OFF_QI])
    vb_ref[...] = zvb
    vbb_ref[...] = zvb.astype(BF16)

    ci, sai, sbi = ci_ref[...], sai_ref[...], sbi_ref[...]
    zqi = _dot(xb, w_ref[:, OFF_QI:OFF_KI])
    for p in range(N_IDX_HEADS * IDX_DIM // LANES):
        cols = slice(p * LANES, (p + 1) * LANES)
        qi_ref[:, cols] = _rope_lanes(zqi[:, cols], ci, sai, sbi, IDX_ROT_DIM // 2).astype(BF16)
    zl = _dot(xb, w_ref[:, OFF_KI:IN_COLS_PAD])
    r = _rope_lanes(zl, ci, sai, sbi, IDX_ROT_DIM // 2)
    ki_ref[...] = r[:, :IDX_DIM]
    lane = lax.broadcasted_iota(jnp.int32, r.shape, 1)
    kib_ref[...] = jnp.where(lane < IDX_DIM, r, pltpu.roll(r, IDX_DIM, 1)).astype(BF16)
    wi_ref[...] = zl[:, IDX_DIM:IDX_DIM + N_IDX_HEADS] * (N_IDX_HEADS ** -0.5 * IDX_DIM ** -0.5)


def _rope_tables(pos, rot_dim, head_dim):
    half = rot_dim // 2
    inv = ROPE_THETA ** (-jnp.arange(half, dtype=F32) / half)
    ang = pos.astype(F32)[:, None] * inv[None, :]
    cos, sin = jnp.cos(ang), jnp.sin(ang)
    lane = jnp.arange(LANES) % head_dim
    cos_l = jnp.take(cos, lane % half, axis=1)
    sin_l = jnp.take(sin, lane % half, axis=1)
    c = jnp.where(lane[None] < rot_dim, cos_l, 1.0)
    s_lo = jnp.where(lane[None] < half, -sin_l, 0.0)
    s_hi = jnp.where((lane[None] >= half) & (lane[None] < rot_dim), sin_l, 0.0)
    return c.astype(F32), s_lo.astype(F32), s_hi.astype(F32)


def _in_proj(x_p, x_s, tiles_per_seq, norm_mix, w_in_b, norm_v_a, wmix, bmix, tables):
    d_model = x_p.shape[1]
    n_prompt_tiles = x_p.shape[0] // IN_TM
    n_rows = x_p.shape[0] + x_s.shape[0]
    n_tiles = n_rows // IN_TM
    row = lambda i: (i, 0)
    const = lambda i: (0, 0)
    tab = lambda i: (jnp.where(i < n_prompt_tiles, i % tiles_per_seq, tiles_per_seq), 0)
    grp = lambda i: (jnp.where(i < n_prompt_tiles, 0, 1), 0, 0, 0)
    grp2 = lambda i: (jnp.where(i < n_prompt_tiles, 0, 1), 0, 0)
    tab_spec = pl.BlockSpec((IN_TM, LANES), tab)
    outs = [
        ((n_rows, D_A), BF16),
        ((n_rows, D_B), BF16),
        ((n_rows, D_KV_B), F32),
        ((n_rows, D_KV_B), BF16),
        ((n_rows, D_KV_B), F32),
        ((n_rows, D_KV_B), BF16),
        ((n_rows, N_IDX_HEADS * IDX_DIM), BF16),
        ((n_rows, IDX_DIM), F32),
        ((n_rows, LANES), BF16),
        ((n_rows, N_IDX_HEADS), F32),
        ((n_rows, D_A), F32),
    ]
    return pl.pallas_call(
        functools.partial(_in_proj_kernel, n_prompt_tiles),
        grid=(n_tiles,),
        in_specs=_group_specs(n_prompt_tiles, IN_TM, d_model) + [
            pl.BlockSpec((1, d_model), const),
            pl.BlockSpec((d_model, IN_COLS_PAD), const, pipeline_mode=pl.Buffered(1)),
            pl.BlockSpec((1, D_A), const),
            pl.BlockSpec((None, N_HEADS_A, CHUNK, CHUNK), grp),
            pl.BlockSpec((None, CHUNK, N_HEADS_A), grp2),
            tab_spec, tab_spec, tab_spec, tab_spec, tab_spec, tab_spec,
        ],
        out_specs=[pl.BlockSpec((IN_TM, s[1]), row) for s, _ in outs],
        out_shape=[jax.ShapeDtypeStruct(s, d) for s, d in outs],
        compiler_params=_params(("parallel",)),
        name="in_proj",
    )(x_p, x_s, norm_mix, w_in_b, norm_v_a, wmix, bmix, *tables)


ORDER_KEY_NEG_INF = -(2 ** 31) + 0x7FFFFF
KTH_SEARCH_UNROLL = 4


def _order_key_to_float(key):
    return pltpu.bitcast(key ^ ((key >> 31) & jnp.int32(0x7FFFFFFF)), F32)


def _kth_largest(score, k_top):
    def body(it, t):
        cand = t + lax.shift_left(jnp.int32(1), jnp.int32(31) - it)
        cnt = jnp.sum((score >= _order_key_to_float(cand)).astype(F32), axis=1, keepdims=True)
        return jnp.where(cnt >= k_top, cand, t)

    t0 = jnp.full((score.shape[0], 1), INT_MIN, jnp.int32)
    t = lax.fori_loop(0, 32, body, t0, unroll=KTH_SEARCH_UNROLL)
    return _order_key_to_float(jnp.maximum(t, ORDER_KEY_NEG_INF))


def _topk_select(score, k_top, tri):
    thr = _kth_largest(score, k_top)
    above = score > thr
    need = k_top - jnp.sum(above.astype(F32), axis=1, keepdims=True)
    tie = score == thr
    pieces = []
    run = jnp.zeros_like(need)
    for j in range(score.shape[1] // LANES):
        cols = slice(j * LANES, (j + 1) * LANES)
        tie_j = tie[:, cols]
        prefix = _dot(tie_j.astype(BF16), tri) + run
        run = prefix[:, LANES - 1:LANES]
        pieces.append(jnp.where(above[:, cols] | (tie_j & (prefix <= need)), 1.0, 0.0))
    return jnp.concatenate(pieces, axis=1)


def _tri_matrix():
    r = lax.broadcasted_iota(jnp.int32, (LANES, LANES), 0)
    c = lax.broadcasted_iota(jnp.int32, (LANES, LANES), 1)
    return (r <= c).astype(BF16)


PA_TQ = 128


PA_EXTENTS = 8


def _prompt_attn_kernel(k_top, qi_ref, wi_ref, kib_ref, q_ref, kb_ref, vbb_ref, o_ref):
    i = pl.program_id(1)
    step = kb_ref.shape[0] // PA_EXTENTS
    blocks_per_step = step // PA_TQ
    for c in range(PA_EXTENTS):
        @pl.when(i // blocks_per_step == c)
        def _():
            _prompt_attn_block(k_top, (c + 1) * step, i, qi_ref, wi_ref, kib_ref, q_ref, kb_ref, vbb_ref, o_ref)


def _prompt_attn_block(k_top, seq, i, qi_ref, wi_ref, kib_ref, q_ref, kb_ref, vbb_ref, o_ref):
    qi = qi_ref[...]
    wi = wi_ref[...]
    ki2 = kib_ref[:seq, :]
    lane = lax.broadcasted_iota(jnp.int32, (PA_TQ, LANES), 1)
    score = jnp.zeros((PA_TQ, seq), F32)
    for h in range(N_IDX_HEADS):
        pair = qi[:, (h // 2) * LANES:(h // 2 + 1) * LANES]
        keep = (lane < IDX_DIM) if h % 2 == 0 else (lane >= IDX_DIM)
        d = _dot_nt(jnp.where(keep, pair, jnp.zeros_like(pair)), ki2)
        score = score + wi[:, h:h + 1] * jnp.maximum(d, 0.0)
    q_pos = i * PA_TQ + lax.broadcasted_iota(jnp.int32, (PA_TQ, seq), 0)
    s_pos = lax.broadcasted_iota(jnp.int32, (PA_TQ, seq), 1)
    causal = s_pos <= q_pos
    sel = _topk_select(jnp.where(causal, score, NEG_INF), k_top, _tri_matrix())
    bias = jnp.where(causal & (sel > 0.0), 0.0, NEG_INF)

    q = q_ref[...]
    for g in range(N_KV_B):
        q4 = jnp.concatenate(
            [q[:, (g * KV_GROUP + r) * HEAD_DIM:(g * KV_GROUP + r + 1) * HEAD_DIM] for r in range(KV_GROUP)], axis=0)
        kg = kb_ref[:seq, g * HEAD_DIM:(g + 1) * HEAD_DIM]
        vg = vbb_ref[:seq, g * HEAD_DIM:(g + 1) * HEAD_DIM]
        s = _dot_nt(q4, kg).reshape(KV_GROUP, PA_TQ, seq) + bias[None]
        m = jnp.max(s, axis=-1, keepdims=True)
        p = jnp.exp(s - m)
        l = jnp.sum(p, axis=-1, keepdims=True)
        o = _dot(p.reshape(KV_GROUP * PA_TQ, seq).astype(BF16), vg).reshape(KV_GROUP, PA_TQ, HEAD_DIM) / l
        for r in range(KV_GROUP):
            o_ref[:, (g * KV_GROUP + r) * HEAD_DIM:(g * KV_GROUP + r + 1) * HEAD_DIM] = o[r].astype(BF16)


def _prompt_attn(n_batch, seq, k_top, qi, wi, kib, q, kb, vbb):
    nq = seq // PA_TQ
    qrow = lambda b, i: (b * nq + i, 0)
    krow = lambda b, i: (b, 0)
    return pl.pallas_call(
        functools.partial(_prompt_attn_kernel, k_top),
        grid=(n_batch, nq),
        in_specs=[
            pl.BlockSpec((PA_TQ, N_IDX_HEADS * IDX_DIM), qrow),
            pl.BlockSpec((PA_TQ, N_IDX_HEADS), qrow),
            pl.BlockSpec((seq, LANES), krow),
            pl.BlockSpec((PA_TQ, D_B), qrow),
            pl.BlockSpec((seq, D_KV_B), krow),
            pl.BlockSpec((seq, D_KV_B), krow),
        ],
        out_specs=pl.BlockSpec((PA_TQ, D_B), qrow),
        out_shape=jax.ShapeDtypeStruct((n_batch * seq, D_B), BF16),
        compiler_params=_params(("parallel", "parallel")),
        name="prompt_attn",
    )(qi, wi, kib, q, kb, vbb)


SP_PAGES = 16
SP_KEYS = SP_PAGES * PAGE_SIZE
SC_PAGES = 32
SC_KEYS = SC_PAGES * PAGE_SIZE
SP_TPAD = SUBLANES


def _page_specs(page_shape, n_steps, layer, pages):
    def make(p):
        def index(b, j, pt):
            return (layer, pt[b, jnp.minimum(j, n_steps - 1) * pages + p], 0, 0)
        return pl.BlockSpec((None, None) + page_shape, index)
    return [make(p) for p in range(pages)]


def _sample_scores_kernel(n_steps, pt_ref, qs_ref, ws_ref, kin_ref, *refs):
    page_refs, o_ref = refs[:SC_PAGES], refs[SC_PAGES]
    j = pl.program_id(1)
    qs = qs_ref[...]
    ws = ws_ref[...]

    def head_sum(d):
        w = ws * jnp.maximum(d, 0.0)
        return jnp.sum(w.reshape(N_IDX_HEADS, SP_TPAD, d.shape[1]), axis=0)

    @pl.when(j < n_steps)
    def _():
        keys_t = jnp.concatenate([r[...] for r in page_refs], axis=1).astype(BF16)
        o_ref[...] = head_sum(_dot(qs, keys_t))

    @pl.when(j == n_steps)
    def _():
        d = _dot_nt(qs, kin_ref[...])
        o_ref[...] = jnp.concatenate(
            [head_sum(d), jnp.zeros((SP_TPAD, SC_KEYS - d.shape[1]), F32)], axis=1)


def _sample_scores(page_table, qs, ws, ki_new, cache_idx_k, layer):
    n_db, n_pages = page_table.shape
    n_steps = n_pages // SC_PAGES
    rows = N_IDX_HEADS * SP_TPAD
    per_b = lambda b, j, pt: (b, 0, 0)
    grid_spec = pltpu.PrefetchScalarGridSpec(
        num_scalar_prefetch=1,
        grid=(n_db, n_steps + 1),
        in_specs=[
            pl.BlockSpec((None, rows, IDX_DIM), per_b),
            pl.BlockSpec((None, rows, 1), per_b),
            pl.BlockSpec((None, LANES, IDX_DIM), per_b),
        ] + _page_specs((IDX_DIM, PAGE_SIZE), n_steps, layer, SC_PAGES),
        out_specs=pl.BlockSpec((None, SP_TPAD, SC_KEYS), lambda b, j, pt: (b, 0, j)),
    )
    return pl.pallas_call(
        functools.partial(_sample_scores_kernel, n_steps),
        grid_spec=grid_spec,
        out_shape=jax.ShapeDtypeStruct((n_db, SP_TPAD, (n_steps + 1) * SC_KEYS), F32),
        compiler_params=_params(("parallel", "arbitrary")),
        name="sample_scores",
    )(page_table, qs, ws, ki_new, *([cache_idx_k] * SC_PAGES))


SM_ROWS = 64


def _sample_mask_kernel(k_top, past, n_new, s_ref, o_ref):
    score = s_ref[...]
    t = lax.broadcasted_iota(jnp.int32, score.shape, 0) % SP_TPAD
    s_pos = lax.broadcasted_iota(jnp.int32, score.shape, 1)
    admissible = s_pos <= past + t
    sel = _topk_select(jnp.where(admissible, score, NEG_INF), k_top, _tri_matrix())
    keep = jnp.where((admissible & (sel > 0.0)) | (t >= n_new), 1.0, 0.0).astype(BF16)
    r = lax.broadcasted_iota(jnp.int32, (LANES, N_KV_B * LANES), 0)
    c = lax.broadcasted_iota(jnp.int32, (LANES, N_KV_B * LANES), 1)
    spread = (c // N_KV_B == r).astype(BF16)
    for j in range(score.shape[1] // LANES):
        d = _dot(keep[:, j * LANES:(j + 1) * LANES], spread)
        o_ref[:, j * N_KV_B * LANES:(j + 1) * N_KV_B * LANES] = jnp.where(d > 0.5, 0.0, NEG_INF)


def _sample_mask(scores2d, k_top, past, n_new):
    n_rows, width = scores2d.shape
    return pl.pallas_call(
        functools.partial(_sample_mask_kernel, k_top, past, n_new),
        grid=(n_rows // SM_ROWS,),
        in_specs=[pl.BlockSpec((SM_ROWS, width), lambda i: (i, 0))],
        out_specs=pl.BlockSpec((SM_ROWS, N_KV_B * width), lambda i: (i, 0)),
        out_shape=jax.ShapeDtypeStruct((n_rows, N_KV_B * width), F32),
        compiler_params=_params(("parallel",)),
        name="sample_mask",
    )(scores2d)


SA_ROWS = KV_GROUP * SP_TPAD
SA_ROWS_ALL = N_KV_B * SA_ROWS
SA_PAGE_ROWS = PAGE_SIZE * N_KV_B


def _sample_attn_kernel(n_steps, pt_ref, q_ref, bias_ref, kn_ref, vn_ref, *refs):
    k_refs, v_refs = refs[:SP_PAGES], refs[SP_PAGES:2 * SP_PAGES]
    o_ref, m_ref, l_ref, acc_ref = refs[2 * SP_PAGES:]
    j = pl.program_id(1)

    @pl.when(j == 0)
    def _():
        m_ref[...] = jnp.full(m_ref.shape, NEG_INF, F32)
        l_ref[...] = jnp.zeros(l_ref.shape, F32)
        acc_ref[...] = jnp.zeros(acc_ref.shape, F32)

    def update(kc, vc, bias):
        n = kc.shape[0]
        s = _dot_nt(q_ref[...], kc)
        row_head = lax.broadcasted_iota(jnp.int32, (SA_ROWS_ALL, n), 0) // SA_ROWS
        col_head = lax.broadcasted_iota(jnp.int32, (SA_ROWS_ALL, n), 1) % N_KV_B
        s = jnp.where(row_head == col_head, s + jnp.concatenate([bias] * (SA_ROWS_ALL // SP_TPAD), axis=0), NEG_INF)
        m_old = m_ref[...]
        m_new = jnp.maximum(m_old, jnp.max(s, axis=-1, keepdims=True))
        m_safe = jnp.where(m_new == NEG_INF, 0.0, m_new)
        alpha = jnp.exp(m_old - m_safe)
        p = jnp.exp(s - m_safe)
        l_ref[...] = alpha * l_ref[...] + jnp.sum(p, axis=-1, keepdims=True)
        acc_ref[...] = alpha * acc_ref[...] + _dot(p.astype(BF16), vc)
        m_ref[...] = m_new

    @pl.when(j < n_steps)
    def _():
        kc = jnp.concatenate([r[...] for r in k_refs], axis=0).astype(BF16)
        vc = jnp.concatenate([r[...] for r in v_refs], axis=0).astype(BF16)
        update(kc, vc, bias_ref[...])

    @pl.when(j == n_steps)
    def _():
        update(kn_ref[...], vn_ref[...], bias_ref[:, :SA_PAGE_ROWS])
        o_ref[...] = acc_ref[...] / l_ref[...]


def _sample_attn(page_table, qs, bias, k_new, v_new, cache_k, cache_v, layer):
    n_db, n_pages = page_table.shape
    n_steps = n_pages // SP_PAGES
    per_b = lambda b, j, pt: (b, 0, 0)
    grid_spec = pltpu.PrefetchScalarGridSpec(
        num_scalar_prefetch=1,
        grid=(n_db, n_steps + 1),
        in_specs=[
            pl.BlockSpec((None, SA_ROWS_ALL, HEAD_DIM), per_b),
            pl.BlockSpec((None, SP_TPAD, N_KV_B * SP_KEYS), lambda b, j, pt: (b, 0, j)),
            pl.BlockSpec((None, SA_PAGE_ROWS, HEAD_DIM), per_b),
            pl.BlockSpec((None, SA_PAGE_ROWS, HEAD_DIM), per_b),
        ] + 2 * _page_specs((SA_PAGE_ROWS, HEAD_DIM), n_steps, layer, SP_PAGES),
        out_specs=pl.BlockSpec((None, SA_ROWS_ALL, HEAD_DIM), per_b),
        scratch_shapes=[
            pltpu.VMEM((SA_ROWS_ALL, 1), F32),
            pltpu.VMEM((SA_ROWS_ALL, 1), F32),
            pltpu.VMEM((SA_ROWS_ALL, HEAD_DIM), F32),
        ],
    )
    return pl.pallas_call(
        functools.partial(_sample_attn_kernel, n_steps),
        grid_spec=grid_spec,
        out_shape=jax.ShapeDtypeStruct((n_db, SA_ROWS_ALL, HEAD_DIM), F32),
        compiler_params=_params(("parallel", "arbitrary")),
        name="sample_attn",
    )(page_table, qs, bias, k_new, v_new, *([cache_k] * SP_PAGES), *([cache_v] * SP_PAGES))


OR_TM = 256


def _split_bf16(x):
    hi = x.astype(BF16)
    return hi, (x - hi.astype(F32)).astype(BF16)


def _out_router_kernel(n_prompt_tiles, a_ref, b_ref, xp_ref, xs_ref, wo_ref, nf_ref, wrh_ref, wrl_ref, br_ref,
                       xmid_ref, h_ref, idx_ref, gate_ref):
    x = _group_rows(n_prompt_tiles, xp_ref, xs_ref)
    x_mid = x + _dot(a_ref[...], wo_ref[:D_A, :]) + _dot(b_ref[...], wo_ref[D_A:, :])
    xmid_ref[...] = x_mid
    h = (x_mid * lax.rsqrt(jnp.mean(x_mid * x_mid, axis=-1, keepdims=True) + RMS_EPS)) * nf_ref[...]
    h_ref[...] = h
    h_hi, h_lo = _split_bf16(h)
    wr_hi = wrh_ref[...]
    logits = _dot(h_hi, wr_hi) + _dot(h_lo, wr_hi) + _dot(h_hi, wrl_ref[...]) + br_ref[...]
    lane = lax.broadcasted_iota(jnp.int32, logits.shape, 1)
    vals, idxs = [], []
    for _ in range(TOP_K_EXPERTS):
        m = jnp.max(logits, axis=-1, keepdims=True)
        i = jnp.min(jnp.where(logits == m, lane, LANES), axis=-1, keepdims=True)
        vals.append(m)
        idxs.append(i)
        logits = jnp.where(lane == i, NEG_INF, logits)
    exps = [jnp.exp(v - vals[0]) for v in vals]
    total = exps[0]
    for e in exps[1:]:
        total = total + e
    idx_out = jnp.zeros(logits.shape, jnp.int32)
    gate_out = jnp.zeros(logits.shape, F32)
    for k in range(TOP_K_EXPERTS):
        idx_out = jnp.where(lane == k, idxs[k], idx_out)
        gate_out = jnp.where(lane == k, exps[k] / total, gate_out)
    idx_ref[...] = idx_out
    gate_ref[...] = gate_out


def _out_router(a_out, b_out, x_p, x_s, w_out_b, norm_ffn, wr_hi, wr_lo, br_pad):
    d_model = x_p.shape[1]
    n_prompt_tiles = x_p.shape[0] // OR_TM
    n_rows = x_p.shape[0] + x_s.shape[0]
    row = lambda i: (i, 0)
    const = lambda i: (0, 0)
    return pl.pallas_call(
        functools.partial(_out_router_kernel, n_prompt_tiles),
        grid=(n_rows // OR_TM,),
        in_specs=[
            pl.BlockSpec((OR_TM, D_A), row),
            pl.BlockSpec((OR_TM, D_B), row),
        ] + _group_specs(n_prompt_tiles, OR_TM, d_model) + [
            pl.BlockSpec((D_A + D_B, d_model), const),
            pl.BlockSpec((1, d_model), const),
            pl.BlockSpec((d_model, LANES), const),
            pl.BlockSpec((d_model, LANES), const),
            pl.BlockSpec((1, LANES), const),
        ],
        out_specs=[
            pl.BlockSpec((OR_TM, d_model), row),
            pl.BlockSpec((OR_TM, d_model), row),
            pl.BlockSpec((OR_TM, LANES), row),
            pl.BlockSpec((OR_TM, LANES), row),
        ],
        out_shape=[
            jax.ShapeDtypeStruct((n_rows, d_model), F32),
            jax.ShapeDtypeStruct((n_rows, d_model), F32),
            jax.ShapeDtypeStruct((n_rows, LANES), jnp.int32),
            jax.ShapeDtypeStruct((n_rows, LANES), F32),
        ],
        compiler_params=_params(("parallel",)),
        name="out_router",
    )(a_out, b_out, x_p, x_s, w_out_b, norm_ffn, wr_hi, wr_lo, br_pad)


MOE_TM = 256
MOE_UNIT_TILES = 5
MOE_R = MOE_TM * MOE_UNIT_TILES
MOE_TF = 256
COMBINE_TM = 128
ROW_COPY_UNROLL = 8


def _row_copy(src_ref, src_row, dst_ref, dst_row, sem):
    return pltpu.make_async_copy(src_ref.at[pl.ds(src_row, 1), :], dst_ref.at[pl.ds(dst_row, 1), :], sem)


def _tile_rows(rows):
    return pl.cdiv(rows, MOE_TM) * MOE_TM


def _moe_kernel(ue_ref, ur_ref, ub_ref, tok_ref, tokn_ref, h_ref, wg_ref, wu_ref, bg_ref, bu_ref, wd_ref,
                bd_ref, ys_ref, land_ref, xb_ref, y_ref, wgb_ref, wub_ref, wdb_ref, gsem, ysem):
    u, f = pl.program_id(0), pl.program_id(1)
    n_units, n_f = pl.num_programs(0), pl.num_programs(1)
    rows = ur_ref[u]
    n_tiles = pl.cdiv(rows, MOE_TM)
    rows_next = jnp.where(u + 1 < n_units, ur_ref[jnp.minimum(u + 1, n_units - 1)], 0)

    def gather(tokens_ref, n_rows, wait):
        def tile(m, c):
            def body(j, c2):
                r = m * MOE_TM + j
                cp = _row_copy(h_ref, tokens_ref[0, r], land_ref, r, gsem)
                cp.wait() if wait else cp.start()
                return c2
            return lax.fori_loop(0, MOE_TM, body, c, unroll=ROW_COPY_UNROLL)
        lax.fori_loop(0, n_rows // MOE_TM, tile, 0)

    def writeback(unit, n, wait):
        def body(m, c):
            r0 = pl.multiple_of(m * MOE_TM, MOE_TM)
            cp = pltpu.make_async_copy(y_ref.at[pl.ds(r0, MOE_TM), :],
                                       ys_ref.at[pl.ds(ub_ref[unit] * MOE_R + r0, MOE_TM), :], ysem)
            cp.wait() if wait else cp.start()
            return c
        lax.fori_loop(0, n, body, 0)

    @pl.when((f == 0) & (rows > 0))
    def _():
        @pl.when(u == 0)
        def _():
            gather(tok_ref, _tile_rows(rows), wait=False)

        gather(tok_ref, _tile_rows(rows), wait=True)

        def convert(m, c):
            r0 = pl.multiple_of(m * MOE_TM, MOE_TM)
            xb_ref[pl.ds(r0, MOE_TM), :] = land_ref[pl.ds(r0, MOE_TM), :].astype(BF16)
            return c
        lax.fori_loop(0, n_tiles, convert, 0)

        gather(tokn_ref, _tile_rows(rows_next), wait=False)

    @pl.when((f == 0) & (u > 0))
    def _():
        writeback(u - 1, pl.cdiv(ur_ref[jnp.maximum(u - 1, 0)], MOE_TM), wait=True)

    @pl.when(rows > 0)
    def _():
        wgb_ref[...] = wg_ref[...].astype(BF16)
        wub_ref[...] = wu_ref[...].astype(BF16)
        wdb_ref[...] = wd_ref[...].astype(BF16)

        def tile_at(m, size):
            r0 = pl.multiple_of(m * MOE_TM, MOE_TM)
            xm = xb_ref[pl.ds(r0, size), :]
            gate = jnp.minimum(_dot(xm, wgb_ref[...]) + bg_ref[...], SWIGLU_LIMIT)
            up = jnp.clip(_dot(xm, wub_ref[...]) + bu_ref[...], -SWIGLU_LIMIT, SWIGLU_LIMIT)
            act = (up + 1.0) * (gate * jax.nn.sigmoid(SWIGLU_ALPHA * gate))
            y_ref[pl.ds(r0, size), :] += _dot(act.astype(BF16), wdb_ref[...])

        def tile(m, c):
            tile_at(m, MOE_TM)
            return c

        @pl.when(f == 0)
        def _():
            def init(m, c):
                r0 = pl.multiple_of(m * MOE_TM, MOE_TM)
                y_ref[pl.ds(r0, MOE_TM), :] = jnp.broadcast_to(bd_ref[...], (MOE_TM, y_ref.shape[1]))
                return c
            lax.fori_loop(0, n_tiles, init, 0)

        def tile_pair(p, c):
            tile(2 * p, c)
            return tile(2 * p + 1, c)

        n_full = (rows + MOE_TM // 2 - 1) // MOE_TM
        lax.fori_loop(0, n_full // 2, tile_pair, 0)

        @pl.when(n_full % 2 == 1)
        def _():
            tile(n_full - 1, 0)

        @pl.when(rows > n_full * MOE_TM)
        def _():
            tile_at(n_full, MOE_TM // 2)

        @pl.when(f == n_f - 1)
        def _():
            writeback(u, n_tiles, wait=False)

    @pl.when((f == n_f - 1) & (u == n_units - 1))
    def _():
        writeback(u, n_tiles, wait=True)


def _moe_experts(unit_expert, unit_rows, unit_block, slot_token, h, w_gate_up, b_gate_up, w_down, b_down):
    n_units = unit_expert.shape[0]
    d_model = h.shape[1]
    d_ff = w_down.shape[1]
    n_f = d_ff // MOE_TF

    def f_eff(u, f, ur):
        return jnp.where(ur[u] > 0, f, n_f - 1)

    smem_tokens = lambda index: pl.BlockSpec((None, 1, MOE_R), index, memory_space=pltpu.SMEM)
    grid_spec = pltpu.PrefetchScalarGridSpec(
        num_scalar_prefetch=3,
        grid=(n_units, n_f),
        in_specs=[
            smem_tokens(lambda u, f, ue, ur, ub: (ub[u], 0, 0)),
            smem_tokens(lambda u, f, ue, ur, ub: (ub[jnp.minimum(u + 1, n_units - 1)], 0, 0)),
            pl.BlockSpec(memory_space=pl.ANY),
            pl.BlockSpec((None, d_model, MOE_TF), lambda u, f, ue, ur, ub: (ue[u], 0, f_eff(u, f, ur))),
            pl.BlockSpec((None, d_model, MOE_TF), lambda u, f, ue, ur, ub: (ue[u], 0, n_f + f_eff(u, f, ur))),
            pl.BlockSpec((None, 1, MOE_TF), lambda u, f, ue, ur, ub: (ue[u], 0, f_eff(u, f, ur))),
            pl.BlockSpec((None, 1, MOE_TF), lambda u, f, ue, ur, ub: (ue[u], 0, n_f + f_eff(u, f, ur))),
            pl.BlockSpec((None, MOE_TF, d_model), lambda u, f, ue, ur, ub: (ue[u], f_eff(u, f, ur), 0)),
            pl.BlockSpec((None, 1, d_model), lambda u, f, ue, ur, ub: (ue[u], 0, 0)),
        ],
        out_specs=pl.BlockSpec(memory_space=pl.ANY),
        scratch_shapes=[
            pltpu.VMEM((MOE_R, d_model), F32),
            pltpu.VMEM((MOE_R, d_model), BF16),
            pltpu.VMEM((MOE_R, d_model), F32),
            pltpu.VMEM((d_model, MOE_TF), BF16),
            pltpu.VMEM((d_model, MOE_TF), BF16),
            pltpu.VMEM((MOE_TF, d_model), BF16),
            pltpu.SemaphoreType.DMA,
            pltpu.SemaphoreType.DMA,
        ],
    )
    tokens = slot_token.reshape(n_units, 1, MOE_R)
    return pl.pallas_call(
        _moe_kernel,
        grid_spec=grid_spec,
        out_shape=jax.ShapeDtypeStruct((n_units * MOE_R, d_model), F32),
        compiler_params=_params(("arbitrary", "arbitrary")),
        name="moe_experts",
    )(unit_expert, unit_rows, unit_block, tokens, tokens, h, w_gate_up, w_gate_up, b_gate_up, b_gate_up,
      w_down, b_down)


def _combine_kernel(n_prompt_tiles, pos_ref, posn_ref, gate_ref, xmid_ref, nfin_ref, ys_ref, op_ref, os_ref,
                    buf_ref, sem):
    i, n_tiles = pl.program_id(0), pl.num_programs(0)
    slot = i % 2

    def rows(p_ref, s, wait):
        for k in range(TOP_K_EXPERTS):
            def body(t, c):
                cp = _row_copy(ys_ref, p_ref[0, t * TOP_K_EXPERTS + k], buf_ref.at[s], k * COMBINE_TM + t,
                               sem.at[s])
                cp.wait() if wait else cp.start()
                return c
            lax.fori_loop(0, COMBINE_TM, body, 0, unroll=ROW_COPY_UNROLL)

    @pl.when(i == 0)
    def _():
        rows(pos_ref, 0, wait=False)

    @pl.when(i + 1 < n_tiles)
    def _():
        rows(posn_ref, 1 - slot, wait=False)

    rows(pos_ref, slot, wait=True)
    gates = gate_ref[...]
    out = xmid_ref[...]
    for k in range(TOP_K_EXPERTS):
        out = out + gates[:, k:k + 1] * buf_ref[slot, k * COMBINE_TM:(k + 1) * COMBINE_TM, :]
    y = (out * lax.rsqrt(jnp.mean(out * out, axis=-1, keepdims=True) + RMS_EPS)) * nfin_ref[...]

    @pl.when(i < n_prompt_tiles)
    def _():
        op_ref[...] = y

    @pl.when(i >= n_prompt_tiles)
    def _():
        os_ref[...] = y


def _combine(pos, gates, x_mid, norm_final, ys, n_prompt, n_sample):
    d_model = x_mid.shape[1]
    n_prompt_tiles = n_prompt // COMBINE_TM
    n_tiles = n_prompt_tiles + n_sample // COMBINE_TM
    n = COMBINE_TM * TOP_K_EXPERTS
    pos3 = pos.reshape(n_tiles, 1, n)
    row = lambda i: (i, 0)
    return pl.pallas_call(
        functools.partial(_combine_kernel, n_prompt_tiles),
        grid=(n_tiles,),
        in_specs=[
            pl.BlockSpec((None, 1, n), lambda i: (i, 0, 0), memory_space=pltpu.SMEM),
            pl.BlockSpec((None, 1, n), lambda i: (jnp.minimum(i + 1, n_tiles - 1), 0, 0), memory_space=pltpu.SMEM),
            pl.BlockSpec((COMBINE_TM, LANES), row),
            pl.BlockSpec((COMBINE_TM, d_model), row),
            pl.BlockSpec((1, d_model), lambda i: (0, 0)),
            pl.BlockSpec(memory_space=pl.ANY),
        ],
        out_specs=[
            pl.BlockSpec((COMBINE_TM, d_model), lambda i: (jnp.minimum(i, n_prompt_tiles - 1), 0)),
            pl.BlockSpec((COMBINE_TM, d_model), lambda i: (jnp.maximum(i - n_prompt_tiles, 0), 0)),
        ],
        out_shape=[jax.ShapeDtypeStruct((n_prompt, d_model), F32), jax.ShapeDtypeStruct((n_sample, d_model), F32)],
        scratch_shapes=[pltpu.VMEM((2, n, d_model), F32), pltpu.SemaphoreType.DMA((2,))],
        compiler_params=_params(("arbitrary",)),
        name="combine",
    )(pos3, pos3, gates, x_mid, norm_final, ys)


def _routing(top_idx, n_units, zero_row):
    flat_e = top_idx.reshape(-1)
    onehot = (flat_e[:, None] == jnp.arange(N_EXPERTS, dtype=jnp.int32)[None, :]).astype(jnp.int32)
    csum = jnp.cumsum(onehot, axis=0)
    rank = jnp.take_along_axis(csum, flat_e[:, None], axis=1)[:, 0] - 1
    counts = csum[-1]
    units_e = (counts + MOE_R - 1) // MOE_R
    units_end = jnp.cumsum(units_e)
    units_start = units_end - units_e
    pos = (units_start[flat_e] + rank // MOE_R) * MOE_R + rank % MOE_R
    u = jnp.arange(n_units, dtype=jnp.int32)
    total = units_end[-1]
    ue = jnp.minimum(jnp.sum((units_end[None, :] <= u[:, None]).astype(jnp.int32), axis=1), N_EXPERTS - 1)
    rows = jnp.clip(counts[ue] - (u - units_start[ue]) * MOE_R, 0, MOE_R)
    used = u < total
    last_e = ue[jnp.maximum(total - 1, 0)]
    unit_expert = jnp.where(used, ue, last_e).astype(jnp.int32)
    unit_rows = jnp.where(used, rows, 0).astype(jnp.int32)
    unit_block = jnp.minimum(u, total - 1).astype(jnp.int32)
    pos = pos.astype(jnp.int32)
    token = jnp.arange(flat_e.shape[0], dtype=jnp.int32) // TOP_K_EXPERTS
    slot_token = jnp.full((n_units * MOE_R,), zero_row, jnp.int32).at[pos].set(token, unique_indices=True)
    return pos, slot_token, unit_expert, unit_rows, unit_block


def kernel(x_prompt, x_sample, cache_k, cache_v, cache_idx_k, page_table, norm_mix, w_in, norm_v_a,
           w_spatial, b_spatial, w_out, norm_ffn, w_router, b_router, w_gate_up, b_gate_up, w_down,
           b_down, norm_final):
    n_b, seq, d_model = x_prompt.shape
    n_db, n_new, _ = x_sample.shape
    depth = w_in.shape[0]
    past = page_table.shape[1] * PAGE_SIZE
    n_prompt, n_sample = n_b * seq, n_db * n_new
    n_tokens = n_prompt + n_sample
    assert seq % IN_TM == 0 and seq % PA_TQ == 0 and CHUNK % n_new == 0 and n_new <= SP_TPAD
    assert n_sample % CHUNK == 0 and page_table.shape[1] % SP_PAGES == 0 and page_table.shape[1] % SC_PAGES == 0
    assert n_prompt % COMBINE_TM == 0 and n_sample % COMBINE_TM == 0
    assert depth == 1, "the final norm is fused into the last layer's combine; deeper stacks are not wired up"
    n_rows = (n_tokens // IN_TM + 1) * IN_TM
    k_top_prompt = min(TOPK_MAX, seq // 4)
    k_top_sample = min(TOPK_MAX, (past + n_new) // 4)
    n_assign = n_tokens * TOP_K_EXPERTS
    n_units = N_EXPERTS + n_assign // MOE_R

    pos_rows = jnp.concatenate([jnp.arange(seq, dtype=jnp.int32),
                                past + jnp.arange(IN_TM, dtype=jnp.int32) % n_new])
    tables = _rope_tables(pos_rows, ROT_DIM, HEAD_DIM) + _rope_tables(pos_rows, IDX_ROT_DIM, IDX_DIM)

    x_p = x_prompt.reshape(n_prompt, d_model)
    x_s = jnp.pad(x_sample.reshape(n_sample, d_model), ((0, n_rows - n_tokens), (0, 0)))
    outs = []
    for l in range(depth):
        causal = jnp.tril(jnp.ones((CHUNK, CHUNK), bool))
        w_prompt = jnp.where(causal[None], w_spatial[l], 0.0)
        eye = jnp.eye(CHUNK // n_new, dtype=F32)
        w_decode = jax.vmap(lambda w: jnp.kron(eye, w))(w_prompt[:, :n_new, :n_new])
        wmix = jnp.stack([w_prompt, w_decode]).astype(BF16)
        b_decode = jnp.tile(b_spatial[l][:, :n_new], (1, CHUNK // n_new))
        bmix = jnp.stack([b_spatial[l].T, b_decode.T])
        w_in_b = jnp.pad(w_in[l], ((0, 0), (0, IN_COLS_PAD - IN_COLS))).astype(BF16)

        (a_out, q, k, kb, vb, vbb, qi, ki, kib, wi, va) = _in_proj(
            x_p, x_s, seq // IN_TM, norm_mix[l][None], w_in_b, norm_v_a[l][None], wmix, bmix, tables)

        b_prompt = _prompt_attn(n_b, seq, k_top_prompt, qi, wi, kib, q, kb, vbb)

        sl = slice(n_prompt, n_tokens)
        pad_t = ((0, 0), (0, 0), (0, SP_TPAD - n_new), (0, 0))
        qi_s = qi[sl].reshape(n_db, n_new, N_IDX_HEADS, IDX_DIM).transpose(0, 2, 1, 3)
        qs_idx = jnp.pad(qi_s, pad_t).reshape(n_db, N_IDX_HEADS * SP_TPAD, IDX_DIM)
        wi_s = wi[sl].reshape(n_db, n_new, N_IDX_HEADS).transpose(0, 2, 1)
        ws_idx = jnp.pad(wi_s, ((0, 0), (0, 0), (0, SP_TPAD - n_new))).reshape(n_db, N_IDX_HEADS * SP_TPAD, 1)
        pad_rows = ((0, 0), (0, LANES - n_new), (0, 0))
        ki_new = jnp.pad(kib[sl, :IDX_DIM].reshape(n_db, n_new, IDX_DIM), pad_rows)
        scores = _sample_scores(page_table, qs_idx, ws_idx, ki_new, jnp.swapaxes(cache_idx_k, 2, 3), l)
        width = scores.shape[-1]
        bias = _sample_mask(scores.reshape(n_db * SP_TPAD, width), k_top_sample, past, n_new)
        q_s = q[sl].reshape(n_db, n_new, N_KV_B, KV_GROUP, HEAD_DIM).transpose(0, 2, 3, 1, 4)
        qs_att = jnp.pad(q_s, ((0, 0), (0, 0), (0, 0), (0, SP_TPAD - n_new), (0, 0))).reshape(
            n_db, SA_ROWS_ALL, HEAD_DIM)
        pad_kv = ((0, 0), (0, SA_PAGE_ROWS - n_new * N_KV_B), (0, 0))
        k_new = jnp.pad(kb[sl].reshape(n_db, n_new * N_KV_B, HEAD_DIM), pad_kv)
        v_new = jnp.pad(vbb[sl].reshape(n_db, n_new * N_KV_B, HEAD_DIM), pad_kv)
        n_pool = cache_k.shape[1]
        o_s = _sample_attn(page_table, qs_att, bias.reshape(n_db, SP_TPAD, N_KV_B * width), k_new, v_new,
                           cache_k.reshape(depth, n_pool, SA_PAGE_ROWS, HEAD_DIM),
                           cache_v.reshape(depth, n_pool, SA_PAGE_ROWS, HEAD_DIM), l)
        b_sample = o_s.reshape(n_db, N_KV_B, KV_GROUP, SP_TPAD, HEAD_DIM)[:, :, :, :n_new]
        b_sample = b_sample.transpose(0, 3, 1, 2, 4).reshape(n_sample, D_B).astype(BF16)
        b_out = jnp.concatenate([b_prompt, b_sample, jnp.zeros((n_rows - n_tokens, D_B), BF16)])

        wr_pad = jnp.pad(w_router[l], ((0, 0), (0, LANES - N_EXPERTS)))
        wr_hi, wr_lo = _split_bf16(wr_pad)
        br_pad = jnp.pad(b_router[l], (0, LANES - N_EXPERTS), constant_values=NEG_INF)[None]
        x_mid, h, top_idx, gates = _out_router(a_out, b_out, x_p, x_s, w_out[l].astype(BF16),
                                               norm_ffn[l][None], wr_hi, wr_lo, br_pad)

        pos, slot_token, unit_expert, unit_rows, unit_block = _routing(
            top_idx[:n_tokens, :TOP_K_EXPERTS], n_units, zero_row=n_tokens)
        ys = _moe_experts(unit_expert, unit_rows, unit_block, slot_token, h, w_gate_up[l],
                          b_gate_up[l][:, None, :], w_down[l], b_down[l][:, None, :])
        y_p, y_s = _combine(pos, gates, x_mid, norm_final[None], ys, n_prompt, n_sample)
        outs.append((k, vb, ki, va))

    def stack(j, lo, hi, shape):
        return jnp.stack([o[j][lo:hi].reshape(shape) for o in outs])

    return (
        y_p.reshape(n_b, seq, d_model), y_s.reshape(n_db, n_new, d_model),
        stack(0, 0, n_prompt, (n_b, seq, N_KV_B, HEAD_DIM)),
        stack(1, 0, n_prompt, (n_b, seq, N_KV_B, HEAD_DIM)),
        stack(2, 0, n_prompt, (n_b, seq, IDX_DIM)),
        stack(0, n_prompt, n_tokens, (n_db, n_new, N_KV_B, HEAD_DIM)),
        stack(1, n_prompt, n_tokens, (n_db, n_new, N_KV_B, HEAD_DIM)),
        stack(2, n_prompt, n_tokens, (n_db, n_new, IDX_DIM)),
        stack(3, n_prompt, n_tokens, (n_db, n_new, N_HEADS_A, HEAD_DIM)),
    )
```

```python
import functools

import jax
import jax.numpy as jnp
from jax import lax
from jax.experimental import pallas as pl
from jax.experimental.pallas import tpu as pltpu

HEAD_DIM = 128
N_HEADS_A = 8
D_A = N_HEADS_A * HEAD_DIM
CHUNK = 128
N_HEADS_B = 8
N_KV_B = 2
KV_GROUP = N_HEADS_B // N_KV_B
D_B = N_HEADS_B * HEAD_DIM
D_KV_B = N_KV_B * HEAD_DIM
ROT_DIM = HEAD_DIM // 4
ROPE_THETA = 500000.0
N_IDX_HEADS = 16
IDX_DIM = 64
IDX_ROT_DIM = IDX_DIM // 4
TOPK_MAX = 256
N_EXPERTS = 32
TOP_K_EXPERTS = 4
SWIGLU_LIMIT = 7.0
SWIGLU_ALPHA = 1.702
RMS_EPS = 1e-5
PAGE_SIZE = 128

OFF_U = 0
OFF_V = OFF_U + D_A
OFF_Q = OFF_V + D_A
OFF_K = OFF_Q + D_B
OFF_VB = OFF_K + D_KV_B
OFF_QI = OFF_VB + D_KV_B
OFF_KI = OFF_QI + N_IDX_HEADS * IDX_DIM
OFF_WI = OFF_KI + IDX_DIM
IN_COLS = OFF_WI + N_IDX_HEADS

LANES = 128
SUBLANES = 8
VMEM_LIMIT_BYTES = 56 * 1024 * 1024

IN_COLS_PAD = pl.cdiv(IN_COLS, LANES) * LANES
INT_MIN = -(2 ** 31)
NEG_INF = float("-inf")

F32 = jnp.float32
BF16 = jnp.bfloat16


def _params(semantics):
    return pltpu.CompilerParams(dimension_semantics=semantics, vmem_limit_bytes=VMEM_LIMIT_BYTES)


def _dot(a, b):
    return jnp.dot(a, b, preferred_element_type=F32)


def _dot_nt(a, b):
    return lax.dot_general(a, b, (((1,), (1,)), ((), ())), preferred_element_type=F32)


IN_TM = 256


def _rope_lanes(z, cos, sin_lo, sin_hi, half):
    return z * cos + pltpu.roll(z, LANES - half, 1) * sin_lo + pltpu.roll(z, half, 1) * sin_hi


def _gelu(z):
    return 0.5 * z * (1.0 + lax.erf(z * (2.0 ** -0.5)))


def _group_rows(n_prompt_tiles, xp_ref, xs_ref):
    return jnp.where(pl.program_id(0) < n_prompt_tiles, xp_ref[...], xs_ref[...])


def _group_specs(n_prompt_tiles, tm, d_model):
    prompt = pl.BlockSpec((tm, d_model), lambda i: (jnp.minimum(i, n_prompt_tiles - 1), 0))
    decode = pl.BlockSpec((tm, d_model), lambda i: (jnp.maximum(i - n_prompt_tiles, 0), 0))
    return [prompt, decode]


def _in_proj_kernel(n_prompt_tiles, xp_ref, xs_ref, g_ref, w_ref, nva_ref, wmix_ref, bmix_ref,
                    cq_ref, saq_ref, sbq_ref, ci_ref, sai_ref, sbi_ref,
                    aout_ref, q_ref, k_ref, kb_ref, vb_ref, vbb_ref, qi_ref, ki_ref, kib_ref, wi_ref, va_ref):
    x = _group_rows(n_prompt_tiles, xp_ref, xs_ref)
    ms = jnp.mean(x * x, axis=-1, keepdims=True)
    xb = ((x * lax.rsqrt(ms + RMS_EPS)) * g_ref[...]).astype(BF16)

    u = _gelu(_dot(xb, w_ref[:, OFF_U:OFF_V]))
    gv = _gelu(_dot(xb, w_ref[:, OFF_V:OFF_Q]))
    va = gv * lax.rsqrt(jnp.mean(gv * gv, axis=-1, keepdims=True) + RMS_EPS) * nva_ref[...]
    va_ref[...] = va
    vab = va.astype(BF16)
    for sub in range(IN_TM // CHUNK):
        rows = slice(sub * CHUNK, (sub + 1) * CHUNK)
        for h in range(N_HEADS_A):
            cols = slice(h * HEAD_DIM, (h + 1) * HEAD_DIM)
            mixed = _dot(wmix_ref[h], vab[rows, cols]) + bmix_ref[:, h:h + 1]
            aout_ref[rows, cols] = (u[rows, cols] * mixed).astype(BF16)

    cq, saq, sbq = cq_ref[...], saq_ref[...], sbq_ref[...]
    zq = _dot(xb, w_ref[:, OFF_Q:OFF_K])
    for h in range(N_HEADS_B):
        cols = slice(h * HEAD_DIM, (h + 1) * HEAD_DIM)
        r = _rope_lanes(zq[:, cols], cq, saq, sbq, ROT_DIM // 2)
        q_ref[:, cols] = (r * (HEAD_DIM ** -0.5)).astype(BF16)
    zk = _dot(xb, w_ref[:, OFF_K:OFF_VB])
    for h in range(N_KV_B):
        cols = slice(h * HEAD_DIM, (h + 1) * HEAD_DIM)
        r = _rope_lanes(zk[:, cols], cq, saq, sbq, ROT_DIM // 2)
        k_ref[:, cols] = r
        kb_ref[:, cols] = r.astype(BF16)
    zvb = _dot(xb, w_ref[:, OFF_VB:OFF_QI])
    vb_ref[...] = zvb
    vbb_ref[...] = zvb.astype(BF16)

    ci, sai, sbi = ci_ref[...], sai_ref[...], sbi_ref[...]
    zqi = _dot(xb, w_ref[:, OFF_QI:OFF_KI])
    for p in range(N_IDX_HEADS * IDX_DIM // LANES):
        cols = slice(p * LANES, (p + 1) * LANES)
        qi_ref[:, cols] = _rope_lanes(zqi[:, cols], ci, sai, sbi, IDX_ROT_DIM // 2).astype(BF16)
    zl = _dot(xb, w_ref[:, OFF_KI:IN_COLS_PAD])
    r = _rope_lanes(zl, ci, sai, sbi, IDX_ROT_DIM // 2)
    ki_ref[...] = r[:, :IDX_DIM]
    lane = lax.broadcasted_iota(jnp.int32, r.shape, 1)
    kib_ref[...] = jnp.where(lane < IDX_DIM, r, pltpu.roll(r, IDX_DIM, 1)).astype(BF16)
    wi_ref[...] = zl[:, IDX_DIM:IDX_DIM + N_IDX_HEADS] * (N_IDX_HEADS ** -0.5 * IDX_DIM ** -0.5)


def _rope_tables(pos, rot_dim, head_dim):
    half = rot_dim // 2
    inv = ROPE_THETA ** (-jnp.arange(half, dtype=F32) / half)
    ang = pos.astype(F32)[:, None] * inv[None, :]
    cos, sin = jnp.cos(ang), jnp.sin(ang)
    lane = jnp.arange(LANES) % head_dim
    cos_l = jnp.take(cos, lane % half, axis=1)
    sin_l = jnp.take(sin, lane % half, axis=1)
    c = jnp.where(lane[None] < rot_dim, cos_l, 1.0)
    s_lo = jnp.where(lane[None] < half, -sin_l, 0.0)
    s_hi = jnp.where((lane[None] >= half) & (lane[None] < rot_dim), sin_l, 0.0)
    return c.astype(F32), s_lo.astype(F32), s_hi.astype(F32)


def _in_proj(x_p, x_s, tiles_per_seq, norm_mix, w_in_b, norm_v_a, wmix, bmix, tables):
    d_model = x_p.shape[1]
    n_prompt_tiles = x_p.shape[0] // IN_TM
    n_rows = x_p.shape[0] + x_s.shape[0]
    n_tiles = n_rows // IN_TM
    row = lambda i: (i, 0)
    const = lambda i: (0, 0)
    tab = lambda i: (jnp.where(i < n_prompt_tiles, i % tiles_per_seq, tiles_per_seq), 0)
    grp = lambda i: (jnp.where(i < n_prompt_tiles, 0, 1), 0, 0, 0)
    grp2 = lambda i: (jnp.where(i < n_prompt_tiles, 0, 1), 0, 0)
    tab_spec = pl.BlockSpec((IN_TM, LANES), tab)
    outs = [
        ((n_rows, D_A), BF16),
        ((n_rows, D_B), BF16),
        ((n_rows, D_KV_B), F32),
        ((n_rows, D_KV_B), BF16),
        ((n_rows, D_KV_B), F32),
        ((n_rows, D_KV_B), BF16),
        ((n_rows, N_IDX_HEADS * IDX_DIM), BF16),
        ((n_rows, IDX_DIM), F32),
        ((n_rows, LANES), BF16),
        ((n_rows, N_IDX_HEADS), F32),
        ((n_rows, D_A), F32),
    ]
    return pl.pallas_call(
        functools.partial(_in_proj_kernel, n_prompt_tiles),
        grid=(n_tiles,),
        in_specs=_group_specs(n_prompt_tiles, IN_TM, d_model) + [
            pl.BlockSpec((1, d_model), const),
            pl.BlockSpec((d_model, IN_COLS_PAD), const, pipeline_mode=pl.Buffered(1)),
            pl.BlockSpec((1, D_A), const),
            pl.BlockSpec((None, N_HEADS_A, CHUNK, CHUNK), grp),
            pl.BlockSpec((None, CHUNK, N_HEADS_A), grp2),
            tab_spec, tab_spec, tab_spec, tab_spec, tab_spec, tab_spec,
        ],
        out_specs=[pl.BlockSpec((IN_TM, s[1]), row) for s, _ in outs],
        out_shape=[jax.ShapeDtypeStruct(s, d) for s, d in outs],
        compiler_params=_params(("parallel",)),
        name="in_proj",
    )(x_p, x_s, norm_mix, w_in_b, norm_v_a, wmix, bmix, *tables)


ORDER_KEY_NEG_INF = -(2 ** 31) + 0x7FFFFF
KTH_SEARCH_UNROLL = 4


def _order_key_to_float(key):
    return pltpu.bitcast(key ^ ((key >> 31) & jnp.int32(0x7FFFFFFF)), F32)


def _kth_largest(score, k_top):
    def body(it, t):
        cand = t + lax.shift_left(jnp.int32(1), jnp.int32(31) - it)
        cnt = jnp.sum((score >= _order_key_to_float(cand)).astype(F32), axis=1, keepdims=True)
        return jnp.where(cnt >= k_top, cand, t)

    t0 = jnp.full((score.shape[0], 1), INT_MIN, jnp.int32)
    t = lax.fori_loop(0, 32, body, t0, unroll=KTH_SEARCH_UNROLL)
    return _order_key_to_float(jnp.maximum(t, ORDER_KEY_NEG_INF))


def _topk_select(score, k_top, tri):
    thr = _kth_largest(score, k_top)
    above = score > thr
    need = k_top - jnp.sum(above.astype(F32), axis=1, keepdims=True)
    tie = score == thr
    pieces = []
    run = jnp.zeros_like(need)
    for j in range(score.shape[1] // LANES):
        cols = slice(j * LANES, (j + 1) * LANES)
        tie_j = tie[:, cols]
        prefix = _dot(tie_j.astype(BF16), tri) + run
        run = prefix[:, LANES - 1:LANES]
        pieces.append(jnp.where(above[:, cols] | (tie_j & (prefix <= need)), 1.0, 0.0))
    return jnp.concatenate(pieces, axis=1)


def _tri_matrix():
    r = lax.broadcasted_iota(jnp.int32, (LANES, LANES), 0)
    c = lax.broadcasted_iota(jnp.int32, (LANES, LANES), 1)
    return (r <= c).astype(BF16)


PA_TQ = 128


PA_EXTENTS = 8


def _prompt_attn_kernel(k_top, qi_ref, wi_ref, kib_ref, q_ref, kb_ref, vbb_ref, o_ref):
    i = pl.program_id(1)
    step = kb_ref.shape[0] // PA_EXTENTS
    blocks_per_step = step // PA_TQ
    for c in range(PA_EXTENTS):
        @pl.when(i // blocks_per_step == c)
        def _():
            _prompt_attn_block(k_top, (c + 1) * step, i, qi_ref, wi_ref, kib_ref, q_ref, kb_ref, vbb_ref, o_ref)


def _prompt_attn_block(k_top, seq, i, qi_ref, wi_ref, kib_ref, q_ref, kb_ref, vbb_ref, o_ref):
    qi = qi_ref[...]
    wi = wi_ref[...]
    ki2 = kib_ref[:seq, :]
    lane = lax.broadcasted_iota(jnp.int32, (PA_TQ, LANES), 1)
    score = jnp.zeros((PA_TQ, seq), F32)
    for h in range(N_IDX_HEADS):
        pair = qi[:, (h // 2) * LANES:(h // 2 + 1) * LANES]
        keep = (lane < IDX_DIM) if h % 2 == 0 else (lane >= IDX_DIM)
        d = _dot_nt(jnp.where(keep, pair, jnp.zeros_like(pair)), ki2)
        score = score + wi[:, h:h + 1] * jnp.maximum(d, 0.0)
    q_pos = i * PA_TQ + lax.broadcasted_iota(jnp.int32, (PA_TQ, seq), 0)
    s_pos = lax.broadcasted_iota(jnp.int32, (PA_TQ, seq), 1)
    causal = s_pos <= q_pos
    sel = _topk_select(jnp.where(causal, score, NEG_INF), k_top, _tri_matrix())
    bias = jnp.where(causal & (sel > 0.0), 0.0, NEG_INF)

    q = q_ref[...]
    for g in range(N_KV_B):
        q4 = jnp.concatenate(
            [q[:, (g * KV_GROUP + r) * HEAD_DIM:(g * KV_GROUP + r + 1) * HEAD_DIM] for r in range(KV_GROUP)], axis=0)
        kg = kb_ref[:seq, g * HEAD_DIM:(g + 1) * HEAD_DIM]
        vg = vbb_ref[:seq, g * HEAD_DIM:(g + 1) * HEAD_DIM]
        s = _dot_nt(q4, kg).reshape(KV_GROUP, PA_TQ, seq) + bias[None]
        m = jnp.max(s, axis=-1, keepdims=True)
        p = jnp.exp(s - m)
        l = jnp.sum(p, axis=-1, keepdims=True)
        o = _dot(p.reshape(KV_GROUP * PA_TQ, seq).astype(BF16), vg).reshape(KV_GROUP, PA_TQ, HEAD_DIM) / l
        for r in range(KV_GROUP):
            o_ref[:, (g * KV_GROUP + r) * HEAD_DIM:(g * KV_GROUP + r + 1) * HEAD_DIM] = o[r].astype(BF16)


def _prompt_attn(n_batch, seq, k_top, qi, wi, kib, q, kb, vbb):
    nq = seq // PA_TQ
    qrow = lambda b, i: (b * nq + i, 0)
    krow = lambda b, i: (b, 0)
    return pl.pallas_call(
        functools.partial(_prompt_attn_kernel, k_top),
        grid=(n_batch, nq),
        in_specs=[
            pl.BlockSpec((PA_TQ, N_IDX_HEADS * IDX_DIM), qrow),
            pl.BlockSpec((PA_TQ, N_IDX_HEADS), qrow),
            pl.BlockSpec((seq, LANES), krow),
            pl.BlockSpec((PA_TQ, D_B), qrow),
            pl.BlockSpec((seq, D_KV_B), krow),
            pl.BlockSpec((seq, D_KV_B), krow),
        ],
        out_specs=pl.BlockSpec((PA_TQ, D_B), qrow),
        out_shape=jax.ShapeDtypeStruct((n_batch * seq, D_B), BF16),
        compiler_params=_params(("parallel", "parallel")),
        name="prompt_attn",
    )(qi, wi, kib, q, kb, vbb)


SP_PAGES = 16
SP_KEYS = SP_PAGES * PAGE_SIZE
SC_PAGES = 32
SC_KEYS = SC_PAGES * PAGE_SIZE
SP_TPAD = SUBLANES


def _page_specs(page_shape, n_steps, layer, pages):
    def make(p):
        def index(b, j, pt):
            return (layer, pt[b, jnp.minimum(j, n_steps - 1) * pages + p], 0, 0)
        return pl.BlockSpec((None, None) + page_shape, index)
    return [make(p) for p in range(pages)]


def _sample_scores_kernel(n_steps, pt_ref, qs_ref, ws_ref, kin_ref, *refs):
    page_refs, o_ref = refs[:SC_PAGES], refs[SC_PAGES]
    j = pl.program_id(1)
    qs = qs_ref[...]
    ws = ws_ref[...]

    def head_sum(d):
        w = ws * jnp.maximum(d, 0.0)
        return jnp.sum(w.reshape(N_IDX_HEADS, SP_TPAD, d.shape[1]), axis=0)

    @pl.when(j < n_steps)
    def _():
        keys_t = jnp.concatenate([r[...] for r in page_refs], axis=1).astype(BF16)
        o_ref[...] = head_sum(_dot(qs, keys_t))

    @pl.when(j == n_steps)
    def _():
        d = _dot_nt(qs, kin_ref[...])
        o_ref[...] = jnp.concatenate(
            [head_sum(d), jnp.zeros((SP_TPAD, SC_KEYS - d.shape[1]), F32)], axis=1)


def _sample_scores(page_table, qs, ws, ki_new, cache_idx_k, layer):
    n_db, n_pages = page_table.shape
    n_steps = n_pages // SC_PAGES
    rows = N_IDX_HEADS * SP_TPAD
    per_b = lambda b, j, pt: (b, 0, 0)
    grid_spec = pltpu.PrefetchScalarGridSpec(
        num_scalar_prefetch=1,
        grid=(n_db, n_steps + 1),
        in_specs=[
            pl.BlockSpec((None, rows, IDX_DIM), per_b),
            pl.BlockSpec((None, rows, 1), per_b),
            pl.BlockSpec((None, LANES, IDX_DIM), per_b),
        ] + _page_specs((IDX_DIM, PAGE_SIZE), n_steps, layer, SC_PAGES),
        out_specs=pl.BlockSpec((None, SP_TPAD, SC_KEYS), lambda b, j, pt: (b, 0, j)),
    )
    return pl.pallas_call(
        functools.partial(_sample_scores_kernel, n_steps),
        grid_spec=grid_spec,
        out_shape=jax.ShapeDtypeStruct((n_db, SP_TPAD, (n_steps + 1) * SC_KEYS), F32),
        compiler_params=_params(("parallel", "arbitrary")),
        name="sample_scores",
    )(page_table, qs, ws, ki_new, *([cache_idx_k] * SC_PAGES))


SM_ROWS = 64


def _sample_mask_kernel(k_top, past, n_new, s_ref, o_ref):
    score = s_ref[...]
    t = lax.broadcasted_iota(jnp.int32, score.shape, 0) % SP_TPAD
    s_pos = lax.broadcasted_iota(jnp.int32, score.shape, 1)
    admissible = s_pos <= past + t
    sel = _topk_select(jnp.where(admissible, score, NEG_INF), k_top, _tri_matrix())
    keep = jnp.where((admissible & (sel > 0.0)) | (t >= n_new), 1.0, 0.0).astype(BF16)
    r = lax.broadcasted_iota(jnp.int32, (LANES, N_KV_B * LANES), 0)
    c = lax.broadcasted_iota(jnp.int32, (LANES, N_KV_B * LANES), 1)
    spread = (c // N_KV_B == r).astype(BF16)
    for j in range(score.shape[1] // LANES):
        d = _dot(keep[:, j * LANES:(j + 1) * LANES], spread)
        o_ref[:, j * N_KV_B * LANES:(j + 1) * N_KV_B * LANES] = jnp.where(d > 0.5, 0.0, NEG_INF)


def _sample_mask(scores2d, k_top, past, n_new):
    n_rows, width = scores2d.shape
    return pl.pallas_call(
        functools.partial(_sample_mask_kernel, k_top, past, n_new),
        grid=(n_rows // SM_ROWS,),
        in_specs=[pl.BlockSpec((SM_ROWS, width), lambda i: (i, 0))],
        out_specs=pl.BlockSpec((SM_ROWS, N_KV_B * width), lambda i: (i, 0)),
        out_shape=jax.ShapeDtypeStruct((n_rows, N_KV_B * width), F32),
        compiler_params=_params(("parallel",)),
        name="sample_mask",
    )(scores2d)


SA_ROWS = KV_GROUP * SP_TPAD
SA_ROWS_ALL = N_KV_B * SA_ROWS
SA_PAGE_ROWS = PAGE_SIZE * N_KV_B


def _sample_attn_kernel(n_steps, pt_ref, q_ref, bias_ref, kn_ref, vn_ref, *refs):
    k_refs, v_refs = refs[:SP_PAGES], refs[SP_PAGES:2 * SP_PAGES]
    o_ref, m_ref, l_ref, acc_ref = refs[2 * SP_PAGES:]
    j = pl.program_id(1)

    @pl.when(j == 0)
    def _():
        m_ref[...] = jnp.full(m_ref.shape, NEG_INF, F32)
        l_ref[...] = jnp.zeros(l_ref.shape, F32)
        acc_ref[...] = jnp.zeros(acc_ref.shape, F32)

    def update(kc, vc, bias):
        n = kc.shape[0]
        s = _dot_nt(q_ref[...], kc)
        row_head = lax.broadcasted_iota(jnp.int32, (SA_ROWS_ALL, n), 0) // SA_ROWS
        col_head = lax.broadcasted_iota(jnp.int32, (SA_ROWS_ALL, n), 1) % N_KV_B
        s = jnp.where(row_head == col_head, s + jnp.concatenate([bias] * (SA_ROWS_ALL // SP_TPAD), axis=0), NEG_INF)
        m_old = m_ref[...]
        m_new = jnp.maximum(m_old, jnp.max(s, axis=-1, keepdims=True))
        m_safe = jnp.where(m_new == NEG_INF, 0.0, m_new)
        alpha = jnp.exp(m_old - m_safe)
        p = jnp.exp(s - m_safe)
        l_ref[...] = alpha * l_ref[...] + jnp.sum(p, axis=-1, keepdims=True)
        acc_ref[...] = alpha * acc_ref[...] + _dot(p.astype(BF16), vc)
        m_ref[...] = m_new

    @pl.when(j < n_steps)
    def _():
        kc = jnp.concatenate([r[...] for r in k_refs], axis=0).astype(BF16)
        vc = jnp.concatenate([r[...] for r in v_refs], axis=0).astype(BF16)
        update(kc, vc, bias_ref[...])

    @pl.when(j == n_steps)
    def _():
        update(kn_ref[...], vn_ref[...], bias_ref[:, :SA_PAGE_ROWS])
        o_ref[...] = acc_ref[...] / l_ref[...]


def _sample_attn(page_table, qs, bias, k_new, v_new, cache_k, cache_v, layer):
    n_db, n_pages = page_table.shape
    n_steps = n_pages // SP_PAGES
    per_b = lambda b, j, pt: (b, 0, 0)
    grid_spec = pltpu.PrefetchScalarGridSpec(
        num_scalar_prefetch=1,
        grid=(n_db, n_steps + 1),
        in_specs=[
            pl.BlockSpec((None, SA_ROWS_ALL, HEAD_DIM), per_b),
            pl.BlockSpec((None, SP_TPAD, N_KV_B * SP_KEYS), lambda b, j, pt: (b, 0, j)),
            pl.BlockSpec((None, SA_PAGE_ROWS, HEAD_DIM), per_b),
            pl.BlockSpec((None, SA_PAGE_ROWS, HEAD_DIM), per_b),
        ] + 2 * _page_specs((SA_PAGE_ROWS, HEAD_DIM), n_steps, layer, SP_PAGES),
        out_specs=pl.BlockSpec((None, SA_ROWS_ALL, HEAD_DIM), per_b),
        scratch_shapes=[
            pltpu.VMEM((SA_ROWS_ALL, 1), F32),
            pltpu.VMEM((SA_ROWS_ALL, 1), F32),
            pltpu.VMEM((SA_ROWS_ALL, HEAD_DIM), F32),
        ],
    )
    return pl.pallas_call(
        functools.partial(_sample_attn_kernel, n_steps),
        grid_spec=grid_spec,
        out_shape=jax.ShapeDtypeStruct((n_db, SA_ROWS_ALL, HEAD_DIM), F32),
        compiler_params=_params(("parallel", "arbitrary")),
        name="sample_attn",
    )(page_table, qs, bias, k_new, v_new, *([cache_k] * SP_PAGES), *([cache_v] * SP_PAGES))


OR_TM = 256


def _split_bf16(x):
    hi = x.astype(BF16)
    return hi, (x - hi.astype(F32)).astype(BF16)


def _out_router_kernel(n_prompt_tiles, a_ref, b_ref, xp_ref, xs_ref, wo_ref, nf_ref, wrh_ref, wrl_ref, br_ref,
                       xmid_ref, h_ref, idx_ref, gate_ref):
    x = _group_rows(n_prompt_tiles, xp_ref, xs_ref)
    x_mid = x + _dot(a_ref[...], wo_ref[:D_A, :]) + _dot(b_ref[...], wo_ref[D_A:, :])
    xmid_ref[...] = x_mid
    h = (x_mid * lax.rsqrt(jnp.mean(x_mid * x_mid, axis=-1, keepdims=True) + RMS_EPS)) * nf_ref[...]
    h_ref[...] = h
    h_hi, h_lo = _split_bf16(h)
    wr_hi = wrh_ref[...]
    logits = _dot(h_hi, wr_hi) + _dot(h_lo, wr_hi) + _dot(h_hi, wrl_ref[...]) + br_ref[...]
    lane = lax.broadcasted_iota(jnp.int32, logits.shape, 1)
    vals, idxs = [], []
    for _ in range(TOP_K_EXPERTS):
        m = jnp.max(logits, axis=-1, keepdims=True)
        i = jnp.min(jnp.where(logits == m, lane, LANES), axis=-1, keepdims=True)
        vals.append(m)
        idxs.append(i)
        logits = jnp.where(lane == i, NEG_INF, logits)
    exps = [jnp.exp(v - vals[0]) for v in vals]
    total = exps[0]
    for e in exps[1:]:
        total = total + e
    idx_out = jnp.zeros(logits.shape, jnp.int32)
    gate_out = jnp.zeros(logits.shape, F32)
    for k in range(TOP_K_EXPERTS):
        idx_out = jnp.where(lane == k, idxs[k], idx_out)
        gate_out = jnp.where(lane == k, exps[k] / total, gate_out)
    idx_ref[...] = idx_out
    gate_ref[...] = gate_out


def _out_router(a_out, b_out, x_p, x_s, w_out_b, norm_ffn, wr_hi, wr_lo, br_pad):
    d_model = x_p.shape[1]
    n_prompt_tiles = x_p.shape[0] // OR_TM
    n_rows = x_p.shape[0] + x_s.shape[0]
    row = lambda i: (i, 0)
    const = lambda i: (0, 0)
    return pl.pallas_call(
        functools.partial(_out_router_kernel, n_prompt_tiles),
        grid=(n_rows // OR_TM,),
        in_specs=[
            pl.BlockSpec((OR_TM, D_A), row),
            pl.BlockSpec((OR_TM, D_B), row),
        ] + _group_specs(n_prompt_tiles, OR_TM, d_model) + [
            pl.BlockSpec((D_A + D_B, d_model), const),
            pl.BlockSpec((1, d_model), const),
            pl.BlockSpec((d_model, LANES), const),
            pl.BlockSpec((d_model, LANES), const),
            pl.BlockSpec((1, LANES), const),
        ],
        out_specs=[
            pl.BlockSpec((OR_TM, d_model), row),
            pl.BlockSpec((OR_TM, d_model), row),
            pl.BlockSpec((OR_TM, LANES), row),
            pl.BlockSpec((OR_TM, LANES), row),
        ],
        out_shape=[
            jax.ShapeDtypeStruct((n_rows, d_model), F32),
            jax.ShapeDtypeStruct((n_rows, d_model), F32),
            jax.ShapeDtypeStruct((n_rows, LANES), jnp.int32),
            jax.ShapeDtypeStruct((n_rows, LANES), F32),
        ],
        compiler_params=_params(("parallel",)),
        name="out_router",
    )(a_out, b_out, x_p, x_s, w_out_b, norm_ffn, wr_hi, wr_lo, br_pad)


MOE_TM = 256
MOE_UNIT_TILES = 5
MOE_R = MOE_TM * MOE_UNIT_TILES
MOE_TF = 256
COMBINE_TM = 128
ROW_COPY_UNROLL = 8


def _row_copy(src_ref, src_row, dst_ref, dst_row, sem):
    return pltpu.make_async_copy(src_ref.at[pl.ds(src_row, 1), :], dst_ref.at[pl.ds(dst_row, 1), :], sem)


MOE_HALF = MOE_TM // 2


def _pieces(rows):
    return pl.cdiv(rows, MOE_HALF)


def _moe_kernel(ue_ref, ur_ref, ub_ref, tok_ref, tokn_ref, h_ref, wg_ref, wu_ref, bg_ref, bu_ref, wd_ref,
                bd_ref, ys_ref, land_ref, xb_ref, y_ref, wgb_ref, wub_ref, wdb_ref, gsem, ysem):
    u, f = pl.program_id(0), pl.program_id(1)
    n_units, n_f = pl.num_programs(0), pl.num_programs(1)
    rows = ur_ref[u]
    n_pieces = _pieces(rows)
    rows_next = jnp.where(u + 1 < n_units, ur_ref[jnp.minimum(u + 1, n_units - 1)], 0)

    def gather(tokens_ref, n, wait):
        def piece(m, c):
            def body(j, c2):
                r = m * MOE_HALF + j
                cp = _row_copy(h_ref, tokens_ref[0, r], land_ref, r, gsem)
                cp.wait() if wait else cp.start()
                return c2
            return lax.fori_loop(0, MOE_HALF, body, c, unroll=ROW_COPY_UNROLL)
        lax.fori_loop(0, n, piece, 0)

    def writeback(unit, n, wait):
        def body(m, c):
            r0 = pl.multiple_of(m * MOE_HALF, MOE_HALF)
            cp = pltpu.make_async_copy(y_ref.at[pl.ds(r0, MOE_HALF), :],
                                       ys_ref.at[pl.ds(ub_ref[unit] * MOE_R + r0, MOE_HALF), :], ysem)
            cp.wait() if wait else cp.start()
            return c
        lax.fori_loop(0, n, body, 0)

    @pl.when((f == 0) & (rows > 0))
    def _():
        @pl.when(u == 0)
        def _():
            gather(tok_ref, n_pieces, wait=False)

        gather(tok_ref, n_pieces, wait=True)

        def convert(m, c):
            r0 = pl.multiple_of(m * MOE_HALF, MOE_HALF)
            xb_ref[pl.ds(r0, MOE_HALF), :] = land_ref[pl.ds(r0, MOE_HALF), :].astype(BF16)
            return c
        lax.fori_loop(0, n_pieces, convert, 0)

        gather(tokn_ref, _pieces(rows_next), wait=False)

    @pl.when((f == 0) & (u > 0))
    def _():
        writeback(u - 1, _pieces(ur_ref[jnp.maximum(u - 1, 0)]), wait=True)

    @pl.when(rows > 0)
    def _():
        wgb_ref[...] = wg_ref[...].astype(BF16)
        wub_ref[...] = wu_ref[...].astype(BF16)
        wdb_ref[...] = wd_ref[...].astype(BF16)

        def tile_at(m, size):
            r0 = pl.multiple_of(m * MOE_TM, MOE_TM)
            xm = xb_ref[pl.ds(r0, size), :]
            gate = jnp.minimum(_dot(xm, wgb_ref[...]) + bg_ref[...], SWIGLU_LIMIT)
            up = jnp.clip(_dot(xm, wub_ref[...]) + bu_ref[...], -SWIGLU_LIMIT, SWIGLU_LIMIT)
            act = (up + 1.0) * (gate * jax.nn.sigmoid(SWIGLU_ALPHA * gate))
            y_ref[pl.ds(r0, size), :] += _dot(act.astype(BF16), wdb_ref[...])

        def tile(m, c):
            tile_at(m, MOE_TM)
            return c

        @pl.when(f == 0)
        def _():
            def init(m, c):
                r0 = pl.multiple_of(m * MOE_HALF, MOE_HALF)
                y_ref[pl.ds(r0, MOE_HALF), :] = jnp.broadcast_to(bd_ref[...], (MOE_HALF, y_ref.shape[1]))
                return c
            lax.fori_loop(0, n_pieces, init, 0)

        def tile_pair(p, c):
            tile(2 * p, c)
            return tile(2 * p + 1, c)

        n_full = n_pieces // 2
        lax.fori_loop(0, n_full // 2, tile_pair, 0)

        @pl.when(n_full % 2 == 1)
        def _():
            tile(n_full - 1, 0)

        @pl.when(n_pieces % 2 == 1)
        def _():
            tile_at(n_full, MOE_HALF)

        @pl.when(f == n_f - 1)
        def _():
            writeback(u, n_pieces, wait=False)

    @pl.when((f == n_f - 1) & (u == n_units - 1))
    def _():
        writeback(u, n_pieces, wait=True)


def _moe_experts(unit_expert, unit_rows, unit_block, slot_token, h, w_gate_up, b_gate_up, w_down, b_down):
    n_units = unit_expert.shape[0]
    d_model = h.shape[1]
    d_ff = w_down.shape[1]
    n_f = d_ff // MOE_TF

    def f_eff(u, f, ur):
        return jnp.where(ur[u] > 0, f, n_f - 1)

    smem_tokens = lambda index: pl.BlockSpec((None, 1, MOE_R), index, memory_space=pltpu.SMEM)
    grid_spec = pltpu.PrefetchScalarGridSpec(
        num_scalar_prefetch=3,
        grid=(n_units, n_f),
        in_specs=[
            smem_tokens(lambda u, f, ue, ur, ub: (ub[u], 0, 0)),
            smem_tokens(lambda u, f, ue, ur, ub: (ub[jnp.minimum(u + 1, n_units - 1)], 0, 0)),
            pl.BlockSpec(memory_space=pl.ANY),
            pl.BlockSpec((None, d_model, MOE_TF), lambda u, f, ue, ur, ub: (ue[u], 0, f_eff(u, f, ur))),
            pl.BlockSpec((None, d_model, MOE_TF), lambda u, f, ue, ur, ub: (ue[u], 0, n_f + f_eff(u, f, ur))),
            pl.BlockSpec((None, 1, MOE_TF), lambda u, f, ue, ur, ub: (ue[u], 0, f_eff(u, f, ur))),
            pl.BlockSpec((None, 1, MOE_TF), lambda u, f, ue, ur, ub: (ue[u], 0, n_f + f_eff(u, f, ur))),
            pl.BlockSpec((None, MOE_TF, d_model), lambda u, f, ue, ur, ub: (ue[u], f_eff(u, f, ur), 0)),
            pl.BlockSpec((None, 1, d_model), lambda u, f, ue, ur, ub: (ue[u], 0, 0)),
        ],
        out_specs=pl.BlockSpec(memory_space=pl.ANY),
        scratch_shapes=[
            pltpu.VMEM((MOE_R, d_model), F32),
            pltpu.VMEM((MOE_R, d_model), BF16),
            pltpu.VMEM((MOE_R, d_model), F32),
            pltpu.VMEM((d_model, MOE_TF), BF16),
            pltpu.VMEM((d_model, MOE_TF), BF16),
            pltpu.VMEM((MOE_TF, d_model), BF16),
            pltpu.SemaphoreType.DMA,
            pltpu.SemaphoreType.DMA,
        ],
    )
    tokens = slot_token.reshape(n_units, 1, MOE_R)
    return pl.pallas_call(
        _moe_kernel,
        grid_spec=grid_spec,
        out_shape=jax.ShapeDtypeStruct((n_units * MOE_R, d_model), F32),
        compiler_params=_params(("arbitrary", "arbitrary")),
        name="moe_experts",
    )(unit_expert, unit_rows, unit_block, tokens, tokens, h, w_gate_up, w_gate_up, b_gate_up, b_gate_up,
      w_down, b_down)


def _combine_kernel(n_prompt_tiles, pos_ref, posn_ref, gate_ref, xmid_ref, nfin_ref, ys_ref, op_ref, os_ref,
                    buf_ref, sem):
    i, n_tiles = pl.program_id(0), pl.num_programs(0)
    slot = i % 2

    def rows(p_ref, s, wait):
        for k in range(TOP_K_EXPERTS):
            def body(t, c):
                cp = _row_copy(ys_ref, p_ref[0, t * TOP_K_EXPERTS + k], buf_ref.at[s], k * COMBINE_TM + t,
                               sem.at[s])
                cp.wait() if wait else cp.start()
                return c
            lax.fori_loop(0, COMBINE_TM, body, 0, unroll=ROW_COPY_UNROLL)

    @pl.when(i == 0)
    def _():
        rows(pos_ref, 0, wait=False)

    @pl.when(i + 1 < n_tiles)
    def _():
        rows(posn_ref, 1 - slot, wait=False)

    rows(pos_ref, slot, wait=True)
    gates = gate_ref[...]
    out = xmid_ref[...]
    for k in range(TOP_K_EXPERTS):
        out = out + gates[:, k:k + 1] * buf_ref[slot, k * COMBINE_TM:(k + 1) * COMBINE_TM, :]
    y = (out * lax.rsqrt(jnp.mean(out * out, axis=-1, keepdims=True) + RMS_EPS)) * nfin_ref[...]

    @pl.when(i < n_prompt_tiles)
    def _():
        op_ref[...] = y

    @pl.when(i >= n_prompt_tiles)
    def _():
        os_ref[...] = y


def _combine(pos, gates, x_mid, norm_final, ys, n_prompt, n_sample):
    d_model = x_mid.shape[1]
    n_prompt_tiles = n_prompt // COMBINE_TM
    n_tiles = n_prompt_tiles + n_sample // COMBINE_TM
    n = COMBINE_TM * TOP_K_EXPERTS
    pos3 = pos.reshape(n_tiles, 1, n)
    row = lambda i: (i, 0)
    return pl.pallas_call(
        functools.partial(_combine_kernel, n_prompt_tiles),
        grid=(n_tiles,),
        in_specs=[
            pl.BlockSpec((None, 1, n), lambda i: (i, 0, 0), memory_space=pltpu.SMEM),
            pl.BlockSpec((None, 1, n), lambda i: (jnp.minimum(i + 1, n_tiles - 1), 0, 0), memory_space=pltpu.SMEM),
            pl.BlockSpec((COMBINE_TM, LANES), row),
            pl.BlockSpec((COMBINE_TM, d_model), row),
            pl.BlockSpec((1, d_model), lambda i: (0, 0)),
            pl.BlockSpec(memory_space=pl.ANY),
        ],
        out_specs=[
            pl.BlockSpec((COMBINE_TM, d_model), lambda i: (jnp.minimum(i, n_prompt_tiles - 1), 0)),
            pl.BlockSpec((COMBINE_TM, d_model), lambda i: (jnp.maximum(i - n_prompt_tiles, 0), 0)),
        ],
        out_shape=[jax.ShapeDtypeStruct((n_prompt, d_model), F32), jax.ShapeDtypeStruct((n_sample, d_model), F32)],
        scratch_shapes=[pltpu.VMEM((2, n, d_model), F32), pltpu.SemaphoreType.DMA((2,))],
        compiler_params=_params(("arbitrary",)),
        name="combine",
    )(pos3, pos3, gates, x_mid, norm_final, ys)


def _routing(top_idx, n_units, zero_row):
    flat_e = top_idx.reshape(-1)
    onehot = (flat_e[:, None] == jnp.arange(N_EXPERTS, dtype=jnp.int32)[None, :]).astype(jnp.int32)
    csum = jnp.cumsum(onehot, axis=0)
    rank = jnp.take_along_axis(csum, flat_e[:, None], axis=1)[:, 0] - 1
    counts = csum[-1]
    units_e = (counts + MOE_R - 1) // MOE_R
    units_end = jnp.cumsum(units_e)
    units_start = units_end - units_e
    pos = (units_start[flat_e] + rank // MOE_R) * MOE_R + rank % MOE_R
    u = jnp.arange(n_units, dtype=jnp.int32)
    total = units_end[-1]
    ue = jnp.minimum(jnp.sum((units_end[None, :] <= u[:, None]).astype(jnp.int32), axis=1), N_EXPERTS - 1)
    rows = jnp.clip(counts[ue] - (u - units_start[ue]) * MOE_R, 0, MOE_R)
    used = u < total
    last_e = ue[jnp.maximum(total - 1, 0)]
    unit_expert = jnp.where(used, ue, last_e).astype(jnp.int32)
    unit_rows = jnp.where(used, rows, 0).astype(jnp.int32)
    unit_block = jnp.minimum(u, total - 1).astype(jnp.int32)
    pos = pos.astype(jnp.int32)
    token = jnp.arange(flat_e.shape[0], dtype=jnp.int32) // TOP_K_EXPERTS
    slot_token = jnp.full((n_units * MOE_R,), zero_row, jnp.int32).at[pos].set(token, unique_indices=True)
    return pos, slot_token, unit_expert, unit_rows, unit_block


def kernel(x_prompt, x_sample, cache_k, cache_v, cache_idx_k, page_table, norm_mix, w_in, norm_v_a,
           w_spatial, b_spatial, w_out, norm_ffn, w_router, b_router, w_gate_up, b_gate_up, w_down,
           b_down, norm_final):
    n_b, seq, d_model = x_prompt.shape
    n_db, n_new, _ = x_sample.shape
    depth = w_in.shape[0]
    past = page_table.shape[1] * PAGE_SIZE
    n_prompt, n_sample = n_b * seq, n_db * n_new
    n_tokens = n_prompt + n_sample
    assert seq % IN_TM == 0 and seq % PA_TQ == 0 and CHUNK % n_new == 0 and n_new <= SP_TPAD
    assert n_sample % CHUNK == 0 and page_table.shape[1] % SP_PAGES == 0 and page_table.shape[1] % SC_PAGES == 0
    assert n_prompt % COMBINE_TM == 0 and n_sample % COMBINE_TM == 0
    assert depth == 1, "the final norm is fused into the last layer's combine; deeper stacks are not wired up"
    n_rows = (n_tokens // IN_TM + 1) * IN_TM
    k_top_prompt = min(TOPK_MAX, seq // 4)
    k_top_sample = min(TOPK_MAX, (past + n_new) // 4)
    n_assign = n_tokens * TOP_K_EXPERTS
    n_units = N_EXPERTS + n_assign // MOE_R

    pos_rows = jnp.concatenate([jnp.arange(seq, dtype=jnp.int32),
                                past + jnp.arange(IN_TM, dtype=jnp.int32) % n_new])
    tables = _rope_tables(pos_rows, ROT_DIM, HEAD_DIM) + _rope_tables(pos_rows, IDX_ROT_DIM, IDX_DIM)

    x_p = x_prompt.reshape(n_prompt, d_model)
    x_s = jnp.pad(x_sample.reshape(n_sample, d_model), ((0, n_rows - n_tokens), (0, 0)))
    outs = []
    for l in range(depth):
        causal = jnp.tril(jnp.ones((CHUNK, CHUNK), bool))
        w_prompt = jnp.where(causal[None], w_spatial[l], 0.0)
        eye = jnp.eye(CHUNK // n_new, dtype=F32)
        w_decode = jax.vmap(lambda w: jnp.kron(eye, w))(w_prompt[:, :n_new, :n_new])
        wmix = jnp.stack([w_prompt, w_decode]).astype(BF16)
        b_decode = jnp.tile(b_spatial[l][:, :n_new], (1, CHUNK // n_new))
        bmix = jnp.stack([b_spatial[l].T, b_decode.T])
        w_in_b = jnp.pad(w_in[l], ((0, 0), (0, IN_COLS_PAD - IN_COLS))).astype(BF16)

        (a_out, q, k, kb, vb, vbb, qi, ki, kib, wi, va) = _in_proj(
            x_p, x_s, seq // IN_TM, norm_mix[l][None], w_in_b, norm_v_a[l][None], wmix, bmix, tables)

        b_prompt = _prompt_attn(n_b, seq, k_top_prompt, qi, wi, kib, q, kb, vbb)

        sl = slice(n_prompt, n_tokens)
        pad_t = ((0, 0), (0, 0), (0, SP_TPAD - n_new), (0, 0))
        qi_s = qi[sl].reshape(n_db, n_new, N_IDX_HEADS, IDX_DIM).transpose(0, 2, 1, 3)
        qs_idx = jnp.pad(qi_s, pad_t).reshape(n_db, N_IDX_HEADS * SP_TPAD, IDX_DIM)
        wi_s = wi[sl].reshape(n_db, n_new, N_IDX_HEADS).transpose(0, 2, 1)
        ws_idx = jnp.pad(wi_s, ((0, 0), (0, 0), (0, SP_TPAD - n_new))).reshape(n_db, N_IDX_HEADS * SP_TPAD, 1)
        pad_rows = ((0, 0), (0, LANES - n_new), (0, 0))
        ki_new = jnp.pad(kib[sl, :IDX_DIM].reshape(n_db, n_new, IDX_DIM), pad_rows)
        scores = _sample_scores(page_table, qs_idx, ws_idx, ki_new, jnp.swapaxes(cache_idx_k, 2, 3), l)
        width = scores.shape[-1]
        bias = _sample_mask(scores.reshape(n_db * SP_TPAD, width), k_top_sample, past, n_new)
        q_s = q[sl].reshape(n_db, n_new, N_KV_B, KV_GROUP, HEAD_DIM).transpose(0, 2, 3, 1, 4)
        qs_att = jnp.pad(q_s, ((0, 0), (0, 0), (0, 0), (0, SP_TPAD - n_new), (0, 0))).reshape(
            n_db, SA_ROWS_ALL, HEAD_DIM)
        pad_kv = ((0, 0), (0, SA_PAGE_ROWS - n_new * N_KV_B), (0, 0))
        k_new = jnp.pad(kb[sl].reshape(n_db, n_new * N_KV_B, HEAD_DIM), pad_kv)
        v_new = jnp.pad(vbb[sl].reshape(n_db, n_new * N_KV_B, HEAD_DIM), pad_kv)
        n_pool = cache_k.shape[1]
        o_s = _sample_attn(page_table, qs_att, bias.reshape(n_db, SP_TPAD, N_KV_B * width), k_new, v_new,
                           cache_k.reshape(depth, n_pool, SA_PAGE_ROWS, HEAD_DIM),
                           cache_v.reshape(depth, n_pool, SA_PAGE_ROWS, HEAD_DIM), l)
        b_sample = o_s.reshape(n_db, N_KV_B, KV_GROUP, SP_TPAD, HEAD_DIM)[:, :, :, :n_new]
        b_sample = b_sample.transpose(0, 3, 1, 2, 4).reshape(n_sample, D_B).astype(BF16)
        b_out = jnp.concatenate([b_prompt, b_sample, jnp.zeros((n_rows - n_tokens, D_B), BF16)])

        wr_pad = jnp.pad(w_router[l], ((0, 0), (0, LANES - N_EXPERTS)))
        wr_hi, wr_lo = _split_bf16(wr_pad)
        br_pad = jnp.pad(b_router[l], (0, LANES - N_EXPERTS), constant_values=NEG_INF)[None]
        x_mid, h, top_idx, gates = _out_router(a_out, b_out, x_p, x_s, w_out[l].astype(BF16),
                                               norm_ffn[l][None], wr_hi, wr_lo, br_pad)

        pos, slot_token, unit_expert, unit_rows, unit_block = _routing(
            top_idx[:n_tokens, :TOP_K_EXPERTS], n_units, zero_row=n_tokens)
        ys = _moe_experts(unit_expert, unit_rows, unit_block, slot_token, h, w_gate_up[l],
                          b_gate_up[l][:, None, :], w_down[l], b_down[l][:, None, :])
        y_p, y_s = _combine(pos, gates, x_mid, norm_final[None], ys, n_prompt, n_sample)
        outs.append((k, vb, ki, va))

    def stack(j, lo, hi, shape):
        return jnp.stack([o[j][lo:hi].reshape(shape) for o in outs])

    return (
        y_p.reshape(n_b, seq, d_model), y_s.reshape(n_db, n_new, d_model),
        stack(0, 0, n_prompt, (n_b, seq, N_KV_B, HEAD_DIM)),
        stack(1, 0, n_prompt, (n_b, seq, N_KV_B, HEAD_DIM)),
        stack(2, 0, n_prompt, (n_b, seq, IDX_DIM)),
        stack(0, n_prompt, n_tokens, (n_db, n_new, N_KV_B, HEAD_DIM)),
        stack(1, n_prompt, n_tokens, (n_db, n_new, N_KV_B, HEAD_DIM)),
        stack(2, n_prompt, n_tokens, (n_db, n_new, IDX_DIM)),
        stack(3, n_prompt, n_tokens, (n_db, n_new, N_HEADS_A, HEAD_DIM)),
    )
```
